```python
import jax
import jax.numpy as jnp
from jax import lax
import numpy as np

D_MODEL = 1024
BATCH = 16
SEQ = 2048
DEPTH = 4

N_META = 16
N_MIXERS = 3
CHUNK = 64
LEAD_PAD = (-N_META) % CHUNK
RMS_EPS = 1e-6
D_FF = 4 * D_MODEL

GDN_QK_HEADS = 8
GDN_V_HEADS = 16
GDN_HEAD_DIM = 128
GDN_CONV = 4
GDN_QK_DIM = GDN_QK_HEADS * GDN_HEAD_DIM
GDN_V_DIM = GDN_V_HEADS * GDN_HEAD_DIM
GDN_CONV_DIM = 2 * GDN_QK_DIM + GDN_V_DIM
GDN_IN = GDN_CONV_DIM + GDN_V_DIM + 2 * GDN_V_HEADS

MLSTM_HEADS = 8
MLSTM_QK_DIM = D_MODEL // 2
MLSTM_V_DIM = D_MODEL
MLSTM_DQK = MLSTM_QK_DIM // MLSTM_HEADS
MLSTM_DV = MLSTM_V_DIM // MLSTM_HEADS
MLSTM_IN = 2 * MLSTM_QK_DIM + 2 * MLSTM_V_DIM + 2 * MLSTM_HEADS
GATE_SOFTCAP = 15.0

MLA_HEADS = 16
MLA_NOPE = 64
MLA_ROPE = 32
MLA_QK = MLA_NOPE + MLA_ROPE
MLA_V = 64
MLA_Q_RANK = 384
MLA_KV_RANK = 256
MLA_IN = MLA_Q_RANK + MLA_KV_RANK + MLA_ROPE
ROPE_THETA = 10000.0
Q_BLOCK = 128

N_GDN_LAYERS = len(range(0, DEPTH, N_MIXERS))
N_MLSTM_LAYERS = len(range(1, DEPTH, N_MIXERS))
N_MLA_LAYERS = len(range(2, DEPTH, N_MIXERS))

kernel_name = 'hybrid_gdn_mlstm_mla_trunk'


def rms_norm(x, w):
    xf = x.astype(jnp.float32)
    y = xf * lax.rsqrt(jnp.mean(xf * xf, axis=-1, keepdims=True) + RMS_EPS)
    return (y * w.astype(jnp.float32)).astype(x.dtype)


def l2_norm(x):
    xf = x.astype(jnp.float32)
    return xf * lax.rsqrt(jnp.sum(xf * xf, axis=-1, keepdims=True) + RMS_EPS)


def to_chunks(x, pad_value=0.0):
    widths = [(0, 0), (LEAD_PAD, 0)] + [(0, 0)] * (x.ndim - 2)
    x = jnp.pad(x, widths, constant_values=pad_value)
    b, tp = x.shape[:2]
    x = x.reshape((b, tp // CHUNK, CHUNK) + x.shape[2:])
    return jnp.moveaxis(x, 3, 1)


def from_chunks(x):
    b, nh, n, c, d = x.shape
    x = jnp.moveaxis(x, 1, 3).reshape(b, n * c, nh, d)
    return x[:, LEAD_PAD:]


def causal_conv_silu(x, w):
    k, c = w.shape
    y = lax.conv_general_dilated(x, w[:, None, :].astype(x.dtype), window_strides=(1,),
                                 padding=[(k - 1, 0)], dimension_numbers=('NWC', 'WIO', 'NWC'),
                                 feature_group_count=c)
    return jax.nn.silu(y)


def chunk_gated_delta_rule(q, k, v, g, beta):
    q, k, v = to_chunks(q), to_chunks(k), to_chunks(v)
    g, beta = to_chunks(g), to_chunks(beta)
    dv = v.shape[-1]
    causal = jnp.tril(jnp.ones((CHUNK, CHUNK), dtype=bool))
    strict = jnp.tril(jnp.ones((CHUNK, CHUNK), dtype=bool), k=-1)
    cum = jnp.cumsum(g, axis=-1)
    decay = jnp.exp(jnp.where(causal, cum[..., :, None] - cum[..., None, :], -jnp.inf))
    k_beta = k * beta[..., None]
    a = jnp.where(strict, jnp.einsum('bhnid,bhnjd->bhnij', k_beta, k) * decay, 0.0)
    eye = jnp.eye(CHUNK, dtype=a.dtype)
    rhs = jnp.concatenate([v * beta[..., None], k_beta * jnp.exp(cum)[..., None]], axis=-1)
    sol = lax.linalg.triangular_solve(a + eye, rhs, left_side=True, lower=True, unit_diagonal=True)
    u, w = sol[..., :dv], sol[..., dv:]
    attn = jnp.einsum('bhnid,bhnjd->bhnij', q, k) * decay
    q_dec = q * jnp.exp(cum)[..., None]
    k_dec = k * jnp.exp(cum[..., -1:] - cum)[..., None]
    last = jnp.exp(cum[..., -1])

    def step(state, xs):
        u_c, w_c, q_c, k_c, attn_c, last_c = xs
        v_new = u_c - jnp.einsum('bhck,bhkv->bhcv', w_c, state)
        o = jnp.einsum('bhck,bhkv->bhcv', q_c, state) + jnp.einsum('bhij,bhjv->bhiv', attn_c, v_new)
        state = state * last_c[..., None, None] + jnp.einsum('bhck,bhcv->bhkv', k_c, v_new)
        return state, o

    xs = tuple(jnp.moveaxis(t, 2, 0) for t in (u, w, q_dec, k_dec, attn, last))
    b, nh = q.shape[:2]
    state0 = jnp.zeros((b, nh, q.shape[-1], dv), jnp.float32)
    _, o = lax.scan(step, state0, xs)
    return from_chunks(jnp.moveaxis(o, 0, 2))


def gated_deltanet(h, w_in, w_conv, a_log, dt_bias, w_norm, w_out):
    b, t, _ = h.shape
    proj = h @ w_in
    qkv, z, beta_pre, a_pre = jnp.split(
        proj, [GDN_CONV_DIM, GDN_CONV_DIM + GDN_V_DIM, GDN_CONV_DIM + GDN_V_DIM + GDN_V_HEADS], axis=-1)
    qkv = causal_conv_silu(qkv, w_conv)
    q, k, v = jnp.split(qkv, [GDN_QK_DIM, 2 * GDN_QK_DIM], axis=-1)
    rep = GDN_V_HEADS // GDN_QK_HEADS
    q = l2_norm(q.reshape(b, t, GDN_QK_HEADS, GDN_HEAD_DIM)) * (GDN_HEAD_DIM ** -0.5)
    k = l2_norm(k.reshape(b, t, GDN_QK_HEADS, GDN_HEAD_DIM))
    q = jnp.repeat(q, rep, axis=2)
    k = jnp.repeat(k, rep, axis=2)
    v = v.reshape(b, t, GDN_V_HEADS, GDN_HEAD_DIM).astype(jnp.float32)
    beta = jax.nn.sigmoid(beta_pre.astype(jnp.float32))
    g = -jnp.exp(a_log.astype(jnp.float32)) * jax.nn.softplus(a_pre.astype(jnp.float32) + dt_bias.astype(jnp.float32))
    o = chunk_gated_delta_rule(q, k, v, g, beta)
    o = rms_norm(o, w_norm) * jax.nn.silu(z.reshape(b, t, GDN_V_HEADS, GDN_HEAD_DIM).astype(jnp.float32))
    return o.reshape(b, t, GDN_V_DIM).astype(h.dtype) @ w_out


def chunk_mlstm(q, k, v, i_pre, log_f):
    q, k, v = to_chunks(q), to_chunks(k), to_chunks(v)
    i_pre = to_chunks(i_pre, -jnp.inf)
    log_f = to_chunks(log_f)
    causal = jnp.tril(jnp.ones((CHUNK, CHUNK), dtype=bool))
    cum = jnp.cumsum(log_f, axis=-1)
    log_w = jnp.where(causal, cum[..., :, None] - cum[..., None, :] + i_pre[..., None, :], -jnp.inf)
    log_w_end = cum[..., -1:] - cum + i_pre
    qk = jnp.einsum('bhnid,bhnjd->bhnij', q, k)

    def step(carry, xs):
        c_mat, n_vec, m = carry
        q_c, k_c, v_c, cum_c, lw_c, lwe_c, qk_c = xs
        log_inter = cum_c + m[..., None]
        m_row = jnp.maximum(log_inter, jnp.max(lw_c, axis=-1))
        s_inter = jnp.exp(log_inter - m_row)
        w_intra = jnp.exp(lw_c - m_row[..., None]) * qk_c
        num = s_inter[..., None] * jnp.einsum('bhck,bhkv->bhcv', q_c, c_mat) \
            + jnp.einsum('bhij,bhjv->bhiv', w_intra, v_c)
        den = s_inter * jnp.einsum('bhck,bhk->bhc', q_c, n_vec) + jnp.sum(w_intra, axis=-1)
        h_c = num / jnp.maximum(jnp.abs(den), jnp.exp(-m_row))[..., None]
        log_keep = cum_c[..., -1] + m
        m_new = jnp.maximum(log_keep, jnp.max(lwe_c, axis=-1))
        s_keep = jnp.exp(log_keep - m_new)
        w_end = jnp.exp(lwe_c - m_new[..., None])
        c_mat = s_keep[..., None, None] * c_mat + jnp.einsum('bhc,bhck,bhcv->bhkv', w_end, k_c, v_c)
        n_vec = s_keep[..., None] * n_vec + jnp.einsum('bhc,bhck->bhk', w_end, k_c)
        return (c_mat, n_vec, m_new), h_c

    xs = tuple(jnp.moveaxis(t, 2, 0) for t in (q, k, v, cum, log_w, log_w_end, qk))
    b, nh, _, _, dk = q.shape
    dv = v.shape[-1]
    carry0 = (jnp.zeros((b, nh, dk, dv), jnp.float32), jnp.zeros((b, nh, dk), jnp.float32),
              jnp.zeros((b, nh), jnp.float32))
    _, hs = lax.scan(step, carry0, xs)
    return from_chunks(jnp.moveaxis(hs, 0, 2))


def mlstm(h, w_in, gate_bias, w_norm, w_out):
    b, t, _ = h.shape
    q, k, v, o, gates = jnp.split(
        h @ w_in, [MLSTM_QK_DIM, 2 * MLSTM_QK_DIM, 2 * MLSTM_QK_DIM + MLSTM_V_DIM,
                   2 * MLSTM_QK_DIM + 2 * MLSTM_V_DIM], axis=-1)
    q = q.reshape(b, t, MLSTM_HEADS, MLSTM_DQK).astype(jnp.float32)
    k = k.reshape(b, t, MLSTM_HEADS, MLSTM_DQK).astype(jnp.float32) * (MLSTM_DQK ** -0.5)
    v = v.reshape(b, t, MLSTM_HEADS, MLSTM_DV).astype(jnp.float32)
    gates = gates.astype(jnp.float32) + gate_bias.astype(jnp.float32)
    gates = GATE_SOFTCAP * jnp.tanh(gates / GATE_SOFTCAP)
    i_pre, f_pre = gates[..., :MLSTM_HEADS], gates[..., MLSTM_HEADS:]
    hs = chunk_mlstm(q, k, v, i_pre, jax.nn.log_sigmoid(f_pre))
    hs = rms_norm(hs, w_norm.reshape(MLSTM_HEADS, MLSTM_DV)) \
        * jax.nn.sigmoid(o.reshape(b, t, MLSTM_HEADS, MLSTM_DV).astype(jnp.float32))
    return hs.reshape(b, t, MLSTM_V_DIM).astype(h.dtype) @ w_out


def rope_tables(t):
    pos = jnp.arange(t, dtype=jnp.float32)
    inv_freq = ROPE_THETA ** (-jnp.arange(0, MLA_ROPE, 2, dtype=jnp.float32) / MLA_ROPE)
    ang = pos[:, None] * inv_freq[None, :]
    return jnp.cos(ang), jnp.sin(ang)


def apply_rope(x, cos, sin):
    x_pass, x_rot = x[..., :MLA_NOPE], x[..., MLA_NOPE:]
    x1, x2 = x_rot[..., :MLA_ROPE // 2], x_rot[..., MLA_ROPE // 2:]
    c = cos[None, :, None, :].astype(x.dtype)
    s = sin[None, :, None, :].astype(x.dtype)
    return jnp.concatenate([x_pass, x1 * c - x2 * s, x2 * c + x1 * s], axis=-1)


def causal_block_attention(q, k, v):
    b, t, nh, dq = q.shape
    dv = v.shape[-1]
    n_blk = -(-t // Q_BLOCK)
    qp = jnp.pad(q, ((0, 0), (0, n_blk * Q_BLOCK - t), (0, 0), (0, 0)))
    scale = dq ** -0.5
    k_pos = jnp.arange(t)

    def one_block(i):
        start = i * Q_BLOCK
        qb = lax.dynamic_slice_in_dim(qp, start, Q_BLOCK, axis=1)
        s = jnp.einsum('bqhd,bkhd->bhqk', qb, k).astype(jnp.float32) * scale
        q_pos = start + jnp.arange(Q_BLOCK)
        s = jnp.where(k_pos[None, :] <= q_pos[:, None], s, -jnp.inf)
        p = jax.nn.softmax(s, axis=-1).astype(v.dtype)
        return jnp.einsum('bhqk,bkhd->bqhd', p, v)

    o = lax.map(one_block, jnp.arange(n_blk))
    return jnp.moveaxis(o, 0, 1).reshape(b, n_blk * Q_BLOCK, nh, dv)[:, :t]


def mla(h, w_in, q_norm, w_uq, kv_norm, w_ukv, q_head_norm, k_head_norm, w_out):
    b, t, _ = h.shape
    c_q, c_kv, k_rope = jnp.split(h @ w_in, [MLA_Q_RANK, MLA_Q_RANK + MLA_KV_RANK], axis=-1)
    q = (rms_norm(c_q, q_norm) @ w_uq).reshape(b, t, MLA_HEADS, MLA_QK)
    kv = (rms_norm(c_kv, kv_norm) @ w_ukv).reshape(b, t, MLA_HEADS, MLA_NOPE + MLA_V)
    k_nope, v = kv[..., :MLA_NOPE], kv[..., MLA_NOPE:]
    k_rope = jnp.broadcast_to(k_rope[:, :, None, :], (b, t, MLA_HEADS, MLA_ROPE))
    k = jnp.concatenate([k_nope, k_rope], axis=-1)
    cos, sin = rope_tables(t)
    q = apply_rope(rms_norm(q, q_head_norm), cos, sin)
    k = apply_rope(rms_norm(k, k_head_norm), cos, sin)
    o = causal_block_attention(q, k, v)
    return o.reshape(b, t, MLA_HEADS * MLA_V) @ w_out


def squared_relu_mlp(h, w_up, w_down):
    a = jax.nn.relu(h @ w_up)
    return (a * a) @ w_down


def setup_inputs(seed: int = 0) -> dict:
    key = jax.random.key(seed)
    keys = iter(jax.random.split(key, 40))
    f32 = jnp.float32

    def dense(shape, fan_in):
        return jax.random.normal(next(keys), shape, f32) * (fan_in ** -0.5)

    def gain(shape):
        return 1.0 + 0.02 * jax.random.normal(next(keys), shape, f32)

    x = jax.random.normal(next(keys), (BATCH, SEQ, D_MODEL), f32)
    meta_tokens = jax.random.normal(next(keys), (N_META, D_MODEL), f32)
    attn_norm = gain((DEPTH, D_MODEL))
    ffn_norm = gain((DEPTH, D_MODEL))
    ff_up = dense((DEPTH, D_MODEL, D_FF), D_MODEL)
    ff_down = dense((DEPTH, D_FF, D_MODEL), D_FF)
    gdn_in = dense((N_GDN_LAYERS, D_MODEL, GDN_IN), D_MODEL)
    gdn_conv = dense((N_GDN_LAYERS, GDN_CONV, GDN_CONV_DIM), GDN_CONV)
    gdn_a_log = jnp.log(jax.random.uniform(next(keys), (N_GDN_LAYERS, GDN_V_HEADS), f32, 1.0, 16.0))
    dt = jnp.exp(jax.random.uniform(next(keys), (N_GDN_LAYERS, GDN_V_HEADS), f32,
                                    np.float32(np.log(1e-3)), np.float32(np.log(1e-1))))
    gdn_dt_bias = dt + jnp.log(-jnp.expm1(-dt))
    gdn_norm = gain((N_GDN_LAYERS, GDN_HEAD_DIM))
    gdn_out = dense((N_GDN_LAYERS, GDN_V_DIM, D_MODEL), GDN_V_DIM)
    mlstm_in = dense((N_MLSTM_LAYERS, D_MODEL, MLSTM_IN), D_MODEL)
    ig_bias = 0.1 * jax.random.normal(next(keys), (N_MLSTM_LAYERS, MLSTM_HEADS), f32)
    fg_bias = jnp.linspace(3.0, 6.0, MLSTM_HEADS, dtype=f32)[None, :] \
        + 0.1 * jax.random.normal(next(keys), (N_MLSTM_LAYERS, MLSTM_HEADS), f32)
    mlstm_gate_bias = jnp.concatenate([ig_bias, fg_bias], axis=-1)
    mlstm_norm = gain((N_MLSTM_LAYERS, MLSTM_V_DIM))
    mlstm_out = dense((N_MLSTM_LAYERS, MLSTM_V_DIM, D_MODEL), MLSTM_V_DIM)
    mla_in = dense((N_MLA_LAYERS, D_MODEL, MLA_IN), D_MODEL)
    mla_q_norm = gain((N_MLA_LAYERS, MLA_Q_RANK))
    mla_uq = dense((N_MLA_LAYERS, MLA_Q_RANK, MLA_HEADS * MLA_QK), MLA_Q_RANK)
    mla_kv_norm = gain((N_MLA_LAYERS, MLA_KV_RANK))
    mla_ukv = dense((N_MLA_LAYERS, MLA_KV_RANK, MLA_HEADS * (MLA_NOPE + MLA_V)), MLA_KV_RANK)
    mla_q_head_norm = gain((N_MLA_LAYERS, MLA_QK))
    mla_k_head_norm = gain((N_MLA_LAYERS, MLA_QK))
    mla_out = dense((N_MLA_LAYERS, MLA_HEADS * MLA_V, D_MODEL), MLA_HEADS * MLA_V)
    return {'x': x, 'meta_tokens': meta_tokens, 'attn_norm': attn_norm, 'ffn_norm': ffn_norm,
            'ff_up': ff_up, 'ff_down': ff_down,
            'gdn_in': gdn_in, 'gdn_conv': gdn_conv, 'gdn_a_log': gdn_a_log, 'gdn_dt_bias': gdn_dt_bias,
            'gdn_norm': gdn_norm, 'gdn_out': gdn_out,
            'mlstm_in': mlstm_in, 'mlstm_gate_bias': mlstm_gate_bias, 'mlstm_norm': mlstm_norm,
            'mlstm_out': mlstm_out,
            'mla_in': mla_in, 'mla_q_norm': mla_q_norm, 'mla_uq': mla_uq, 'mla_kv_norm': mla_kv_norm,
            'mla_ukv': mla_ukv, 'mla_q_head_norm': mla_q_head_norm, 'mla_k_head_norm': mla_k_head_norm,
            'mla_out': mla_out}


def reference(x, meta_tokens, attn_norm, ffn_norm, ff_up, ff_down,
              gdn_in, gdn_conv, gdn_a_log, gdn_dt_bias, gdn_norm, gdn_out,
              mlstm_in, mlstm_gate_bias, mlstm_norm, mlstm_out,
              mla_in, mla_q_norm, mla_uq, mla_kv_norm, mla_ukv, mla_q_head_norm, mla_k_head_norm,
              mla_out):
    b = x.shape[0]
    meta = jnp.broadcast_to(meta_tokens[None].astype(x.dtype), (b, N_META, x.shape[-1]))
    h = jnp.concatenate([meta, x], axis=1)
    for layer in range(DEPTH):
        kind, j = layer % N_MIXERS, layer // N_MIXERS
        hn = rms_norm(h, attn_norm[layer])
        if kind == 0:
            mix = gated_deltanet(hn, gdn_in[j], gdn_conv[j], gdn_a_log[j], gdn_dt_bias[j],
                                 gdn_norm[j], gdn_out[j])
        elif kind == 1:
            mix = mlstm(hn, mlstm_in[j], mlstm_gate_bias[j], mlstm_norm[j], mlstm_out[j])
        else:
            mix = mla(hn, mla_in[j], mla_q_norm[j], mla_uq[j], mla_kv_norm[j], mla_ukv[j],
                      mla_q_head_norm[j], mla_k_head_norm[j], mla_out[j])
        h = h + mix
        h = h + squared_relu_mlp(rms_norm(h, ffn_norm[layer]), ff_up[layer], ff_down[layer])
    return h[:, N_META:]
```

```python
import functools

import jax
import jax.numpy as jnp
import numpy as np
from jax import lax
from jax.experimental import pallas as pl
from jax.experimental.pallas import tpu as pltpu

F32 = jnp.float32
BF16 = jnp.bfloat16

N_META = 16
CHUNK = 64
LEAD_PAD = (-N_META) % CHUNK
ROW0 = LEAD_PAD
RMS_EPS = 1e-6
N_MIXERS = 3

GDN_QK_HEADS = 8
GDN_V_HEADS = 16
GDN_HEAD_DIM = 128
GDN_CONV = 4

MLSTM_HEADS = 8
MLSTM_DQK = 64
MLSTM_DV = 128
GATE_SOFTCAP = 15.0

MLA_HEADS = 16
MLA_NOPE = 64
MLA_ROPE = 32
MLA_QK = MLA_NOPE + MLA_ROPE
MLA_V = 64
MLA_Q_RANK = 384
MLA_KV_RANK = 256
ROPE_THETA = 10000.0

LANES = 128
NEG_BIG = -1e30
VMEM_LIMIT = 56 * 1024 * 1024


def _pick(n, candidates):
    for c in candidates:
        if n % c == 0:
            return c
    raise ValueError(f"no tile for {n} in {candidates}")


def _params(*sem):
    return pltpu.CompilerParams(dimension_semantics=sem, vmem_limit_bytes=VMEM_LIMIT)


def _norm_rows(x, w):
    ms = jnp.mean(x * x, axis=-1, keepdims=True)
    return x * lax.rsqrt(ms + RMS_EPS) * w


def _softplus(x):
    return jnp.maximum(x, 0.0) + jnp.log1p(jnp.exp(-jnp.abs(x)))


def _dot(a, b):
    return jnp.dot(a, b, preferred_element_type=F32)


def _dot_nt(a, b):
    return lax.dot_general(a, b, (((1,), (1,)), ((), ())), preferred_element_type=F32)


def _dot_tn(a, b):
    return lax.dot_general(a, b, (((0,), (0,)), ((), ())), preferred_element_type=F32)


def _norm_matmul_kernel(x_ref, nw_ref, w_ref, wg_ref, o_ref, g_ref, xn_ref):
    @pl.when(pl.program_id(1) == 0)
    def _():
        xn = _norm_rows(x_ref[...], nw_ref[...]).astype(BF16)
        xn_ref[...] = xn
        g_ref[...] = _dot(xn, wg_ref[...])

    o_ref[...] = _dot(xn_ref[...], w_ref[...]).astype(o_ref.dtype)


def _norm_matmul(x, nw, w, wg):
    m, d = x.shape
    n = w.shape[1]
    tm = _pick(m, (1024, 768, 512, 384, 256, 192, 128, 64))
    tn = _pick(n, (512, 256, 128))
    return pl.pallas_call(
        _norm_matmul_kernel,
        grid=(m // tm, n // tn),
        in_specs=[
            pl.BlockSpec((tm, d), lambda i, j: (i, 0)),
            pl.BlockSpec((1, d), lambda i, j: (0, 0)),
            pl.BlockSpec((d, tn), lambda i, j: (0, j)),
            pl.BlockSpec((d, LANES), lambda i, j: (0, 0)),
        ],
        out_specs=[
            pl.BlockSpec((tm, tn), lambda i, j: (i, j)),
            pl.BlockSpec((tm, LANES), lambda i, j: (i, 0)),
        ],
        out_shape=[
            jax.ShapeDtypeStruct((m, n), BF16),
            jax.ShapeDtypeStruct((m, LANES), F32),
        ],
        scratch_shapes=[pltpu.VMEM((tm, d), BF16)],
        compiler_params=_params("parallel", "arbitrary"),
        name="norm_matmul",
    )(x, nw, w, wg)


def _proj_res_kernel(y_ref, w_ref, h_ref, o_ref):
    o_ref[...] = h_ref[...] + _dot(y_ref[...], w_ref[...])


def _proj_res(y, w, h):
    m, k = y.shape
    d = w.shape[1]
    tm = _pick(m, (1024, 768, 512, 384, 256, 192, 128, 64))
    return pl.pallas_call(
        _proj_res_kernel,
        grid=(m // tm,),
        in_specs=[
            pl.BlockSpec((tm, k), lambda i: (i, 0)),
            pl.BlockSpec((k, d), lambda i: (0, 0)),
            pl.BlockSpec((tm, d), lambda i: (i, 0)),
        ],
        out_specs=pl.BlockSpec((tm, d), lambda i: (i, 0)),
        out_shape=jax.ShapeDtypeStruct((m, d), F32),
        compiler_params=_params("parallel"),
        name="proj_res",
    )(y, w, h)


def _mlp_kernel(h_ref, nw_ref, wu_ref, wd_ref, o_ref, xn_ref):
    j = pl.program_id(1)

    @pl.when(j == 0)
    def _():
        h = h_ref[...]
        xn_ref[...] = _norm_rows(h, nw_ref[...]).astype(BF16)
        o_ref[...] = h

    a = jnp.maximum(_dot(xn_ref[...], wu_ref[...]), 0.0)
    o_ref[...] += _dot((a * a).astype(BF16), wd_ref[...])


def _mlp(h, nw, wu, wd):
    m, d = h.shape
    f = wu.shape[1]
    tm = _pick(m, (1024, 768, 512, 384, 256, 192, 128, 64))
    tf = _pick(f, (512, 256, 128))
    return pl.pallas_call(
        _mlp_kernel,
        grid=(m // tm, f // tf),
        in_specs=[
            pl.BlockSpec((tm, d), lambda i, j: (i, 0)),
            pl.BlockSpec((1, d), lambda i, j: (0, 0)),
            pl.BlockSpec((d, tf), lambda i, j: (0, j)),
            pl.BlockSpec((tf, d), lambda i, j: (j, 0)),
        ],
        out_specs=pl.BlockSpec((tm, d), lambda i, j: (i, 0)),
        out_shape=jax.ShapeDtypeStruct((m, d), F32),
        scratch_shapes=[pltpu.VMEM((tm, d), BF16)],
        compiler_params=_params("parallel", "arbitrary"),
        name="mlp",
    )(h, nw, wu, wd)


def _gdn_prep_kernel(x_ref, w_ref, o_ref, *, n_q_tiles, n_qk_tiles, tc):
    c = pl.program_id(1)
    x = x_ref[...].astype(F32)
    w = w_ref[...]
    y = x * w[GDN_CONV - 1:GDN_CONV, :]
    for s in range(1, GDN_CONV):
        y = y + pltpu.roll(x, s, 0) * w[GDN_CONV - 1 - s:GDN_CONV - s, :]
    y = y * jax.nn.sigmoid(y)
    rows = lax.broadcasted_iota(jnp.int32, y.shape, 0)
    y = jnp.where(rows >= ROW0, y, 0.0)

    @pl.when(c < n_qk_tiles)
    def _():
        scale = jnp.where(c < n_q_tiles, GDN_HEAD_DIM ** -0.5, 1.0).astype(F32)
        for i in range(tc // GDN_HEAD_DIM):
            yi = y[:, i * GDN_HEAD_DIM:(i + 1) * GDN_HEAD_DIM]
            ss = jnp.sum(yi * yi, axis=-1, keepdims=True)
            o_ref[:, i * GDN_HEAD_DIM:(i + 1) * GDN_HEAD_DIM] = (
                yi * (lax.rsqrt(ss + RMS_EPS) * scale)).astype(BF16)

    @pl.when(c >= n_qk_tiles)
    def _():
        o_ref[...] = y.astype(BF16)


def _gdn_prep(proj, w_conv):
    b, tp, _ = proj.shape
    cdim = w_conv.shape[1]
    tc = 256
    qk = GDN_QK_HEADS * GDN_HEAD_DIM
    kern = functools.partial(_gdn_prep_kernel, n_q_tiles=qk // tc, n_qk_tiles=2 * qk // tc, tc=tc)
    return pl.pallas_call(
        kern,
        grid=(b, cdim // tc),
        in_specs=[
            pl.BlockSpec((None, tp, tc), lambda i, c: (i, 0, c)),
            pl.BlockSpec((GDN_CONV, tc), lambda i, c: (0, c)),
        ],
        out_specs=pl.BlockSpec((None, tp, tc), lambda i, c: (i, 0, c)),
        out_shape=jax.ShapeDtypeStruct((b, tp, cdim), BF16),
        compiler_params=_params("parallel", "arbitrary"),
        name="gdn_prep",
    )(proj, w_conv)


def _tri_masks():
    ii = lax.broadcasted_iota(jnp.int32, (CHUNK, CHUNK), 0)
    jj = lax.broadcasted_iota(jnp.int32, (CHUNK, CHUNK), 1)
    return ii, jj


def _gdn_gates_kernel(g_ref, gt_ref, alc_ref, dtc_ref, alr_ref, dtr_ref,
                      beta_ref, cumc_ref, cumr_ref, *, nc, hg):
    nh = GDN_V_HEADS
    ii, jj = _tri_masks()
    tril = (ii >= jj).astype(F32)
    triu = (ii <= jj).astype(F32)
    neg_a_c = -jnp.exp(alc_ref[...])
    neg_a_r = -jnp.exp(alr_ref[...])
    for c in range(nc):
        blk = g_ref[c * CHUNK:(c + 1) * CHUNK, :]
        beta = jax.nn.sigmoid(blk[:, 0:nh])
        g = neg_a_c * _softplus(blk[:, nh:2 * nh] + dtc_ref[...])
        gr = neg_a_r * _softplus(gt_ref[nh:2 * nh, c * CHUNK:(c + 1) * CHUNK] + dtr_ref[...])
        if c == 0:
            rows = lax.broadcasted_iota(jnp.int32, (CHUNK, nh), 0)
            cols = lax.broadcasted_iota(jnp.int32, (nh, CHUNK), 1)
            beta = jnp.where(rows >= ROW0, beta, 0.0)
            g = jnp.where(rows >= ROW0, g, 0.0)
            gr = jnp.where(cols >= ROW0, gr, 0.0)
        cum = jnp.dot(tril, g, preferred_element_type=F32, precision=lax.Precision.HIGHEST)
        cumr = jnp.dot(gr, triu, preferred_element_type=F32, precision=lax.Precision.HIGHEST)
        for q in range(nh // hg):
            beta_ref[q, c] = beta[:, q * hg:(q + 1) * hg]
            cumc_ref[q, c] = cum[:, q * hg:(q + 1) * hg]
            cumr_ref[q, c] = cumr[q * hg:(q + 1) * hg, :]


def _gdn_gates(gates, gates_t, a_log, dt_bias, hg):
    b, tp, _ = gates.shape
    nc = tp // CHUNK
    nh = GDN_V_HEADS
    ng = nh // hg
    kern = functools.partial(_gdn_gates_kernel, nc=nc, hg=hg)
    col = jax.ShapeDtypeStruct((b, ng, nc, CHUNK, hg), F32)
    row = jax.ShapeDtypeStruct((b, ng, nc, hg, CHUNK), F32)
    col_spec = pl.BlockSpec((None, ng, nc, CHUNK, hg), lambda i: (i, 0, 0, 0, 0))
    row_spec = pl.BlockSpec((None, ng, nc, hg, CHUNK), lambda i: (i, 0, 0, 0, 0))
    small = lambda shape: pl.BlockSpec(shape, lambda i: (0, 0))
    return pl.pallas_call(
        kern,
        grid=(b,),
        in_specs=[
            pl.BlockSpec((None, tp, LANES), lambda i: (i, 0, 0)),
            pl.BlockSpec((None, 2 * nh, tp), lambda i: (i, 0, 0)),
            small((1, nh)), small((1, nh)), small((nh, 1)), small((nh, 1)),
        ],
        out_specs=[col_spec, col_spec, row_spec],
        out_shape=[col, col, row],
        compiler_params=_params("parallel"),
        name="gdn_gates",
    )(gates, gates_t, a_log.reshape(1, nh), dt_bias.reshape(1, nh),
      a_log.reshape(nh, 1), dt_bias.reshape(nh, 1))


def _gdn_chunk_kernel(q_ref, k_ref, v_ref, z_ref, beta_ref, cumc_ref, cumr_ref, wn_ref, y_ref,
                      s_ref, u_ref, wq_ref, kd_ref, at_ref, *, nc, hg):
    hd = GDN_HEAD_DIM
    ii, jj = _tri_masks()
    causal = ii >= jj
    strict = ii > jj
    eye = (ii == jj).astype(F32)
    blk = [(ii >> l) == (jj >> l) for l in range(1, CHUNK.bit_length())]
    wn = wn_ref[...]

    def prepare(c):
        r0 = pl.multiple_of(c * CHUNK, CHUNK)
        cumc = cumc_ref[c]
        betac = beta_ref[c]
        cumr = cumr_ref[c]
        for hq in range(hg // 2):
            q = q_ref[pl.ds(r0, CHUNK), hq * hd:(hq + 1) * hd]
            k = k_ref[pl.ds(r0, CHUNK), hq * hd:(hq + 1) * hd]
            qkk = _dot_nt(jnp.concatenate([q, k], axis=0), k)
            qk = qkk[:CHUNK]
            kk = qkk[CHUNK:]
            qf = q.astype(F32)
            kf = k.astype(F32)
            for r in range(2):
                h = 2 * hq + r
                cc = cumc[:, h:h + 1]
                bc = betac[:, h:h + 1]
                cr = cumr[h:h + 1, :]
                decay = jnp.exp(jnp.where(causal, cc - cr, -jnp.inf))
                a = jnp.where(strict, bc * kk * decay, 0.0)
                at_ref[h] = (qk * decay).astype(BF16)
                t = eye - jnp.where(blk[0], a, 0.0)
                for lvl in range(1, len(blk)):
                    a_off = jnp.where(blk[lvl], jnp.where(blk[lvl - 1], 0.0, a), 0.0).astype(BF16)
                    tb = t.astype(BF16)
                    t = t - _dot(_dot(tb, a_off).astype(BF16), tb)
                ec = jnp.exp(cc)
                v = v_ref[pl.ds(r0, CHUNK), h * hd:(h + 1) * hd].astype(F32)
                rhs = jnp.concatenate([(v * bc).astype(BF16), (kf * (bc * ec)).astype(BF16)], axis=1)
                sol = _dot(t.astype(BF16), rhs)
                u_ref[h] = sol[:, :hd]
                wq_ref[h] = jnp.concatenate(
                    [sol[:, hd:].astype(BF16), (qf * ec).astype(BF16)], axis=0)
                kd_ref[h] = (kf * jnp.exp(cc[CHUNK - 1:CHUNK, :] - cc)).astype(BF16)

    def recur(c):
        r0 = pl.multiple_of(c * CHUNK, CHUNK)
        cumc = cumc_ref[c]
        for h in range(hg):
            s = s_ref[h]
            ws = _dot(wq_ref[h], s.astype(BF16))
            v_new = (u_ref[h] - ws[:CHUNK]).astype(BF16)
            o = ws[CHUNK:] + _dot(at_ref[h], v_new)
            last = jnp.exp(cumc[CHUNK - 1:CHUNK, h:h + 1])
            s_ref[h] = s * last + _dot_tn(kd_ref[h], v_new)
            z = z_ref[pl.ds(r0, CHUNK), h * hd:(h + 1) * hd].astype(F32)
            y = _norm_rows(o, wn) * (z * jax.nn.sigmoid(z))
            y_ref[pl.ds(r0, CHUNK), h * hd:(h + 1) * hd] = y.astype(BF16)

    s_ref[...] = jnp.zeros_like(s_ref)
    prepare(0)

    def body(c, carry):
        recur(c)
        prepare(jnp.minimum(c + 1, nc - 1))
        return carry

    lax.fori_loop(0, nc, body, 0)


def _gdn_chunk(qkv, proj, beta, cumc, cumr, w_norm, hg):
    b, tp, _ = qkv.shape
    nc = tp // CHUNK
    hd = GDN_HEAD_DIM
    ng = GDN_V_HEADS // hg
    wqk = hd * hg // 2
    wv = hd * hg
    qk_dim = GDN_QK_HEADS * hd
    v_dim = GDN_V_HEADS * hd
    kern = functools.partial(_gdn_chunk_kernel, nc=nc, hg=hg)
    col_spec = pl.BlockSpec((None, None, nc, CHUNK, hg), lambda i, j: (i, j, 0, 0, 0))
    row_spec = pl.BlockSpec((None, None, nc, hg, CHUNK), lambda i, j: (i, j, 0, 0, 0))
    return pl.pallas_call(
        kern,
        grid=(b, ng),
        in_specs=[
            pl.BlockSpec((None, tp, wqk), lambda i, j: (i, 0, j)),
            pl.BlockSpec((None, tp, wqk), lambda i, j: (i, 0, qk_dim // wqk + j)),
            pl.BlockSpec((None, tp, wv), lambda i, j: (i, 0, 2 * qk_dim // wv + j)),
            pl.BlockSpec((None, tp, wv), lambda i, j: (i, 0, (2 * qk_dim + v_dim) // wv + j)),
            col_spec, col_spec, row_spec,
            pl.BlockSpec((1, hd), lambda i, j: (0, 0)),
        ],
        out_specs=pl.BlockSpec((None, tp, wv), lambda i, j: (i, 0, j)),
        out_shape=jax.ShapeDtypeStruct((b, tp, v_dim), BF16),
        scratch_shapes=[
            pltpu.VMEM((hg, hd, hd), F32),
            pltpu.VMEM((hg, CHUNK, hd), F32),
            pltpu.VMEM((hg, 2 * CHUNK, hd), BF16),
            pltpu.VMEM((hg, CHUNK, hd), BF16),
            pltpu.VMEM((hg, CHUNK, CHUNK), BF16),
        ],
        compiler_params=_params("parallel", "arbitrary"),
        name="gdn_chunk",
    )(qkv, qkv, qkv, proj, beta, cumc, cumr, w_norm.reshape(1, hd))


def _gdn_layer(h, nw, w_in, w_conv, a_log, dt_bias, w_norm, w_out):
    b, tp, d = h.shape
    m = b * tp
    conv_dim = 2 * GDN_QK_HEADS * GDN_HEAD_DIM + GDN_V_HEADS * GDN_HEAD_DIM
    main = conv_dim + GDN_V_HEADS * GDN_HEAD_DIM
    hg = 4
    w_main = w_in[:, :main].astype(BF16)
    w_gate = jnp.pad(w_in[:, main:], ((0, 0), (0, LANES - 2 * GDN_V_HEADS))).astype(BF16)
    proj, gates = _norm_matmul(h.reshape(m, d), nw.reshape(1, d), w_main, w_gate)
    proj = proj.reshape(b, tp, main)
    gates = gates.reshape(b, tp, LANES)
    gates_t = jnp.swapaxes(gates[:, :, :2 * GDN_V_HEADS], 1, 2)
    beta, cumc, cumr = _gdn_gates(gates, gates_t, a_log, dt_bias, hg)
    qkv = _gdn_prep(proj, w_conv)
    y = _gdn_chunk(qkv, proj, beta, cumc, cumr, w_norm, hg)
    return _proj_res(y.reshape(m, -1), w_out.astype(BF16), h.reshape(m, d)).reshape(b, tp, d)


def _mlstm_gates_kernel(g_ref, gt_ref, bc_ref, br_ref, cumc_ref, rc_ref, rr_ref, *, nc, hg):
    nh = MLSTM_HEADS
    ii, jj = _tri_masks()
    tril = (ii >= jj).astype(F32)
    triu = (ii <= jj).astype(F32)

    def split(raw, axis):
        capped = GATE_SOFTCAP * jnp.tanh(raw / GATE_SOFTCAP)
        if axis == 1:
            i_pre, f_pre = capped[:, :nh], capped[:, nh:2 * nh]
        else:
            i_pre, f_pre = capped[:nh], capped[nh:2 * nh]
        return i_pre, -_softplus(-f_pre)

    for c in range(nc):
        i_c, lf_c = split(g_ref[c * CHUNK:(c + 1) * CHUNK, 0:2 * nh] + bc_ref[...], 1)
        i_r, lf_r = split(gt_ref[:, c * CHUNK:(c + 1) * CHUNK] + br_ref[...], 0)
        if c == 0:
            rows = lax.broadcasted_iota(jnp.int32, (CHUNK, nh), 0)
            cols = lax.broadcasted_iota(jnp.int32, (nh, CHUNK), 1)
            i_c = jnp.where(rows >= ROW0, i_c, -jnp.inf)
            lf_c = jnp.where(rows >= ROW0, lf_c, 0.0)
            i_r = jnp.where(cols >= ROW0, i_r, -jnp.inf)
            lf_r = jnp.where(cols >= ROW0, lf_r, 0.0)
        cum = jnp.dot(tril, lf_c, preferred_element_type=F32, precision=lax.Precision.HIGHEST)
        cumr = jnp.dot(lf_r, triu, preferred_element_type=F32, precision=lax.Precision.HIGHEST)
        rc = i_c - cum
        rr = i_r - cumr
        for q in range(nh // hg):
            cumc_ref[q, c] = cum[:, q * hg:(q + 1) * hg]
            rc_ref[q, c] = rc[:, q * hg:(q + 1) * hg]
            rr_ref[q, c] = rr[q * hg:(q + 1) * hg, :]


def _mlstm_gates(gates, gates_t, bias, hg):
    b, tp, _ = gates.shape
    nc = tp // CHUNK
    nh = MLSTM_HEADS
    ng = nh // hg
    kern = functools.partial(_mlstm_gates_kernel, nc=nc, hg=hg)
    col = jax.ShapeDtypeStruct((b, ng, nc, CHUNK, hg), F32)
    row = jax.ShapeDtypeStruct((b, ng, nc, hg, CHUNK), F32)
    col_spec = pl.BlockSpec((None, ng, nc, CHUNK, hg), lambda i: (i, 0, 0, 0, 0))
    row_spec = pl.BlockSpec((None, ng, nc, hg, CHUNK), lambda i: (i, 0, 0, 0, 0))
    return pl.pallas_call(
        kern,
        grid=(b,),
        in_specs=[
            pl.BlockSpec((None, tp, LANES), lambda i: (i, 0, 0)),
            pl.BlockSpec((None, 2 * nh, tp), lambda i: (i, 0, 0)),
            pl.BlockSpec((1, 2 * nh), lambda i: (0, 0)),
            pl.BlockSpec((2 * nh, 1), lambda i: (0, 0)),
        ],
        out_specs=[col_spec, col_spec, row_spec],
        out_shape=[col, col, row],
        compiler_params=_params("parallel"),
        name="mlstm_gates",
    )(gates, gates_t, bias.reshape(1, 2 * nh), bias.reshape(2 * nh, 1))


def _mlstm_chunk_kernel(q_ref, k_ref, v_ref, og_ref, cumc_ref, rc_ref, rr_ref, wn_ref, y_ref,
                        c_ref, m_ref, *, nc, hg):
    dv = MLSTM_DV
    ii, jj = _tri_masks()
    causal = ii >= jj
    ones_col = (lax.broadcasted_iota(jnp.int32, (CHUNK, LANES), 1) == 0).astype(BF16)

    c_ref[...] = jnp.zeros_like(c_ref)
    m_ref[...] = jnp.zeros_like(m_ref)

    def body(c, carry):
        r0 = pl.multiple_of(c * CHUNK, CHUNK)
        cumc = cumc_ref[c]
        rcol = rc_ref[c]
        rrow = rr_ref[c]
        for h in range(hg):
            q = q_ref[pl.ds(r0, CHUNK), h * LANES:(h + 1) * LANES]
            k = k_ref[pl.ds(r0, CHUNK), h * LANES:(h + 1) * LANES]
            v = v_ref[pl.ds(r0, CHUNK), h * dv:(h + 1) * dv]
            v_aug = jnp.concatenate([v, ones_col], axis=1)
            qk = _dot_nt(q, k)
            cc = cumc[:, h:h + 1]
            rc = rcol[:, h:h + 1]
            rr = rrow[h:h + 1, :]
            m = m_ref[h][0:1, 0:1]
            log_w = jnp.where(causal, cc + rr, -jnp.inf)
            log_inter = cc + m
            m_row = jnp.maximum(log_inter, jnp.max(log_w, axis=-1, keepdims=True))
            s_inter = jnp.exp(log_inter - m_row)
            w_intra = jnp.exp(log_w - m_row) * qk
            state = c_ref[h]
            tot = s_inter * _dot(q, state.astype(BF16)) + _dot(w_intra.astype(BF16), v_aug)
            den = tot[:, dv:dv + 1]
            inv = 1.0 / jnp.maximum(jnp.abs(den), jnp.exp(-m_row))
            hs = tot[:, :dv] * inv
            og = og_ref[pl.ds(r0, CHUNK), h * dv:(h + 1) * dv].astype(F32)
            y = _norm_rows(hs, wn_ref[:, h * dv:(h + 1) * dv]) * jax.nn.sigmoid(og)
            y_ref[pl.ds(r0, CHUNK), h * dv:(h + 1) * dv] = y.astype(BF16)
            c_last = cc[CHUNK - 1:CHUNK, :]
            log_keep = c_last + m
            m_new = jnp.maximum(log_keep, c_last + jnp.max(rc, axis=0, keepdims=True))
            w_end = jnp.exp(c_last + rc - m_new)
            kw = (k.astype(F32) * w_end).astype(BF16)
            c_ref[h] = jnp.exp(log_keep - m_new) * state + _dot_tn(kw, v_aug)
            m_ref[h] = jnp.broadcast_to(m_new, m_ref.shape[1:])
        return carry

    lax.fori_loop(0, nc, body, 0)


def _mlstm_chunk(proj, cumc, rc, rr, w_norm, hg):
    b, tp, _ = proj.shape
    nc = tp // CHUNK
    nh = MLSTM_HEADS
    ng = nh // hg
    wb = LANES * hg
    kern = functools.partial(_mlstm_chunk_kernel, nc=nc, hg=hg)
    col_spec = pl.BlockSpec((None, None, nc, CHUNK, hg), lambda i, j: (i, j, 0, 0, 0))
    row_spec = pl.BlockSpec((None, None, nc, hg, CHUNK), lambda i, j: (i, j, 0, 0, 0))
    return pl.pallas_call(
        kern,
        grid=(b, ng),
        in_specs=[
            pl.BlockSpec((None, tp, wb), lambda i, j: (i, 0, j)),
            pl.BlockSpec((None, tp, wb), lambda i, j: (i, 0, ng + j)),
            pl.BlockSpec((None, tp, wb), lambda i, j: (i, 0, 2 * ng + j)),
            pl.BlockSpec((None, tp, wb), lambda i, j: (i, 0, 3 * ng + j)),
            col_spec, col_spec, row_spec,
            pl.BlockSpec((1, wb), lambda i, j: (0, j)),
        ],
        out_specs=pl.BlockSpec((None, tp, wb), lambda i, j: (i, 0, j)),
        out_shape=jax.ShapeDtypeStruct((b, tp, nh * MLSTM_DV), BF16),
        scratch_shapes=[
            pltpu.VMEM((hg, LANES, 2 * MLSTM_DV), F32),
            pltpu.VMEM((hg, 8, LANES), F32),
        ],
        compiler_params=_params("parallel", "arbitrary"),
        name="mlstm_chunk",
    )(proj, proj, proj, proj, cumc, rc, rr, w_norm.reshape(1, nh * MLSTM_DV))


def _pad_heads(w, nh, dh):
    d = w.shape[0]
    w = w.reshape(d, nh, dh)
    return jnp.pad(w, ((0, 0), (0, 0), (0, LANES - dh))).reshape(d, nh * LANES)


def _mlstm_layer(h, nw, w_in, gate_bias, w_norm, w_out):
    b, tp, d = h.shape
    m = b * tp
    nh = MLSTM_HEADS
    qk = nh * MLSTM_DQK
    vd = nh * MLSTM_DV
    hg = 4
    w_main = jnp.concatenate([
        _pad_heads(w_in[:, :qk], nh, MLSTM_DQK),
        _pad_heads(w_in[:, qk:2 * qk] * (MLSTM_DQK ** -0.5), nh, MLSTM_DQK),
        w_in[:, 2 * qk:2 * qk + 2 * vd],
    ], axis=1).astype(BF16)
    w_gate = jnp.pad(w_in[:, 2 * qk + 2 * vd:], ((0, 0), (0, LANES - 2 * nh))).astype(BF16)
    proj, gates = _norm_matmul(h.reshape(m, d), nw.reshape(1, d), w_main, w_gate)
    proj = proj.reshape(b, tp, -1)
    gates = gates.reshape(b, tp, LANES)
    gates_t = jnp.swapaxes(gates[:, :, :2 * nh], 1, 2)
    cumc, rc, rr = _mlstm_gates(gates, gates_t, gate_bias, hg)
    y = _mlstm_chunk(proj, cumc, rc, rr, w_norm, hg)
    return _proj_res(y.reshape(m, vd), w_out.astype(BF16), h.reshape(m, d)).reshape(b, tp, d)


def _mla_proj_kernel(h_ref, nw_ref, win_ref, qn_ref, wuq_ref, kvn_ref, wuk_ref, wuv_ref,
                     qhn_ref, khn_ref, cos_ref, s1_ref, s2_ref, q_out, k_out, v_out):
    xn = _norm_rows(h_ref[...], nw_ref[...]).astype(BF16)
    c = _dot(xn, win_ref[...])
    cq = _norm_rows(c[:, :MLA_Q_RANK], qn_ref[...]).astype(BF16)
    ckv = _norm_rows(c[:, MLA_Q_RANK:MLA_Q_RANK + MLA_KV_RANK], kvn_ref[...]).astype(BF16)
    kr = c[:, MLA_Q_RANK + MLA_KV_RANK:]
    q = _dot(cq, wuq_ref[...])
    kn = _dot(ckv, wuk_ref[...])
    v_out[...] = _dot(ckv, wuv_ref[...]).astype(BF16)
    cos = cos_ref[...]
    s1 = s1_ref[...]
    s2 = s2_ref[...]

    def head_norm_rope(x, gain):
        ms = jnp.sum(x * x, axis=-1, keepdims=True) * (1.0 / MLA_QK)
        x = x * lax.rsqrt(ms + RMS_EPS) * gain
        half = MLA_ROPE // 2
        return x * cos + pltpu.roll(x, LANES - half, 1) * s1 + pltpu.roll(x, half, 1) * s2

    for h in range(MLA_HEADS):
        sl = slice(h * LANES, (h + 1) * LANES)
        q_out[:, sl] = (head_norm_rope(q[:, sl], qhn_ref[...]) * (MLA_QK ** -0.5)).astype(BF16)
        k_out[:, sl] = head_norm_rope(kn[:, sl] + kr, khn_ref[...]).astype(BF16)


def _mla_proj(h, nw, w_in, q_norm, w_uq, kv_norm, w_uk, w_uv, qhn, khn, cos, s1, s2):
    b, tp, d = h.shape
    tt = _pick(tp, (704, 352, 192, 64))
    nh = MLA_HEADS
    full = lambda a: pl.BlockSpec(a.shape, lambda i, j: (0,) * a.ndim)
    tab = pl.BlockSpec((tt, LANES), lambda i, j: (j, 0))
    row = lambda n: pl.BlockSpec((None, tt, n), lambda i, j: (i, j, 0))
    return pl.pallas_call(
        _mla_proj_kernel,
        grid=(b, tp // tt),
        in_specs=[row(d), full(nw), full(w_in), full(q_norm), full(w_uq), full(kv_norm),
                  full(w_uk), full(w_uv), full(qhn), full(khn), tab, tab, tab],
        out_specs=[row(nh * LANES), row(nh * LANES), row(nh * MLA_V)],
        out_shape=[
            jax.ShapeDtypeStruct((b, tp, nh * LANES), BF16),
            jax.ShapeDtypeStruct((b, tp, nh * LANES), BF16),
            jax.ShapeDtypeStruct((b, tp, nh * MLA_V), BF16),
        ],
        compiler_params=_params("parallel", "arbitrary"),
        name="mla_proj",
    )(h, nw, w_in, q_norm, w_uq, kv_norm, w_uk, w_uv, qhn, khn, cos, s1, s2)


def _mla_attn_kernel(q_ref, k_ref, v_ref, o_ref, *, tq, nq):
    for qi in range(nq):
        kend = (qi + 1) * tq
        rows = qi * tq + lax.broadcasted_iota(jnp.int32, (tq, kend), 0)
        cols = lax.broadcasted_iota(jnp.int32, (tq, kend), 1)
        bias = jnp.where(cols <= rows, jnp.where(cols >= ROW0, 0.0, NEG_BIG), NEG_BIG)
        v = v_ref[0:kend, :]
        outs = []
        for r in range(2):
            q = q_ref[qi * tq:(qi + 1) * tq, r * LANES:(r + 1) * LANES]
            k = k_ref[0:kend, r * LANES:(r + 1) * LANES]
            s = _dot_nt(q, k) + bias
            p = jnp.exp(s - jnp.max(s, axis=-1, keepdims=True))
            inv = 1.0 / jnp.sum(p, axis=-1, keepdims=True)
            outs.append(_dot(p.astype(BF16), v) * inv)
        lane = lax.broadcasted_iota(jnp.int32, (tq, LANES), 1)
        o = jnp.where(lane < MLA_V, outs[0], outs[1])
        orow = qi * tq + lax.broadcasted_iota(jnp.int32, (tq, LANES), 0)
        o_ref[qi * tq:(qi + 1) * tq, :] = jnp.where(orow >= ROW0, o, 0.0).astype(BF16)


def _mla_attn(q, k, v):
    b, tp, _ = q.shape
    tq = _pick(tp, (192, 64))
    npair = MLA_HEADS // 2
    kern = functools.partial(_mla_attn_kernel, tq=tq, nq=tp // tq)
    return pl.pallas_call(
        kern,
        grid=(b, npair),
        in_specs=[
            pl.BlockSpec((None, tp, 2 * LANES), lambda i, j: (i, 0, j)),
            pl.BlockSpec((None, tp, 2 * LANES), lambda i, j: (i, 0, j)),
            pl.BlockSpec((None, tp, LANES), lambda i, j: (i, 0, j)),
        ],
        out_specs=pl.BlockSpec((None, tp, LANES), lambda i, j: (i, 0, j)),
        out_shape=jax.ShapeDtypeStruct((b, tp, MLA_HEADS * MLA_V), BF16),
        compiler_params=_params("parallel", "arbitrary"),
        name="mla_attn",
    )(q, k, v)


def _rope_tables(tp):
    half = MLA_ROPE // 2
    pos = jnp.arange(tp, dtype=F32) - float(ROW0)
    inv_freq = ROPE_THETA ** (-jnp.arange(0, MLA_ROPE, 2, dtype=F32) / MLA_ROPE)
    ang = pos[:, None] * inv_freq[None, :]
    cos, sin = jnp.cos(ang), jnp.sin(ang)
    ones = jnp.ones((tp, MLA_NOPE), F32)
    z_nope = jnp.zeros((tp, MLA_NOPE), F32)
    z_half = jnp.zeros((tp, half), F32)
    z_tail = jnp.zeros((tp, LANES - MLA_QK), F32)
    cos_t = jnp.concatenate([ones, cos, cos, z_tail], axis=1)
    s1 = jnp.concatenate([z_nope, -sin, z_half, z_tail], axis=1)
    s2 = jnp.concatenate([z_nope, z_half, sin, z_tail], axis=1)
    return cos_t, s1, s2


def _mla_layer(h, nw, w_in, q_norm, w_uq, kv_norm, w_ukv, q_head_norm, k_head_norm, w_out):
    b, tp, d = h.shape
    m = b * tp
    nh = MLA_HEADS
    lat = MLA_Q_RANK + MLA_KV_RANK
    w_in_p = jnp.concatenate([
        w_in[:, :lat],
        jnp.zeros((d, MLA_NOPE), F32), w_in[:, lat:], jnp.zeros((d, LANES - MLA_QK), F32),
    ], axis=1).astype(BF16)
    w_uq_p = _pad_heads(w_uq, nh, MLA_QK).astype(BF16)
    w_ukv3 = w_ukv.reshape(MLA_KV_RANK, nh, MLA_NOPE + MLA_V)
    w_uk_p = _pad_heads(w_ukv3[:, :, :MLA_NOPE].reshape(MLA_KV_RANK, nh * MLA_NOPE), nh, MLA_NOPE).astype(BF16)
    w_uv = w_ukv3[:, :, MLA_NOPE:].reshape(MLA_KV_RANK, nh * MLA_V).astype(BF16)
    pad_gain = lambda g: jnp.pad(g, (0, LANES - MLA_QK)).reshape(1, LANES)
    cos, s1, s2 = _rope_tables(tp)
    q, k, v = _mla_proj(h, nw.reshape(1, d), w_in_p, q_norm.reshape(1, -1), w_uq_p,
                        kv_norm.reshape(1, -1), w_uk_p, w_uv,
                        pad_gain(q_head_norm), pad_gain(k_head_norm), cos, s1, s2)
    o = _mla_attn(q, k, v)
    return _proj_res(o.reshape(m, nh * MLA_V), w_out.astype(BF16), h.reshape(m, d)).reshape(b, tp, d)


def kernel(x, meta_tokens, attn_norm, ffn_norm, ff_up, ff_down, gdn_in, gdn_conv, gdn_a_log, gdn_dt_bias, gdn_norm, gdn_out, mlstm_in, mlstm_gate_bias, mlstm_norm, mlstm_out, mla_in, mla_q_norm, mla_uq, mla_kv_norm, mla_ukv, mla_q_head_norm, mla_k_head_norm, mla_out):
    b, t, d = x.shape
    depth = attn_norm.shape[0]
    meta = jnp.broadcast_to(meta_tokens[None].astype(x.dtype), (b, N_META, d))
    h = jnp.concatenate([jnp.zeros((b, LEAD_PAD, d), x.dtype), meta, x], axis=1)
    tp = h.shape[1]
    for layer in range(depth):
        kind, j = layer % N_MIXERS, layer // N_MIXERS
        if kind == 0:
            h = _gdn_layer(h, attn_norm[layer], gdn_in[j], gdn_conv[j], gdn_a_log[j],
                           gdn_dt_bias[j], gdn_norm[j], gdn_out[j])
        elif kind == 1:
            h = _mlstm_layer(h, attn_norm[layer], mlstm_in[j], mlstm_gate_bias[j],
                             mlstm_norm[j], mlstm_out[j])
        else:
            h = _mla_layer(h, attn_norm[layer], mla_in[j], mla_q_norm[j], mla_uq[j],
                           mla_kv_norm[j], mla_ukv[j], mla_q_head_norm[j],
                           mla_k_head_norm[j], mla_out[j])
        h = _mlp(h.reshape(b * tp, d), ffn_norm[layer].reshape(1, d),
                 ff_up[layer].astype(BF16), ff_down[layer].astype(BF16)).reshape(b, tp, d)
    return h[:, LEAD_PAD + N_META:]
```

```python
import functools

import jax
import jax.numpy as jnp
import numpy as np
from jax import lax
from jax.experimental import pallas as pl
from jax.experimental.pallas import tpu as pltpu

F32 = jnp.float32
BF16 = jnp.bfloat16

N_META = 16
CHUNK = 64
LEAD_PAD = (-N_META) % CHUNK
ROW0 = LEAD_PAD
RMS_EPS = 1e-6
N_MIXERS = 3

GDN_QK_HEADS = 8
GDN_V_HEADS = 16
GDN_HEAD_DIM = 128
GDN_CONV = 4
GDN_HEADS_PER_STEP = 8
GDN_CHUNKS_PER_STEP = 2

MLSTM_HEADS = 8
MLSTM_DQK = 64
MLSTM_DV = 128
GATE_SOFTCAP = 15.0
MLSTM_CHUNKS_PER_STEP = 2

MLA_HEADS = 16
MLA_NOPE = 64
MLA_ROPE = 32
MLA_QK = MLA_NOPE + MLA_ROPE
MLA_V = 64
MLA_Q_RANK = 384
MLA_KV_RANK = 256
ROPE_THETA = 10000.0

LANES = 128
NEG_BIG = -1e30
VMEM_LIMIT = 56 * 1024 * 1024


def _pick(n, candidates):
    for c in candidates:
        if n % c == 0:
            return c
    raise ValueError(f"no tile for {n} in {candidates}")


def _params(*sem):
    return pltpu.CompilerParams(dimension_semantics=sem, vmem_limit_bytes=VMEM_LIMIT)


def _norm_rows(x, w):
    ms = jnp.mean(x * x, axis=-1, keepdims=True)
    return x * lax.rsqrt(ms + RMS_EPS) * w


def _softplus(x):
    return jnp.maximum(x, 0.0) + jnp.log1p(jnp.exp(-jnp.abs(x)))


def _dot(a, b):
    return jnp.dot(a, b, preferred_element_type=F32)


def _dot_nt(a, b):
    return lax.dot_general(a, b, (((1,), (1,)), ((), ())), preferred_element_type=F32)


def _dot_tn(a, b):
    return lax.dot_general(a, b, (((0,), (0,)), ((), ())), preferred_element_type=F32)


def _norm_matmul_kernel(x_ref, nw_ref, w_ref, wg_ref, o_ref, g_ref, xn_ref):
    @pl.when(pl.program_id(1) == 0)
    def _():
        xn = _norm_rows(x_ref[...], nw_ref[...]).astype(BF16)
        xn_ref[...] = xn
        g_ref[...] = _dot(xn, wg_ref[...])

    o_ref[...] = _dot(xn_ref[...], w_ref[...]).astype(o_ref.dtype)


def _norm_matmul(x, nw, w, wg):
    m, d = x.shape
    n = w.shape[1]
    tm = _pick(m, (1024, 768, 512, 384, 256, 192, 128, 64))
    tn = _pick(n, (512, 256, 128))
    return pl.pallas_call(
        _norm_matmul_kernel,
        grid=(m // tm, n // tn),
        in_specs=[
            pl.BlockSpec((tm, d), lambda i, j: (i, 0)),
            pl.BlockSpec((1, d), lambda i, j: (0, 0)),
            pl.BlockSpec((d, tn), lambda i, j: (0, j)),
            pl.BlockSpec((d, LANES), lambda i, j: (0, 0)),
        ],
        out_specs=[
            pl.BlockSpec((tm, tn), lambda i, j: (i, j)),
            pl.BlockSpec((tm, LANES), lambda i, j: (i, 0)),
        ],
        out_shape=[
            jax.ShapeDtypeStruct((m, n), BF16),
            jax.ShapeDtypeStruct((m, LANES), F32),
        ],
        scratch_shapes=[pltpu.VMEM((tm, d), BF16)],
        compiler_params=_params("parallel", "arbitrary"),
        name="norm_matmul",
    )(x, nw, w, wg)


def _proj_res_kernel(y_ref, w_ref, h_ref, o_ref):
    o_ref[...] = h_ref[...] + _dot(y_ref[...], w_ref[...])


def _proj_res(y, w, h):
    m, k = y.shape
    d = w.shape[1]
    tm = _pick(m, (1024, 768, 512, 384, 256, 192, 128, 64))
    return pl.pallas_call(
        _proj_res_kernel,
        grid=(m // tm,),
        in_specs=[
            pl.BlockSpec((tm, k), lambda i: (i, 0)),
            pl.BlockSpec((k, d), lambda i: (0, 0)),
            pl.BlockSpec((tm, d), lambda i: (i, 0)),
        ],
        out_specs=pl.BlockSpec((tm, d), lambda i: (i, 0)),
        out_shape=jax.ShapeDtypeStruct((m, d), F32),
        compiler_params=_params("parallel"),
        name="proj_res",
    )(y, w, h)


def _mlp_kernel(h_ref, nw_ref, wu_ref, wd_ref, o_ref, xn_ref):
    j = pl.program_id(1)

    @pl.when(j == 0)
    def _():
        h = h_ref[...]
        xn_ref[...] = _norm_rows(h, nw_ref[...]).astype(BF16)
        o_ref[...] = h

    a = jnp.maximum(_dot(xn_ref[...], wu_ref[...]), 0.0)
    o_ref[...] += _dot((a * a).astype(BF16), wd_ref[...])


def _mlp(h, nw, wu, wd):
    m, d = h.shape
    f = wu.shape[1]
    tm = _pick(m, (1024, 768, 512, 384, 256, 192, 128, 64))
    tf = _pick(f, (512, 256, 128))
    return pl.pallas_call(
        _mlp_kernel,
        grid=(m // tm, f // tf),
        in_specs=[
            pl.BlockSpec((tm, d), lambda i, j: (i, 0)),
            pl.BlockSpec((1, d), lambda i, j: (0, 0)),
            pl.BlockSpec((d, tf), lambda i, j: (0, j)),
            pl.BlockSpec((tf, d), lambda i, j: (j, 0)),
        ],
        out_specs=pl.BlockSpec((tm, d), lambda i, j: (i, 0)),
        out_shape=jax.ShapeDtypeStruct((m, d), F32),
        scratch_shapes=[pltpu.VMEM((tm, d), BF16)],
        compiler_params=_params("parallel", "arbitrary"),
        name="mlp",
    )(h, nw, wu, wd)


def _gdn_prep_kernel(x_ref, w_ref, o_ref, *, n_q_tiles, n_qk_tiles, tc):
    c = pl.program_id(1)
    x = x_ref[...].astype(F32)
    w = w_ref[...]
    y = x * w[GDN_CONV - 1:GDN_CONV, :]
    for s in range(1, GDN_CONV):
        y = y + pltpu.roll(x, s, 0) * w[GDN_CONV - 1 - s:GDN_CONV - s, :]
    y = y * jax.nn.sigmoid(y)
    rows = lax.broadcasted_iota(jnp.int32, y.shape, 0)
    y = jnp.where(rows >= ROW0, y, 0.0)

    @pl.when(c < n_qk_tiles)
    def _():
        scale = jnp.where(c < n_q_tiles, GDN_HEAD_DIM ** -0.5, 1.0).astype(F32)
        for i in range(tc // GDN_HEAD_DIM):
            yi = y[:, i * GDN_HEAD_DIM:(i + 1) * GDN_HEAD_DIM]
            ss = jnp.sum(yi * yi, axis=-1, keepdims=True)
            o_ref[:, i * GDN_HEAD_DIM:(i + 1) * GDN_HEAD_DIM] = (
                yi * (lax.rsqrt(ss + RMS_EPS) * scale)).astype(BF16)

    @pl.when(c >= n_qk_tiles)
    def _():
        o_ref[...] = y.astype(BF16)


def _gdn_prep(proj, w_conv):
    b, tp, _ = proj.shape
    cdim = w_conv.shape[1]
    tc = 256
    qk = GDN_QK_HEADS * GDN_HEAD_DIM
    kern = functools.partial(_gdn_prep_kernel, n_q_tiles=qk // tc, n_qk_tiles=2 * qk // tc, tc=tc)
    return pl.pallas_call(
        kern,
        grid=(b, cdim // tc),
        in_specs=[
            pl.BlockSpec((None, tp, tc), lambda i, c: (i, 0, c)),
            pl.BlockSpec((GDN_CONV, tc), lambda i, c: (0, c)),
        ],
        out_specs=pl.BlockSpec((None, tp, tc), lambda i, c: (i, 0, c)),
        out_shape=jax.ShapeDtypeStruct((b, tp, cdim), BF16),
        compiler_params=_params("parallel", "arbitrary"),
        name="gdn_prep",
    )(proj, w_conv)


def _tri_masks():
    ii = lax.broadcasted_iota(jnp.int32, (CHUNK, CHUNK), 0)
    jj = lax.broadcasted_iota(jnp.int32, (CHUNK, CHUNK), 1)
    return ii, jj


def _gdn_gates_kernel(g_ref, gt_ref, alc_ref, dtc_ref, alr_ref, dtr_ref,
                      beta_ref, cumc_ref, cumr_ref, *, nc, hg):
    nh = GDN_V_HEADS
    ii, jj = _tri_masks()
    tril = (ii >= jj).astype(F32)
    triu = (ii <= jj).astype(F32)
    neg_a_c = -jnp.exp(alc_ref[...])
    neg_a_r = -jnp.exp(alr_ref[...])
    for c in range(nc):
        blk = g_ref[c * CHUNK:(c + 1) * CHUNK, :]
        beta = jax.nn.sigmoid(blk[:, 0:nh])
        g = neg_a_c * _softplus(blk[:, nh:2 * nh] + dtc_ref[...])
        gr = neg_a_r * _softplus(gt_ref[nh:2 * nh, c * CHUNK:(c + 1) * CHUNK] + dtr_ref[...])
        if c == 0:
            rows = lax.broadcasted_iota(jnp.int32, (CHUNK, nh), 0)
            cols = lax.broadcasted_iota(jnp.int32, (nh, CHUNK), 1)
            beta = jnp.where(rows >= ROW0, beta, 0.0)
            g = jnp.where(rows >= ROW0, g, 0.0)
            gr = jnp.where(cols >= ROW0, gr, 0.0)
        cum = jnp.dot(tril, g, preferred_element_type=F32, precision=lax.Precision.HIGHEST)
        cumr = jnp.dot(gr, triu, preferred_element_type=F32, precision=lax.Precision.HIGHEST)
        for q in range(nh // hg):
            beta_ref[q, c] = beta[:, q * hg:(q + 1) * hg]
            cumc_ref[q, c] = cum[:, q * hg:(q + 1) * hg]
            cumr_ref[q, c] = cumr[q * hg:(q + 1) * hg, :]


def _gdn_gates(gates, gates_t, a_log, dt_bias, hg):
    b, tp, _ = gates.shape
    nc = tp // CHUNK
    nh = GDN_V_HEADS
    ng = nh // hg
    kern = functools.partial(_gdn_gates_kernel, nc=nc, hg=hg)
    col = jax.ShapeDtypeStruct((b, ng, nc, CHUNK, hg), F32)
    row = jax.ShapeDtypeStruct((b, ng, nc, hg, CHUNK), F32)
    col_spec = pl.BlockSpec((None, ng, nc, CHUNK, hg), lambda i: (i, 0, 0, 0, 0))
    row_spec = pl.BlockSpec((None, ng, nc, hg, CHUNK), lambda i: (i, 0, 0, 0, 0))
    small = lambda shape: pl.BlockSpec(shape, lambda i: (0, 0))
    return pl.pallas_call(
        kern,
        grid=(b,),
        in_specs=[
            pl.BlockSpec((None, tp, LANES), lambda i: (i, 0, 0)),
            pl.BlockSpec((None, 2 * nh, tp), lambda i: (i, 0, 0)),
            small((1, nh)), small((1, nh)), small((nh, 1)), small((nh, 1)),
        ],
        out_specs=[col_spec, col_spec, row_spec],
        out_shape=[col, col, row],
        compiler_params=_params("parallel"),
        name="gdn_gates",
    )(gates, gates_t, a_log.reshape(1, nh), dt_bias.reshape(1, nh),
      a_log.reshape(nh, 1), dt_bias.reshape(nh, 1))


def _gdn_chunk_kernel(q_ref, k_ref, v_ref, z_ref, beta_ref, cumc_ref, cumr_ref, wn_ref, y_ref,
                      s_ref, u_ref, wq_ref, kd_ref, at_ref, *, nc, hg, cps):
    hd = GDN_HEAD_DIM
    ii, jj = _tri_masks()
    causal = ii >= jj
    strict = ii > jj
    eye = (ii == jj).astype(F32)
    blk = [(ii >> l) == (jj >> l) for l in range(1, CHUNK.bit_length())]
    wn = wn_ref[...]

    heads = range(hg)
    pairs = range(hg // 2)

    def prepare(chunks):
        units = [(i, h) for i in range(len(chunks)) for h in heads]
        r0 = [pl.multiple_of(c * CHUNK, CHUNK) for c in chunks]
        cumc = [cumc_ref[c] for c in chunks]
        betac = [beta_ref[c] for c in chunks]
        cumr = [cumr_ref[c] for c in chunks]
        q = [[q_ref[pl.ds(r, CHUNK), p * hd:(p + 1) * hd] for p in pairs] for r in r0]
        k = [[k_ref[pl.ds(r, CHUNK), p * hd:(p + 1) * hd] for p in pairs] for r in r0]
        qkk = [[_dot_nt(jnp.concatenate([q[i][p], k[i][p]], axis=0), k[i][p]) for p in pairs]
               for i in range(len(chunks))]
        cc = {u: cumc[u[0]][:, u[1]:u[1] + 1] for u in units}
        bc = {u: betac[u[0]][:, u[1]:u[1] + 1] for u in units}
        decay = {(i, h): jnp.exp(jnp.where(causal, cc[i, h] - cumr[i][h:h + 1, :], -jnp.inf))
                 for i, h in units}
        a = {(i, h): jnp.where(strict, bc[i, h] * qkk[i][h // 2][CHUNK:] * decay[i, h], 0.0)
             for i, h in units}
        for i, h in units:
            at_ref[i, h] = (qkk[i][h // 2][:CHUNK] * decay[i, h]).astype(BF16)
        t = {u: eye - jnp.where(blk[0], a[u], 0.0) for u in units}
        for lvl in range(1, len(blk)):
            a_off = {u: jnp.where(blk[lvl], jnp.where(blk[lvl - 1], 0.0, a[u]), 0.0).astype(BF16)
                     for u in units}
            tb = {u: t[u].astype(BF16) for u in units}
            x = {u: _dot(tb[u], a_off[u]).astype(BF16) for u in units}
            t = {u: t[u] - _dot(x[u], tb[u]) for u in units}
        ec = {u: jnp.exp(cc[u]) for u in units}
        rhs = {}
        for i, h in units:
            v = v_ref[pl.ds(r0[i], CHUNK), h * hd:(h + 1) * hd].astype(F32)
            kf = k[i][h // 2].astype(F32)
            rhs[i, h] = jnp.concatenate([(v * bc[i, h]).astype(BF16),
                                         (kf * (bc[i, h] * ec[i, h])).astype(BF16)], axis=1)
        sol = {u: _dot(t[u].astype(BF16), rhs[u]) for u in units}
        for i, h in units:
            qf = q[i][h // 2].astype(F32)
            kf = k[i][h // 2].astype(F32)
            u_ref[i, h] = sol[i, h][:, :hd]
            wq_ref[i, h] = jnp.concatenate(
                [sol[i, h][:, hd:].astype(BF16), (qf * ec[i, h]).astype(BF16)], axis=0)
            kd_ref[i, h] = (kf * jnp.exp(cc[i, h][CHUNK - 1:CHUNK, :] - cc[i, h])).astype(BF16)

    def recur(c, slot, s):
        r0 = pl.multiple_of(c * CHUNK, CHUNK)
        cumc = cumc_ref[c]
        ws = [_dot(wq_ref[slot, h], s[h].astype(BF16)) for h in heads]
        v_new = [(u_ref[slot, h] - ws[h][:CHUNK]).astype(BF16) for h in heads]
        upd = [_dot_tn(kd_ref[slot, h], v_new[h]) for h in heads]
        o = [ws[h][CHUNK:] + _dot(at_ref[slot, h], v_new[h]) for h in heads]
        s_new = [s[h] * jnp.exp(cumc[CHUNK - 1:CHUNK, h:h + 1]) + upd[h] for h in heads]
        for h in heads:
            z = z_ref[pl.ds(r0, CHUNK), h * hd:(h + 1) * hd].astype(F32)
            y = _norm_rows(o[h], wn) * (z * jax.nn.sigmoid(z))
            y_ref[pl.ds(r0, CHUNK), h * hd:(h + 1) * hd] = y.astype(BF16)
        return s_new

    s_ref[...] = jnp.zeros_like(s_ref)
    prepare([jnp.int32(i) for i in range(min(cps, nc))])

    def body(it, carry):
        c0 = it * cps
        s = [s_ref[h] for h in heads]
        for i in range(cps):
            s = recur(c0 + i, i, s)
        for h in heads:
            s_ref[h] = s[h]
        prepare([jnp.minimum(c0 + cps + i, nc - 1) for i in range(cps)])
        return carry

    lax.fori_loop(0, nc // cps, body, 0)
    s = [s_ref[h] for h in heads]
    for i in range(nc % cps):
        s = recur(jnp.int32(nc - nc % cps + i), i, s)


def _gdn_chunk(qkv, proj, beta, cumc, cumr, w_norm, hg):
    b, tp, _ = qkv.shape
    nc = tp // CHUNK
    hd = GDN_HEAD_DIM
    ng = GDN_V_HEADS // hg
    wqk = hd * hg // 2
    wv = hd * hg
    qk_dim = GDN_QK_HEADS * hd
    v_dim = GDN_V_HEADS * hd
    cps = GDN_CHUNKS_PER_STEP
    kern = functools.partial(_gdn_chunk_kernel, nc=nc, hg=hg, cps=cps)
    col_spec = pl.BlockSpec((None, None, nc, CHUNK, hg), lambda i, j: (i, j, 0, 0, 0))
    row_spec = pl.BlockSpec((None, None, nc, hg, CHUNK), lambda i, j: (i, j, 0, 0, 0))
    return pl.pallas_call(
        kern,
        grid=(b, ng),
        in_specs=[
            pl.BlockSpec((None, tp, wqk), lambda i, j: (i, 0, j)),
            pl.BlockSpec((None, tp, wqk), lambda i, j: (i, 0, qk_dim // wqk + j)),
            pl.BlockSpec((None, tp, wv), lambda i, j: (i, 0, 2 * qk_dim // wv + j)),
            pl.BlockSpec((None, tp, wv), lambda i, j: (i, 0, (2 * qk_dim + v_dim) // wv + j)),
            col_spec, col_spec, row_spec,
            pl.BlockSpec((1, hd), lambda i, j: (0, 0)),
        ],
        out_specs=pl.BlockSpec((None, tp, wv), lambda i, j: (i, 0, j)),
        out_shape=jax.ShapeDtypeStruct((b, tp, v_dim), BF16),
        scratch_shapes=[
            pltpu.VMEM((hg, hd, hd), F32),
            pltpu.VMEM((cps, hg, CHUNK, hd), F32),
            pltpu.VMEM((cps, hg, 2 * CHUNK, hd), BF16),
            pltpu.VMEM((cps, hg, CHUNK, hd), BF16),
            pltpu.VMEM((cps, hg, CHUNK, CHUNK), BF16),
        ],
        compiler_params=_params("parallel", "arbitrary"),
        name="gdn_chunk",
    )(qkv, qkv, qkv, proj, beta, cumc, cumr, w_norm.reshape(1, hd))


def _gdn_layer(h, nw, w_in, w_conv, a_log, dt_bias, w_norm, w_out):
    b, tp, d = h.shape
    m = b * tp
    conv_dim = 2 * GDN_QK_HEADS * GDN_HEAD_DIM + GDN_V_HEADS * GDN_HEAD_DIM
    main = conv_dim + GDN_V_HEADS * GDN_HEAD_DIM
    hg = GDN_HEADS_PER_STEP
    w_main = w_in[:, :main].astype(BF16)
    w_gate = jnp.pad(w_in[:, main:], ((0, 0), (0, LANES - 2 * GDN_V_HEADS))).astype(BF16)
    proj, gates = _norm_matmul(h.reshape(m, d), nw.reshape(1, d), w_main, w_gate)
    proj = proj.reshape(b, tp, main)
    gates = gates.reshape(b, tp, LANES)
    gates_t = jnp.swapaxes(gates[:, :, :2 * GDN_V_HEADS], 1, 2)
    beta, cumc, cumr = _gdn_gates(gates, gates_t, a_log, dt_bias, hg)
    qkv = _gdn_prep(proj, w_conv)
    y = _gdn_chunk(qkv, proj, beta, cumc, cumr, w_norm, hg)
    return _proj_res(y.reshape(m, -1), w_out.astype(BF16), h.reshape(m, d)).reshape(b, tp, d)


def _mlstm_gates_kernel(g_ref, gt_ref, bc_ref, br_ref, cumc_ref, rc_ref, rr_ref, *, nc, hg):
    nh = MLSTM_HEADS
    ii, jj = _tri_masks()
    tril = (ii >= jj).astype(F32)
    triu = (ii <= jj).astype(F32)

    def split(raw, axis):
        capped = GATE_SOFTCAP * jnp.tanh(raw / GATE_SOFTCAP)
        if axis == 1:
            i_pre, f_pre = capped[:, :nh], capped[:, nh:2 * nh]
        else:
            i_pre, f_pre = capped[:nh], capped[nh:2 * nh]
        return i_pre, -_softplus(-f_pre)

    for c in range(nc):
        i_c, lf_c = split(g_ref[c * CHUNK:(c + 1) * CHUNK, 0:2 * nh] + bc_ref[...], 1)
        i_r, lf_r = split(gt_ref[:, c * CHUNK:(c + 1) * CHUNK] + br_ref[...], 0)
        if c == 0:
            rows = lax.broadcasted_iota(jnp.int32, (CHUNK, nh), 0)
            cols = lax.broadcasted_iota(jnp.int32, (nh, CHUNK), 1)
            i_c = jnp.where(rows >= ROW0, i_c, -jnp.inf)
            lf_c = jnp.where(rows >= ROW0, lf_c, 0.0)
            i_r = jnp.where(cols >= ROW0, i_r, -jnp.inf)
            lf_r = jnp.where(cols >= ROW0, lf_r, 0.0)
        cum = jnp.dot(tril, lf_c, preferred_element_type=F32, precision=lax.Precision.HIGHEST)
        cumr = jnp.dot(lf_r, triu, preferred_element_type=F32, precision=lax.Precision.HIGHEST)
        rc = i_c - cum
        rr = i_r - cumr
        for q in range(nh // hg):
            cumc_ref[q, c] = cum[:, q * hg:(q + 1) * hg]
            rc_ref[q, c] = rc[:, q * hg:(q + 1) * hg]
            rr_ref[q, c] = rr[q * hg:(q + 1) * hg, :]


def _mlstm_gates(gates, gates_t, bias, hg):
    b, tp, _ = gates.shape
    nc = tp // CHUNK
    nh = MLSTM_HEADS
    ng = nh // hg
    kern = functools.partial(_mlstm_gates_kernel, nc=nc, hg=hg)
    col = jax.ShapeDtypeStruct((b, ng, nc, CHUNK, hg), F32)
    row = jax.ShapeDtypeStruct((b, ng, nc, hg, CHUNK), F32)
    col_spec = pl.BlockSpec((None, ng, nc, CHUNK, hg), lambda i: (i, 0, 0, 0, 0))
    row_spec = pl.BlockSpec((None, ng, nc, hg, CHUNK), lambda i: (i, 0, 0, 0, 0))
    return pl.pallas_call(
        kern,
        grid=(b,),
        in_specs=[
            pl.BlockSpec((None, tp, LANES), lambda i: (i, 0, 0)),
            pl.BlockSpec((None, 2 * nh, tp), lambda i: (i, 0, 0)),
            pl.BlockSpec((1, 2 * nh), lambda i: (0, 0)),
            pl.BlockSpec((2 * nh, 1), lambda i: (0, 0)),
        ],
        out_specs=[col_spec, col_spec, row_spec],
        out_shape=[col, col, row],
        compiler_params=_params("parallel"),
        name="mlstm_gates",
    )(gates, gates_t, bias.reshape(1, 2 * nh), bias.reshape(2 * nh, 1))


def _mlstm_chunk_kernel(q_ref, k_ref, v_ref, og_ref, cumc_ref, rc_ref, rr_ref, wn_ref, y_ref,
                        c_ref, m_ref, *, nc, hg, cps):
    dv = MLSTM_DV
    ii, jj = _tri_masks()
    causal = ii >= jj
    ones_col = (lax.broadcasted_iota(jnp.int32, (CHUNK, LANES), 1) == 0).astype(BF16)
    heads = range(hg)

    c_ref[...] = jnp.zeros_like(c_ref)
    m_ref[...] = jnp.zeros_like(m_ref)

    def step(chunks):
        n = len(chunks)
        units = [(i, h) for i in range(n) for h in heads]
        r0 = [pl.multiple_of(c * CHUNK, CHUNK) for c in chunks]
        cumc = [cumc_ref[c] for c in chunks]
        rcol = [rc_ref[c] for c in chunks]
        rrow = [rr_ref[c] for c in chunks]
        q = {(i, h): q_ref[pl.ds(r0[i], CHUNK), h * LANES:(h + 1) * LANES] for i, h in units}
        k = {(i, h): k_ref[pl.ds(r0[i], CHUNK), h * LANES:(h + 1) * LANES] for i, h in units}
        v_aug = {(i, h): jnp.concatenate(
            [v_ref[pl.ds(r0[i], CHUNK), h * dv:(h + 1) * dv], ones_col], axis=1) for i, h in units}
        qk = {u: _dot_nt(q[u], k[u]) for u in units}
        cc = {(i, h): cumc[i][:, h:h + 1] for i, h in units}
        rc = {(i, h): rcol[i][:, h:h + 1] for i, h in units}
        m_in, keep, kw = {}, {}, {}
        m = [m_ref[h][0:1, 0:1] for h in heads]
        for i, h in units:
            m_in[i, h] = m[h]
            c_last = cc[i, h][CHUNK - 1:CHUNK, :]
            log_keep = c_last + m[h]
            m_new = jnp.maximum(log_keep, c_last + jnp.max(rc[i, h], axis=0, keepdims=True))
            keep[i, h] = jnp.exp(log_keep - m_new)
            w_end = jnp.exp(c_last + rc[i, h] - m_new)
            kw[i, h] = (k[i, h].astype(F32) * w_end).astype(BF16)
            m[h] = m_new
        for h in heads:
            m_ref[h] = jnp.broadcast_to(m[h], m_ref.shape[1:])
        upd = {u: _dot_tn(kw[u], v_aug[u]) for u in units}
        m_row, s_inter, w_intra = {}, {}, {}
        for i, h in units:
            log_w = jnp.where(causal, cc[i, h] + rrow[i][h:h + 1, :], -jnp.inf)
            log_inter = cc[i, h] + m_in[i, h]
            m_row[i, h] = jnp.maximum(log_inter, jnp.max(log_w, axis=-1, keepdims=True))
            s_inter[i, h] = jnp.exp(log_inter - m_row[i, h])
            w_intra[i, h] = (jnp.exp(log_w - m_row[i, h]) * qk[i, h]).astype(BF16)
        wv = {u: _dot(w_intra[u], v_aug[u]) for u in units}
        state = [c_ref[h] for h in heads]
        for i in range(n):
            qc = [_dot(q[i, h], state[h].astype(BF16)) for h in heads]
            for h in heads:
                tot = s_inter[i, h] * qc[h] + wv[i, h]
                den = tot[:, dv:dv + 1]
                inv = 1.0 / jnp.maximum(jnp.abs(den), jnp.exp(-m_row[i, h]))
                hs = tot[:, :dv] * inv
                og = og_ref[pl.ds(r0[i], CHUNK), h * dv:(h + 1) * dv].astype(F32)
                y = _norm_rows(hs, wn_ref[:, h * dv:(h + 1) * dv]) * jax.nn.sigmoid(og)
                y_ref[pl.ds(r0[i], CHUNK), h * dv:(h + 1) * dv] = y.astype(BF16)
                state[h] = keep[i, h] * state[h] + upd[i, h]
        for h in heads:
            c_ref[h] = state[h]

    def body(it, carry):
        step([it * cps + i for i in range(cps)])
        return carry

    lax.fori_loop(0, nc // cps, body, 0)
    if nc % cps:
        step([jnp.int32(nc - nc % cps + i) for i in range(nc % cps)])


def _mlstm_chunk(proj, cumc, rc, rr, w_norm, hg):
    b, tp, _ = proj.shape
    nc = tp // CHUNK
    nh = MLSTM_HEADS
    ng = nh // hg
    wb = LANES * hg
    kern = functools.partial(_mlstm_chunk_kernel, nc=nc, hg=hg, cps=MLSTM_CHUNKS_PER_STEP)
    col_spec = pl.BlockSpec((None, None, nc, CHUNK, hg), lambda i, j: (i, j, 0, 0, 0))
    row_spec = pl.BlockSpec((None, None, nc, hg, CHUNK), lambda i, j: (i, j, 0, 0, 0))
    return pl.pallas_call(
        kern,
        grid=(b, ng),
        in_specs=[
            pl.BlockSpec((None, tp, wb), lambda i, j: (i, 0, j)),
            pl.BlockSpec((None, tp, wb), lambda i, j: (i, 0, ng + j)),
            pl.BlockSpec((None, tp, wb), lambda i, j: (i, 0, 2 * ng + j)),
            pl.BlockSpec((None, tp, wb), lambda i, j: (i, 0, 3 * ng + j)),
            col_spec, col_spec, row_spec,
            pl.BlockSpec((1, wb), lambda i, j: (0, j)),
        ],
        out_specs=pl.BlockSpec((None, tp, wb), lambda i, j: (i, 0, j)),
        out_shape=jax.ShapeDtypeStruct((b, tp, nh * MLSTM_DV), BF16),
        scratch_shapes=[
            pltpu.VMEM((hg, LANES, 2 * MLSTM_DV), F32),
            pltpu.VMEM((hg, 8, LANES), F32),
        ],
        compiler_params=_params("parallel", "arbitrary"),
        name="mlstm_chunk",
    )(proj, proj, proj, proj, cumc, rc, rr, w_norm.reshape(1, nh * MLSTM_DV))


def _pad_heads(w, nh, dh):
    d = w.shape[0]
    w = w.reshape(d, nh, dh)
    return jnp.pad(w, ((0, 0), (0, 0), (0, LANES - dh))).reshape(d, nh * LANES)


def _mlstm_layer(h, nw, w_in, gate_bias, w_norm, w_out):
    b, tp, d = h.shape
    m = b * tp
    nh = MLSTM_HEADS
    qk = nh * MLSTM_DQK
    vd = nh * MLSTM_DV
    hg = 4
    w_main = jnp.concatenate([
        _pad_heads(w_in[:, :qk], nh, MLSTM_DQK),
        _pad_heads(w_in[:, qk:2 * qk] * (MLSTM_DQK ** -0.5), nh, MLSTM_DQK),
        w_in[:, 2 * qk:2 * qk + 2 * vd],
    ], axis=1).astype(BF16)
    w_gate = jnp.pad(w_in[:, 2 * qk + 2 * vd:], ((0, 0), (0, LANES - 2 * nh))).astype(BF16)
    proj, gates = _norm_matmul(h.reshape(m, d), nw.reshape(1, d), w_main, w_gate)
    proj = proj.reshape(b, tp, -1)
    gates = gates.reshape(b, tp, LANES)
    gates_t = jnp.swapaxes(gates[:, :, :2 * nh], 1, 2)
    cumc, rc, rr = _mlstm_gates(gates, gates_t, gate_bias, hg)
    y = _mlstm_chunk(proj, cumc, rc, rr, w_norm, hg)
    return _proj_res(y.reshape(m, vd), w_out.astype(BF16), h.reshape(m, d)).reshape(b, tp, d)


def _mla_proj_kernel(h_ref, nw_ref, win_ref, qn_ref, wuq_ref, kvn_ref, wuk_ref, wuv_ref,
                     qhn_ref, khn_ref, cos_ref, s1_ref, s2_ref, q_out, k_out, v_out):
    xn = _norm_rows(h_ref[...], nw_ref[...]).astype(BF16)
    c = _dot(xn, win_ref[...])
    cq = _norm_rows(c[:, :MLA_Q_RANK], qn_ref[...]).astype(BF16)
    ckv = _norm_rows(c[:, MLA_Q_RANK:MLA_Q_RANK + MLA_KV_RANK], kvn_ref[...]).astype(BF16)
    kr = c[:, MLA_Q_RANK + MLA_KV_RANK:]
    q = _dot(cq, wuq_ref[...])
    kn = _dot(ckv, wuk_ref[...])
    v_out[...] = _dot(ckv, wuv_ref[...]).astype(BF16)
    cos = cos_ref[...]
    s1 = s1_ref[...]
    s2 = s2_ref[...]

    def head_norm_rope(x, gain):
        ms = jnp.sum(x * x, axis=-1, keepdims=True) * (1.0 / MLA_QK)
        x = x * lax.rsqrt(ms + RMS_EPS) * gain
        half = MLA_ROPE // 2
        return x * cos + pltpu.roll(x, LANES - half, 1) * s1 + pltpu.roll(x, half, 1) * s2

    for h in range(MLA_HEADS):
        sl = slice(h * LANES, (h + 1) * LANES)
        q_out[:, sl] = (head_norm_rope(q[:, sl], qhn_ref[...]) * (MLA_QK ** -0.5)).astype(BF16)
        k_out[:, sl] = head_norm_rope(kn[:, sl] + kr, khn_ref[...]).astype(BF16)


def _mla_proj(h, nw, w_in, q_norm, w_uq, kv_norm, w_uk, w_uv, qhn, khn, cos, s1, s2):
    b, tp, d = h.shape
    tt = _pick(tp, (704, 352, 192, 64))
    nh = MLA_HEADS
    full = lambda a: pl.BlockSpec(a.shape, lambda i, j: (0,) * a.ndim)
    tab = pl.BlockSpec((tt, LANES), lambda i, j: (j, 0))
    row = lambda n: pl.BlockSpec((None, tt, n), lambda i, j: (i, j, 0))
    return pl.pallas_call(
        _mla_proj_kernel,
        grid=(b, tp // tt),
        in_specs=[row(d), full(nw), full(w_in), full(q_norm), full(w_uq), full(kv_norm),
                  full(w_uk), full(w_uv), full(qhn), full(khn), tab, tab, tab],
        out_specs=[row(nh * LANES), row(nh * LANES), row(nh * MLA_V)],
        out_shape=[
            jax.ShapeDtypeStruct((b, tp, nh * LANES), BF16),
            jax.ShapeDtypeStruct((b, tp, nh * LANES), BF16),
            jax.ShapeDtypeStruct((b, tp, nh * MLA_V), BF16),
        ],
        compiler_params=_params("parallel", "arbitrary"),
        name="mla_proj",
    )(h, nw, w_in, q_norm, w_uq, kv_norm, w_uk, w_uv, qhn, khn, cos, s1, s2)


def _mla_attn_kernel(q_ref, k_ref, v_ref, o_ref, *, tq, nq):
    for qi in range(nq):
        kend = (qi + 1) * tq
        rows = qi * tq + lax.broadcasted_iota(jnp.int32, (tq, kend), 0)
        cols = lax.broadcasted_iota(jnp.int32, (tq, kend), 1)
        bias = jnp.where(cols <= rows, jnp.where(cols >= ROW0, 0.0, NEG_BIG), NEG_BIG)
        v = v_ref[0:kend, :]
        outs = []
        for r in range(2):
            q = q_ref[qi * tq:(qi + 1) * tq, r * LANES:(r + 1) * LANES]
            k = k_ref[0:kend, r * LANES:(r + 1) * LANES]
            s = _dot_nt(q, k) + bias
            p = jnp.exp(s - jnp.max(s, axis=-1, keepdims=True))
            inv = 1.0 / jnp.sum(p, axis=-1, keepdims=True)
            outs.append(_dot(p.astype(BF16), v) * inv)
        lane = lax.broadcasted_iota(jnp.int32, (tq, LANES), 1)
        o = jnp.where(lane < MLA_V, outs[0], outs[1])
        orow = qi * tq + lax.broadcasted_iota(jnp.int32, (tq, LANES), 0)
        o_ref[qi * tq:(qi + 1) * tq, :] = jnp.where(orow >= ROW0, o, 0.0).astype(BF16)


def _mla_attn(q, k, v):
    b, tp, _ = q.shape
    tq = _pick(tp, (192, 64))
    npair = MLA_HEADS // 2
    kern = functools.partial(_mla_attn_kernel, tq=tq, nq=tp // tq)
    return pl.pallas_call(
        kern,
        grid=(b, npair),
        in_specs=[
            pl.BlockSpec((None, tp, 2 * LANES), lambda i, j: (i, 0, j)),
            pl.BlockSpec((None, tp, 2 * LANES), lambda i, j: (i, 0, j)),
            pl.BlockSpec((None, tp, LANES), lambda i, j: (i, 0, j)),
        ],
        out_specs=pl.BlockSpec((None, tp, LANES), lambda i, j: (i, 0, j)),
        out_shape=jax.ShapeDtypeStruct((b, tp, MLA_HEADS * MLA_V), BF16),
        compiler_params=_params("parallel", "arbitrary"),
        name="mla_attn",
    )(q, k, v)


def _rope_tables(tp):
    half = MLA_ROPE // 2
    pos = jnp.arange(tp, dtype=F32) - float(ROW0)
    inv_freq = ROPE_THETA ** (-jnp.arange(0, MLA_ROPE, 2, dtype=F32) / MLA_ROPE)
    ang = pos[:, None] * inv_freq[None, :]
    cos, sin = jnp.cos(ang), jnp.sin(ang)
    ones = jnp.ones((tp, MLA_NOPE), F32)
    z_nope = jnp.zeros((tp, MLA_NOPE), F32)
    z_half = jnp.zeros((tp, half), F32)
    z_tail = jnp.zeros((tp, LANES - MLA_QK), F32)
    cos_t = jnp.concatenate([ones, cos, cos, z_tail], axis=1)
    s1 = jnp.concatenate([z_nope, -sin, z_half, z_tail], axis=1)
    s2 = jnp.concatenate([z_nope, z_half, sin, z_tail], axis=1)
    return cos_t, s1, s2


def _mla_layer(h, nw, w_in, q_norm, w_uq, kv_norm, w_ukv, q_head_norm, k_head_norm, w_out):
    b, tp, d = h.shape
    m = b * tp
    nh = MLA_HEADS
    lat = MLA_Q_RANK + MLA_KV_RANK
    w_in_p = jnp.concatenate([
        w_in[:, :lat],
        jnp.zeros((d, MLA_NOPE), F32), w_in[:, lat:], jnp.zeros((d, LANES - MLA_QK), F32),
    ], axis=1).astype(BF16)
    w_uq_p = _pad_heads(w_uq, nh, MLA_QK).astype(BF16)
    w_ukv3 = w_ukv.reshape(MLA_KV_RANK, nh, MLA_NOPE + MLA_V)
    w_uk_p = _pad_heads(w_ukv3[:, :, :MLA_NOPE].reshape(MLA_KV_RANK, nh * MLA_NOPE), nh, MLA_NOPE).astype(BF16)
    w_uv = w_ukv3[:, :, MLA_NOPE:].reshape(MLA_KV_RANK, nh * MLA_V).astype(BF16)
    pad_gain = lambda g: jnp.pad(g, (0, LANES - MLA_QK)).reshape(1, LANES)
    cos, s1, s2 = _rope_tables(tp)
    q, k, v = _mla_proj(h, nw.reshape(1, d), w_in_p, q_norm.reshape(1, -1), w_uq_p,
                        kv_norm.reshape(1, -1), w_uk_p, w_uv,
                        pad_gain(q_head_norm), pad_gain(k_head_norm), cos, s1, s2)
    o = _mla_attn(q, k, v)
    return _proj_res(o.reshape(m, nh * MLA_V), w_out.astype(BF16), h.reshape(m, d)).reshape(b, tp, d)


def kernel(x, meta_tokens, attn_norm, ffn_norm, ff_up, ff_down, gdn_in, gdn_conv, gdn_a_log, gdn_dt_bias, gdn_norm, gdn_out, mlstm_in, mlstm_gate_bias, mlstm_norm, mlstm_out, mla_in, mla_q_norm, mla_uq, mla_kv_norm, mla_ukv, mla_q_head_norm, mla_k_head_norm, mla_out):
    b, t, d = x.shape
    depth = attn_norm.shape[0]
    meta = jnp.broadcast_to(meta_tokens[None].astype(x.dtype), (b, N_META, d))
    h = jnp.concatenate([jnp.zeros((b, LEAD_PAD, d), x.dtype), meta, x], axis=1)
    tp = h.shape[1]
    for layer in range(depth):
        kind, j = layer % N_MIXERS, layer // N_MIXERS
        if kind == 0:
            h = _gdn_layer(h, attn_norm[layer], gdn_in[j], gdn_conv[j], gdn_a_log[j],
                           gdn_dt_bias[j], gdn_norm[j], gdn_out[j])
        elif kind == 1:
            h = _mlstm_layer(h, attn_norm[layer], mlstm_in[j], mlstm_gate_bias[j],
                             mlstm_norm[j], mlstm_out[j])
        else:
            h = _mla_layer(h, attn_norm[layer], mla_in[j], mla_q_norm[j], mla_uq[j],
                           mla_kv_norm[j], mla_ukv[j], mla_q_head_norm[j],
                           mla_k_head_norm[j], mla_out[j])
        h = _mlp(h.reshape(b * tp, d), ffn_norm[layer].reshape(1, d),
                 ff_up[layer].astype(BF16), ff_down[layer].astype(BF16)).reshape(b, tp, d)
    return h[:, LEAD_PAD + N_META:]
```

```python
import functools

import jax
import jax.numpy as jnp
import numpy as np
from jax import lax
from jax.experimental import pallas as pl
from jax.experimental.pallas import tpu as pltpu

F32 = jnp.float32
BF16 = jnp.bfloat16

N_META = 16
CHUNK = 64
LEAD_PAD = (-N_META) % CHUNK
ROW0 = LEAD_PAD
RMS_EPS = 1e-6
N_MIXERS = 3

GDN_QK_HEADS = 8
GDN_V_HEADS = 16
GDN_HEAD_DIM = 128
GDN_CONV = 4
GDN_HEADS_PER_STEP = 8
GDN_CHUNKS_PER_STEP = 2

MLSTM_HEADS = 8
MLSTM_DQK = 64
MLSTM_DV = 128
GATE_SOFTCAP = 15.0
MLSTM_CHUNKS_PER_STEP = 2

MLA_HEADS = 16
MLA_NOPE = 64
MLA_ROPE = 32
MLA_QK = MLA_NOPE + MLA_ROPE
MLA_V = 64
MLA_Q_RANK = 384
MLA_KV_RANK = 256
ROPE_THETA = 10000.0

LANES = 128
NEG_BIG = -1e30
VMEM_LIMIT = 56 * 1024 * 1024


def _pick(n, candidates):
    for c in candidates:
        if n % c == 0:
            return c
    raise ValueError(f"no tile for {n} in {candidates}")


def _params(*sem):
    return pltpu.CompilerParams(dimension_semantics=sem, vmem_limit_bytes=VMEM_LIMIT)


def _norm_rows(x, w):
    ms = jnp.mean(x * x, axis=-1, keepdims=True)
    return x * lax.rsqrt(ms + RMS_EPS) * w


def _softplus(x):
    return jnp.maximum(x, 0.0) + jnp.log1p(jnp.exp(-jnp.abs(x)))


def _dot(a, b):
    return jnp.dot(a, b, preferred_element_type=F32)


def _dot_nt(a, b):
    return lax.dot_general(a, b, (((1,), (1,)), ((), ())), preferred_element_type=F32)


def _dot_tn(a, b):
    return lax.dot_general(a, b, (((0,), (0,)), ((), ())), preferred_element_type=F32)


def _norm_matmul_kernel(x_ref, nw_ref, w_ref, wg_ref, o_ref, g_ref, xn_ref):
    @pl.when(pl.program_id(1) == 0)
    def _():
        xn = _norm_rows(x_ref[...], nw_ref[...]).astype(BF16)
        xn_ref[...] = xn
        g_ref[...] = _dot(xn, wg_ref[...])

    o_ref[...] = _dot(xn_ref[...], w_ref[...]).astype(o_ref.dtype)


def _norm_matmul(x, nw, w, wg):
    m, d = x.shape
    n = w.shape[1]
    tm = _pick(m, (1024, 768, 512, 384, 256, 192, 128, 64))
    tn = _pick(n, (2048, 1024, 512, 256, 128))
    return pl.pallas_call(
        _norm_matmul_kernel,
        grid=(m // tm, n // tn),
        in_specs=[
            pl.BlockSpec((tm, d), lambda i, j: (i, 0)),
            pl.BlockSpec((1, d), lambda i, j: (0, 0)),
            pl.BlockSpec((d, tn), lambda i, j: (0, j)),
            pl.BlockSpec((d, LANES), lambda i, j: (0, 0)),
        ],
        out_specs=[
            pl.BlockSpec((tm, tn), lambda i, j: (i, j)),
            pl.BlockSpec((tm, LANES), lambda i, j: (i, 0)),
        ],
        out_shape=[
            jax.ShapeDtypeStruct((m, n), BF16),
            jax.ShapeDtypeStruct((m, LANES), F32),
        ],
        scratch_shapes=[pltpu.VMEM((tm, d), BF16)],
        compiler_params=_params("parallel", "arbitrary"),
        name="norm_matmul",
    )(x, nw, w, wg)


def _proj_res_kernel(y_ref, w_ref, h_ref, o_ref):
    o_ref[...] = h_ref[...] + _dot(y_ref[...], w_ref[...])


def _proj_res(y, w, h):
    m, k = y.shape
    d = w.shape[1]
    tm = _pick(m, (1024, 768, 512, 384, 256, 192, 128, 64))
    return pl.pallas_call(
        _proj_res_kernel,
        grid=(m // tm,),
        in_specs=[
            pl.BlockSpec((tm, k), lambda i: (i, 0)),
            pl.BlockSpec((k, d), lambda i: (0, 0)),
            pl.BlockSpec((tm, d), lambda i: (i, 0)),
        ],
        out_specs=pl.BlockSpec((tm, d), lambda i: (i, 0)),
        out_shape=jax.ShapeDtypeStruct((m, d), F32),
        compiler_params=_params("parallel"),
        name="proj_res",
    )(y, w, h)


def _mlp_kernel(h_ref, nw_ref, wu_ref, wd_ref, o_ref, xn_ref):
    j = pl.program_id(1)

    @pl.when(j == 0)
    def _():
        h = h_ref[...]
        xn_ref[...] = _norm_rows(h, nw_ref[...]).astype(BF16)
        o_ref[...] = h

    a = jnp.maximum(_dot(xn_ref[...], wu_ref[...]), 0.0)
    o_ref[...] += _dot((a * a).astype(BF16), wd_ref[...])


def _mlp(h, nw, wu, wd):
    m, d = h.shape
    f = wu.shape[1]
    tm = _pick(m, (1024, 768, 512, 384, 256, 192, 128, 64))
    tf = _pick(f, (1024, 512, 256, 128))
    return pl.pallas_call(
        _mlp_kernel,
        grid=(m // tm, f // tf),
        in_specs=[
            pl.BlockSpec((tm, d), lambda i, j: (i, 0)),
            pl.BlockSpec((1, d), lambda i, j: (0, 0)),
            pl.BlockSpec((d, tf), lambda i, j: (0, j)),
            pl.BlockSpec((tf, d), lambda i, j: (j, 0)),
        ],
        out_specs=pl.BlockSpec((tm, d), lambda i, j: (i, 0)),
        out_shape=jax.ShapeDtypeStruct((m, d), F32),
        scratch_shapes=[pltpu.VMEM((tm, d), BF16)],
        compiler_params=_params("parallel", "arbitrary"),
        name="mlp",
    )(h, nw, wu, wd)


def _gdn_prep_kernel(x_ref, w_ref, o_ref, *, n_q_tiles, n_qk_tiles, tc):
    c = pl.program_id(1)
    x = x_ref[...].astype(F32)
    w = w_ref[...]
    y = x * w[GDN_CONV - 1:GDN_CONV, :]
    for s in range(1, GDN_CONV):
        y = y + pltpu.roll(x, s, 0) * w[GDN_CONV - 1 - s:GDN_CONV - s, :]
    y = y * jax.nn.sigmoid(y)
    rows = lax.broadcasted_iota(jnp.int32, y.shape, 0)
    y = jnp.where(rows >= ROW0, y, 0.0)

    @pl.when(c < n_qk_tiles)
    def _():
        scale = jnp.where(c < n_q_tiles, GDN_HEAD_DIM ** -0.5, 1.0).astype(F32)
        for i in range(tc // GDN_HEAD_DIM):
            yi = y[:, i * GDN_HEAD_DIM:(i + 1) * GDN_HEAD_DIM]
            ss = jnp.sum(yi * yi, axis=-1, keepdims=True)
            o_ref[:, i * GDN_HEAD_DIM:(i + 1) * GDN_HEAD_DIM] = (
                yi * (lax.rsqrt(ss + RMS_EPS) * scale)).astype(BF16)

    @pl.when(c >= n_qk_tiles)
    def _():
        o_ref[...] = y.astype(BF16)


def _gdn_prep(proj, w_conv):
    b, tp, _ = proj.shape
    cdim = w_conv.shape[1]
    tc = 256
    qk = GDN_QK_HEADS * GDN_HEAD_DIM
    kern = functools.partial(_gdn_prep_kernel, n_q_tiles=qk // tc, n_qk_tiles=2 * qk // tc, tc=tc)
    return pl.pallas_call(
        kern,
        grid=(b, cdim // tc),
        in_specs=[
            pl.BlockSpec((None, tp, tc), lambda i, c: (i, 0, c)),
            pl.BlockSpec((GDN_CONV, tc), lambda i, c: (0, c)),
        ],
        out_specs=pl.BlockSpec((None, tp, tc), lambda i, c: (i, 0, c)),
        out_shape=jax.ShapeDtypeStruct((b, tp, cdim), BF16),
        compiler_params=_params("parallel", "arbitrary"),
        name="gdn_prep",
    )(proj, w_conv)


def _tri_masks():
    ii = lax.broadcasted_iota(jnp.int32, (CHUNK, CHUNK), 0)
    jj = lax.broadcasted_iota(jnp.int32, (CHUNK, CHUNK), 1)
    return ii, jj


def _gdn_gates_kernel(g_ref, gt_ref, alc_ref, dtc_ref, alr_ref, dtr_ref,
                      beta_ref, cumc_ref, cumr_ref, *, nc, hg):
    nh = GDN_V_HEADS
    ii, jj = _tri_masks()
    tril = (ii >= jj).astype(F32)
    triu = (ii <= jj).astype(F32)
    neg_a_c = -jnp.exp(alc_ref[...])
    neg_a_r = -jnp.exp(alr_ref[...])
    for c in range(nc):
        blk = g_ref[c * CHUNK:(c + 1) * CHUNK, :]
        beta = jax.nn.sigmoid(blk[:, 0:nh])
        g = neg_a_c * _softplus(blk[:, nh:2 * nh] + dtc_ref[...])
        gr = neg_a_r * _softplus(gt_ref[nh:2 * nh, c * CHUNK:(c + 1) * CHUNK] + dtr_ref[...])
        if c == 0:
            rows = lax.broadcasted_iota(jnp.int32, (CHUNK, nh), 0)
            cols = lax.broadcasted_iota(jnp.int32, (nh, CHUNK), 1)
            beta = jnp.where(rows >= ROW0, beta, 0.0)
            g = jnp.where(rows >= ROW0, g, 0.0)
            gr = jnp.where(cols >= ROW0, gr, 0.0)
        cum = jnp.dot(tril, g, preferred_element_type=F32, precision=lax.Precision.HIGHEST)
        cumr = jnp.dot(gr, triu, preferred_element_type=F32, precision=lax.Precision.HIGHEST)
        for q in range(nh // hg):
            beta_ref[q, c] = beta[:, q * hg:(q + 1) * hg]
            cumc_ref[q, c] = cum[:, q * hg:(q + 1) * hg]
            cumr_ref[q, c] = cumr[q * hg:(q + 1) * hg, :]


def _gdn_gates(gates, gates_t, a_log, dt_bias, hg):
    b, tp, _ = gates.shape
    nc = tp // CHUNK
    nh = GDN_V_HEADS
    ng = nh // hg
    kern = functools.partial(_gdn_gates_kernel, nc=nc, hg=hg)
    col = jax.ShapeDtypeStruct((b, ng, nc, CHUNK, hg), F32)
    row = jax.ShapeDtypeStruct((b, ng, nc, hg, CHUNK), F32)
    col_spec = pl.BlockSpec((None, ng, nc, CHUNK, hg), lambda i: (i, 0, 0, 0, 0))
    row_spec = pl.BlockSpec((None, ng, nc, hg, CHUNK), lambda i: (i, 0, 0, 0, 0))
    small = lambda shape: pl.BlockSpec(shape, lambda i: (0, 0))
    return pl.pallas_call(
        kern,
        grid=(b,),
        in_specs=[
            pl.BlockSpec((None, tp, LANES), lambda i: (i, 0, 0)),
            pl.BlockSpec((None, 2 * nh, tp), lambda i: (i, 0, 0)),
            small((1, nh)), small((1, nh)), small((nh, 1)), small((nh, 1)),
        ],
        out_specs=[col_spec, col_spec, row_spec],
        out_shape=[col, col, row],
        compiler_params=_params("parallel"),
        name="gdn_gates",
    )(gates, gates_t, a_log.reshape(1, nh), dt_bias.reshape(1, nh),
      a_log.reshape(nh, 1), dt_bias.reshape(nh, 1))


def _gdn_chunk_kernel(q_ref, k_ref, v_ref, z_ref, beta_ref, cumc_ref, cumr_ref, wn_ref, y_ref,
                      s_ref, u_ref, wq_ref, kd_ref, at_ref, *, nc, hg, cps):
    hd = GDN_HEAD_DIM
    ii, jj = _tri_masks()
    causal = ii >= jj
    strict = ii > jj
    eye = (ii == jj).astype(F32)
    blk = [(ii >> l) == (jj >> l) for l in range(1, CHUNK.bit_length())]
    wn = wn_ref[...]

    heads = range(hg)
    pairs = range(hg // 2)

    def prepare(chunks):
        units = [(i, h) for i in range(len(chunks)) for h in heads]
        r0 = [pl.multiple_of(c * CHUNK, CHUNK) for c in chunks]
        cumc = [cumc_ref[c] for c in chunks]
        betac = [beta_ref[c] for c in chunks]
        cumr = [cumr_ref[c] for c in chunks]
        q = [[q_ref[pl.ds(r, CHUNK), p * hd:(p + 1) * hd] for p in pairs] for r in r0]
        k = [[k_ref[pl.ds(r, CHUNK), p * hd:(p + 1) * hd] for p in pairs] for r in r0]
        qkk = [[_dot_nt(jnp.concatenate([q[i][p], k[i][p]], axis=0), k[i][p]) for p in pairs]
               for i in range(len(chunks))]
        cc = {u: jnp.broadcast_to(cumc[u[0]][:, u[1]:u[1] + 1], (CHUNK, hd)) for u in units}
        bc = {u: jnp.broadcast_to(betac[u[0]][:, u[1]:u[1] + 1], (CHUNK, hd)) for u in units}
        decay = {(i, h): jnp.exp(jnp.where(causal, cc[i, h][:, :CHUNK] - cumr[i][h:h + 1, :], -jnp.inf))
                 for i, h in units}
        a = {(i, h): jnp.where(strict, bc[i, h][:, :CHUNK] * qkk[i][h // 2][CHUNK:] * decay[i, h], 0.0)
             for i, h in units}
        for i, h in units:
            at_ref[i, h] = (qkk[i][h // 2][:CHUNK] * decay[i, h]).astype(BF16)
        t = {u: eye - jnp.where(blk[0], a[u], 0.0) for u in units}
        for lvl in range(1, len(blk)):
            a_off = {u: jnp.where(blk[lvl], jnp.where(blk[lvl - 1], 0.0, a[u]), 0.0).astype(BF16)
                     for u in units}
            tb = {u: t[u].astype(BF16) for u in units}
            x = {u: _dot(tb[u], a_off[u]).astype(BF16) for u in units}
            t = {u: t[u] - _dot(x[u], tb[u]) for u in units}
        ec = {u: jnp.exp(cc[u]) for u in units}
        rhs = {}
        for i, h in units:
            v = v_ref[pl.ds(r0[i], CHUNK), h * hd:(h + 1) * hd].astype(F32)
            kf = k[i][h // 2].astype(F32)
            rhs[i, h] = jnp.concatenate([(v * bc[i, h]).astype(BF16),
                                         (kf * (bc[i, h] * ec[i, h])).astype(BF16)], axis=1)
        sol = {u: _dot(t[u].astype(BF16), rhs[u]) for u in units}
        for i, h in units:
            qf = q[i][h // 2].astype(F32)
            kf = k[i][h // 2].astype(F32)
            u_ref[i, h] = sol[i, h][:, :hd]
            wq_ref[i, h] = jnp.concatenate(
                [sol[i, h][:, hd:].astype(BF16), (qf * ec[i, h]).astype(BF16)], axis=0)
            kd_ref[i, h] = (kf * jnp.exp(cc[i, h][CHUNK - 1:CHUNK, :] - cc[i, h])).astype(BF16)

    def recur(c, slot, s):
        r0 = pl.multiple_of(c * CHUNK, CHUNK)
        cumc = cumc_ref[c]
        ws = [_dot(wq_ref[slot, h], s[h].astype(BF16)) for h in heads]
        v_new = [(u_ref[slot, h] - ws[h][:CHUNK]).astype(BF16) for h in heads]
        upd = [_dot_tn(kd_ref[slot, h], v_new[h]) for h in heads]
        o = [ws[h][CHUNK:] + _dot(at_ref[slot, h], v_new[h]) for h in heads]
        s_new = [s[h] * jnp.exp(cumc[CHUNK - 1:CHUNK, h:h + 1]) + upd[h] for h in heads]
        for h in heads:
            z = z_ref[pl.ds(r0, CHUNK), h * hd:(h + 1) * hd].astype(F32)
            y = _norm_rows(o[h], wn) * (z * jax.nn.sigmoid(z))
            y_ref[pl.ds(r0, CHUNK), h * hd:(h + 1) * hd] = y.astype(BF16)
        return s_new

    s_ref[...] = jnp.zeros_like(s_ref)
    prepare([jnp.int32(i) for i in range(min(cps, nc))])

    def body(it, carry):
        c0 = it * cps
        s = [s_ref[h] for h in heads]
        for i in range(cps):
            s = recur(c0 + i, i, s)
        for h in heads:
            s_ref[h] = s[h]
        prepare([jnp.minimum(c0 + cps + i, nc - 1) for i in range(cps)])
        return carry

    lax.fori_loop(0, nc // cps, body, 0)
    s = [s_ref[h] for h in heads]
    for i in range(nc % cps):
        s = recur(jnp.int32(nc - nc % cps + i), i, s)


def _gdn_chunk(qkv, proj, beta, cumc, cumr, w_norm, hg):
    b, tp, _ = qkv.shape
    nc = tp // CHUNK
    hd = GDN_HEAD_DIM
    ng = GDN_V_HEADS // hg
    wqk = hd * hg // 2
    wv = hd * hg
    qk_dim = GDN_QK_HEADS * hd
    v_dim = GDN_V_HEADS * hd
    cps = GDN_CHUNKS_PER_STEP
    kern = functools.partial(_gdn_chunk_kernel, nc=nc, hg=hg, cps=cps)
    col_spec = pl.BlockSpec((None, None, nc, CHUNK, hg), lambda i, j: (i, j, 0, 0, 0))
    row_spec = pl.BlockSpec((None, None, nc, hg, CHUNK), lambda i, j: (i, j, 0, 0, 0))
    return pl.pallas_call(
        kern,
        grid=(b, ng),
        in_specs=[
            pl.BlockSpec((None, tp, wqk), lambda i, j: (i, 0, j)),
            pl.BlockSpec((None, tp, wqk), lambda i, j: (i, 0, qk_dim // wqk + j)),
            pl.BlockSpec((None, tp, wv), lambda i, j: (i, 0, 2 * qk_dim // wv + j)),
            pl.BlockSpec((None, tp, wv), lambda i, j: (i, 0, (2 * qk_dim + v_dim) // wv + j)),
            col_spec, col_spec, row_spec,
            pl.BlockSpec((1, hd), lambda i, j: (0, 0)),
        ],
        out_specs=pl.BlockSpec((None, tp, wv), lambda i, j: (i, 0, j)),
        out_shape=jax.ShapeDtypeStruct((b, tp, v_dim), BF16),
        scratch_shapes=[
            pltpu.VMEM((hg, hd, hd), F32),
            pltpu.VMEM((cps, hg, CHUNK, hd), F32),
            pltpu.VMEM((cps, hg, 2 * CHUNK, hd), BF16),
            pltpu.VMEM((cps, hg, CHUNK, hd), BF16),
            pltpu.VMEM((cps, hg, CHUNK, CHUNK), BF16),
        ],
        compiler_params=_params("parallel", "arbitrary"),
        name="gdn_chunk",
    )(qkv, qkv, qkv, proj, beta, cumc, cumr, w_norm.reshape(1, hd))


def _gdn_layer(h, nw, w_in, w_conv, a_log, dt_bias, w_norm, w_out):
    b, tp, d = h.shape
    m = b * tp
    conv_dim = 2 * GDN_QK_HEADS * GDN_HEAD_DIM + GDN_V_HEADS * GDN_HEAD_DIM
    main = conv_dim + GDN_V_HEADS * GDN_HEAD_DIM
    hg = GDN_HEADS_PER_STEP
    w_main = w_in[:, :main].astype(BF16)
    w_gate = jnp.pad(w_in[:, main:], ((0, 0), (0, LANES - 2 * GDN_V_HEADS))).astype(BF16)
    proj, gates = _norm_matmul(h.reshape(m, d), nw.reshape(1, d), w_main, w_gate)
    proj = proj.reshape(b, tp, main)
    gates = gates.reshape(b, tp, LANES)
    gates_t = jnp.swapaxes(gates[:, :, :2 * GDN_V_HEADS], 1, 2)
    beta, cumc, cumr = _gdn_gates(gates, gates_t, a_log, dt_bias, hg)
    qkv = _gdn_prep(proj, w_conv)
    y = _gdn_chunk(qkv, proj, beta, cumc, cumr, w_norm, hg)
    return _proj_res(y.reshape(m, -1), w_out.astype(BF16), h.reshape(m, d)).reshape(b, tp, d)


def _mlstm_gates_kernel(g_ref, gt_ref, bc_ref, br_ref, cumc_ref, rc_ref, rr_ref, *, nc, hg):
    nh = MLSTM_HEADS
    ii, jj = _tri_masks()
    tril = (ii >= jj).astype(F32)
    triu = (ii <= jj).astype(F32)

    def split(raw, axis):
        capped = GATE_SOFTCAP * jnp.tanh(raw / GATE_SOFTCAP)
        if axis == 1:
            i_pre, f_pre = capped[:, :nh], capped[:, nh:2 * nh]
        else:
            i_pre, f_pre = capped[:nh], capped[nh:2 * nh]
        return i_pre, -_softplus(-f_pre)

    for c in range(nc):
        i_c, lf_c = split(g_ref[c * CHUNK:(c + 1) * CHUNK, 0:2 * nh] + bc_ref[...], 1)
        i_r, lf_r = split(gt_ref[:, c * CHUNK:(c + 1) * CHUNK] + br_ref[...], 0)
        if c == 0:
            rows = lax.broadcasted_iota(jnp.int32, (CHUNK, nh), 0)
            cols = lax.broadcasted_iota(jnp.int32, (nh, CHUNK), 1)
            i_c = jnp.where(rows >= ROW0, i_c, -jnp.inf)
            lf_c = jnp.where(rows >= ROW0, lf_c, 0.0)
            i_r = jnp.where(cols >= ROW0, i_r, -jnp.inf)
            lf_r = jnp.where(cols >= ROW0, lf_r, 0.0)
        cum = jnp.dot(tril, lf_c, preferred_element_type=F32, precision=lax.Precision.HIGHEST)
        cumr = jnp.dot(lf_r, triu, preferred_element_type=F32, precision=lax.Precision.HIGHEST)
        rc = i_c - cum
        rr = i_r - cumr
        for q in range(nh // hg):
            cumc_ref[q, c] = cum[:, q * hg:(q + 1) * hg]
            rc_ref[q, c] = rc[:, q * hg:(q + 1) * hg]
            rr_ref[q, c] = rr[q * hg:(q + 1) * hg, :]


def _mlstm_gates(gates, gates_t, bias, hg):
    b, tp, _ = gates.shape
    nc = tp // CHUNK
    nh = MLSTM_HEADS
    ng = nh // hg
    kern = functools.partial(_mlstm_gates_kernel, nc=nc, hg=hg)
    col = jax.ShapeDtypeStruct((b, ng, nc, CHUNK, hg), F32)
    row = jax.ShapeDtypeStruct((b, ng, nc, hg, CHUNK), F32)
    col_spec = pl.BlockSpec((None, ng, nc, CHUNK, hg), lambda i: (i, 0, 0, 0, 0))
    row_spec = pl.BlockSpec((None, ng, nc, hg, CHUNK), lambda i: (i, 0, 0, 0, 0))
    return pl.pallas_call(
        kern,
        grid=(b,),
        in_specs=[
            pl.BlockSpec((None, tp, LANES), lambda i: (i, 0, 0)),
            pl.BlockSpec((None, 2 * nh, tp), lambda i: (i, 0, 0)),
            pl.BlockSpec((1, 2 * nh), lambda i: (0, 0)),
            pl.BlockSpec((2 * nh, 1), lambda i: (0, 0)),
        ],
        out_specs=[col_spec, col_spec, row_spec],
        out_shape=[col, col, row],
        compiler_params=_params("parallel"),
        name="mlstm_gates",
    )(gates, gates_t, bias.reshape(1, 2 * nh), bias.reshape(2 * nh, 1))


def _mlstm_chunk_kernel(q_ref, k_ref, v_ref, og_ref, cumc_ref, rc_ref, rr_ref, wn_ref, y_ref,
                        c_ref, m_ref, *, nc, hg, cps):
    dv = MLSTM_DV
    ii, jj = _tri_masks()
    causal = ii >= jj
    ones_col = jnp.ones((CHUNK, LANES), BF16)
    heads = range(hg)

    c_ref[...] = jnp.zeros_like(c_ref)
    m_ref[...] = jnp.zeros_like(m_ref)

    def step(chunks):
        n = len(chunks)
        units = [(i, h) for i in range(n) for h in heads]
        r0 = [pl.multiple_of(c * CHUNK, CHUNK) for c in chunks]
        cumc = [cumc_ref[c] for c in chunks]
        rcol = [rc_ref[c] for c in chunks]
        rrow = [rr_ref[c] for c in chunks]
        q = {(i, h): q_ref[pl.ds(r0[i], CHUNK), h * LANES:(h + 1) * LANES] for i, h in units}
        k = {(i, h): k_ref[pl.ds(r0[i], CHUNK), h * LANES:(h + 1) * LANES] for i, h in units}
        v_aug = {(i, h): jnp.concatenate(
            [v_ref[pl.ds(r0[i], CHUNK), h * dv:(h + 1) * dv], ones_col], axis=1) for i, h in units}
        qk = {u: _dot_nt(q[u], k[u]) for u in units}
        cc = {(i, h): jnp.broadcast_to(cumc[i][:, h:h + 1], (CHUNK, LANES)) for i, h in units}
        rc = {(i, h): jnp.broadcast_to(rcol[i][:, h:h + 1], (CHUNK, LANES)) for i, h in units}
        rr = {(i, h): rrow[i][h:h + 1, :] for i, h in units}
        rmax = {u: jnp.max(jnp.where(causal, rr[u], -jnp.inf), axis=-1, keepdims=True) for u in units}
        m_in, keep, kw = {}, {}, {}
        m = [m_ref[h][0:1, :] for h in heads]
        for i, h in units:
            m_in[i, h] = m[h]
            c_last = cc[i, h][CHUNK - 1:CHUNK, :]
            log_keep = c_last + m[h]
            m_new = jnp.maximum(log_keep, c_last + jnp.max(rc[i, h], axis=0, keepdims=True))
            keep[i, h] = jnp.exp(log_keep - m_new)
            w_end = jnp.exp(c_last + rc[i, h] - m_new)
            kw[i, h] = (k[i, h].astype(F32) * w_end).astype(BF16)
            m[h] = m_new
        for h in heads:
            m_ref[h] = jnp.broadcast_to(m[h], m_ref.shape[1:])
        upd = {u: _dot_tn(kw[u], v_aug[u]) for u in units}
        g, w_intra = {}, {}
        for u in units:
            g[u] = jnp.maximum(m_in[u], rmax[u])
            w_intra[u] = (jnp.exp(jnp.where(causal, rr[u] - g[u][:, :CHUNK], -jnp.inf))
                          * qk[u]).astype(BF16)
        wv = {u: _dot(w_intra[u], v_aug[u]) for u in units}
        state = [c_ref[h] for h in heads]
        for i in range(n):
            qc = [_dot(q[i, h], state[h].astype(BF16)) for h in heads]
            for h in heads:
                s_inter = jnp.exp(m_in[i, h] - g[i, h])
                tot = jnp.concatenate([s_inter, s_inter], axis=1) * qc[h] + wv[i, h]
                den = tot[:, dv:]
                inv = 1.0 / jnp.maximum(jnp.abs(den), jnp.exp(-(cc[i, h] + g[i, h])))
                hs = tot[:, :dv] * inv
                og = og_ref[pl.ds(r0[i], CHUNK), h * dv:(h + 1) * dv].astype(F32)
                y = _norm_rows(hs, wn_ref[:, h * dv:(h + 1) * dv]) * jax.nn.sigmoid(og)
                y_ref[pl.ds(r0[i], CHUNK), h * dv:(h + 1) * dv] = y.astype(BF16)
                state[h] = jnp.concatenate([keep[i, h], keep[i, h]], axis=1) * state[h] + upd[i, h]
        for h in heads:
            c_ref[h] = state[h]

    def body(it, carry):
        step([it * cps + i for i in range(cps)])
        return carry

    lax.fori_loop(0, nc // cps, body, 0)
    if nc % cps:
        step([jnp.int32(nc - nc % cps + i) for i in range(nc % cps)])


def _mlstm_chunk(proj, cumc, rc, rr, w_norm, hg):
    b, tp, _ = proj.shape
    nc = tp // CHUNK
    nh = MLSTM_HEADS
    ng = nh // hg
    wb = LANES * hg
    kern = functools.partial(_mlstm_chunk_kernel, nc=nc, hg=hg, cps=MLSTM_CHUNKS_PER_STEP)
    col_spec = pl.BlockSpec((None, None, nc, CHUNK, hg), lambda i, j: (i, j, 0, 0, 0))
    row_spec = pl.BlockSpec((None, None, nc, hg, CHUNK), lambda i, j: (i, j, 0, 0, 0))
    return pl.pallas_call(
        kern,
        grid=(b, ng),
        in_specs=[
            pl.BlockSpec((None, tp, wb), lambda i, j: (i, 0, j)),
            pl.BlockSpec((None, tp, wb), lambda i, j: (i, 0, ng + j)),
            pl.BlockSpec((None, tp, wb), lambda i, j: (i, 0, 2 * ng + j)),
            pl.BlockSpec((None, tp, wb), lambda i, j: (i, 0, 3 * ng + j)),
            col_spec, col_spec, row_spec,
            pl.BlockSpec((1, wb), lambda i, j: (0, j)),
        ],
        out_specs=pl.BlockSpec((None, tp, wb), lambda i, j: (i, 0, j)),
        out_shape=jax.ShapeDtypeStruct((b, tp, nh * MLSTM_DV), BF16),
        scratch_shapes=[
            pltpu.VMEM((hg, LANES, 2 * MLSTM_DV), F32),
            pltpu.VMEM((hg, 8, LANES), F32),
        ],
        compiler_params=_params("parallel", "arbitrary"),
        name="mlstm_chunk",
    )(proj, proj, proj, proj, cumc, rc, rr, w_norm.reshape(1, nh * MLSTM_DV))


def _pad_heads(w, nh, dh):
    d = w.shape[0]
    w = w.reshape(d, nh, dh)
    return jnp.pad(w, ((0, 0), (0, 0), (0, LANES - dh))).reshape(d, nh * LANES)


def _mlstm_layer(h, nw, w_in, gate_bias, w_norm, w_out):
    b, tp, d = h.shape
    m = b * tp
    nh = MLSTM_HEADS
    qk = nh * MLSTM_DQK
    vd = nh * MLSTM_DV
    hg = 4
    w_main = jnp.concatenate([
        _pad_heads(w_in[:, :qk], nh, MLSTM_DQK),
        _pad_heads(w_in[:, qk:2 * qk] * (MLSTM_DQK ** -0.5), nh, MLSTM_DQK),
        w_in[:, 2 * qk:2 * qk + 2 * vd],
    ], axis=1).astype(BF16)
    w_gate = jnp.pad(w_in[:, 2 * qk + 2 * vd:], ((0, 0), (0, LANES - 2 * nh))).astype(BF16)
    proj, gates = _norm_matmul(h.reshape(m, d), nw.reshape(1, d), w_main, w_gate)
    proj = proj.reshape(b, tp, -1)
    gates = gates.reshape(b, tp, LANES)
    gates_t = jnp.swapaxes(gates[:, :, :2 * nh], 1, 2)
    cumc, rc, rr = _mlstm_gates(gates, gates_t, gate_bias, hg)
    y = _mlstm_chunk(proj, cumc, rc, rr, w_norm, hg)
    return _proj_res(y.reshape(m, vd), w_out.astype(BF16), h.reshape(m, d)).reshape(b, tp, d)


def _mla_proj_kernel(h_ref, nw_ref, win_ref, qn_ref, wuq_ref, kvn_ref, wuk_ref, wuv_ref,
                     qhn_ref, khn_ref, cos_ref, s1_ref, s2_ref, q_out, k_out, v_out):
    xn = _norm_rows(h_ref[...], nw_ref[...]).astype(BF16)
    c = _dot(xn, win_ref[...])
    cq = _norm_rows(c[:, :MLA_Q_RANK], qn_ref[...]).astype(BF16)
    ckv = _norm_rows(c[:, MLA_Q_RANK:MLA_Q_RANK + MLA_KV_RANK], kvn_ref[...]).astype(BF16)
    kr = c[:, MLA_Q_RANK + MLA_KV_RANK:]
    q = _dot(cq, wuq_ref[...])
    kn = _dot(ckv, wuk_ref[...])
    v_out[...] = _dot(ckv, wuv_ref[...]).astype(BF16)
    cos = cos_ref[...]
    s1 = s1_ref[...]
    s2 = s2_ref[...]

    def head_norm_rope(x, gain):
        ms = jnp.sum(x * x, axis=-1, keepdims=True) * (1.0 / MLA_QK)
        x = x * lax.rsqrt(ms + RMS_EPS) * gain
        half = MLA_ROPE // 2
        return x * cos + pltpu.roll(x, LANES - half, 1) * s1 + pltpu.roll(x, half, 1) * s2

    for h in range(MLA_HEADS):
        sl = slice(h * LANES, (h + 1) * LANES)
        q_out[:, sl] = (head_norm_rope(q[:, sl], qhn_ref[...]) * (MLA_QK ** -0.5)).astype(BF16)
        k_out[:, sl] = head_norm_rope(kn[:, sl] + kr, khn_ref[...]).astype(BF16)


def _mla_proj(h, nw, w_in, q_norm, w_uq, kv_norm, w_uk, w_uv, qhn, khn, cos, s1, s2):
    b, tp, d = h.shape
    tt = _pick(tp, (704, 352, 192, 64))
    nh = MLA_HEADS
    full = lambda a: pl.BlockSpec(a.shape, lambda i, j: (0,) * a.ndim)
    tab = pl.BlockSpec((tt, LANES), lambda i, j: (j, 0))
    row = lambda n: pl.BlockSpec((None, tt, n), lambda i, j: (i, j, 0))
    return pl.pallas_call(
        _mla_proj_kernel,
        grid=(b, tp // tt),
        in_specs=[row(d), full(nw), full(w_in), full(q_norm), full(w_uq), full(kv_norm),
                  full(w_uk), full(w_uv), full(qhn), full(khn), tab, tab, tab],
        out_specs=[row(nh * LANES), row(nh * LANES), row(nh * MLA_V)],
        out_shape=[
            jax.ShapeDtypeStruct((b, tp, nh * LANES), BF16),
            jax.ShapeDtypeStruct((b, tp, nh * LANES), BF16),
            jax.ShapeDtypeStruct((b, tp, nh * MLA_V), BF16),
        ],
        compiler_params=_params("parallel", "arbitrary"),
        name="mla_proj",
    )(h, nw, w_in, q_norm, w_uq, kv_norm, w_uk, w_uv, qhn, khn, cos, s1, s2)


def _mla_attn_kernel(q_ref, k_ref, v_ref, o_ref, *, tq, nq):
    ri = lax.broadcasted_iota(jnp.int32, (tq, tq), 0)
    ci = lax.broadcasted_iota(jnp.int32, (tq, tq), 1)
    tri = jnp.where(ci <= ri, 0.0, NEG_BIG)
    tri0 = jnp.where(ci >= ROW0, tri, NEG_BIG)
    pad_row = jnp.where(lax.broadcasted_iota(jnp.int32, (1, max(nq - 1, 1) * tq), 1) >= ROW0, 0.0, NEG_BIG)
    lane = lax.broadcasted_iota(jnp.int32, (tq, LANES), 1)
    units = [(qi, r) for qi in range(nq) for r in range(2)]

    def scores(qi, r):
        lo = qi * tq
        hs = slice(r * LANES, (r + 1) * LANES)
        q = q_ref[lo:lo + tq, hs]
        s_diag = _dot_nt(q, k_ref[lo:lo + tq, hs]) + (tri if qi else tri0)
        s_main = _dot_nt(q, k_ref[0:lo, hs]) + pad_row[:, :lo] if qi else None
        return s_main, s_diag

    def attend(qi, s_main, s_diag):
        lo = qi * tq
        m = jnp.max(s_diag, axis=-1, keepdims=True)
        if qi:
            m = jnp.maximum(m, jnp.max(s_main, axis=-1, keepdims=True))
        p = jnp.exp(s_diag - m)
        l = jnp.sum(p, axis=-1, keepdims=True)
        o = _dot(p.astype(BF16), v_ref[lo:lo + tq, :])
        if qi:
            p = jnp.exp(s_main - m)
            l = l + jnp.sum(p, axis=-1, keepdims=True)
            o = o + _dot(p.astype(BF16), v_ref[0:lo, :])
        return o * (1.0 / l)

    nxt = scores(*units[0])
    outs = []
    for n, (qi, r) in enumerate(units):
        cur = nxt
        if n + 1 < len(units):
            nxt = scores(*units[n + 1])
        outs.append(attend(qi, *cur))
        if r == 1:
            o = jnp.where(lane < MLA_V, outs[0], outs[1])
            outs = []
            if qi == 0:
                o = jnp.where(lax.broadcasted_iota(jnp.int32, (tq, LANES), 0) >= ROW0, o, 0.0)
            o_ref[qi * tq:(qi + 1) * tq, :] = o.astype(BF16)


def _mla_attn(q, k, v):
    b, tp, _ = q.shape
    tq = _pick(tp, (192, 64))
    npair = MLA_HEADS // 2
    kern = functools.partial(_mla_attn_kernel, tq=tq, nq=tp // tq)
    return pl.pallas_call(
        kern,
        grid=(b, npair),
        in_specs=[
            pl.BlockSpec((None, tp, 2 * LANES), lambda i, j: (i, 0, j)),
            pl.BlockSpec((None, tp, 2 * LANES), lambda i, j: (i, 0, j)),
            pl.BlockSpec((None, tp, LANES), lambda i, j: (i, 0, j)),
        ],
        out_specs=pl.BlockSpec((None, tp, LANES), lambda i, j: (i, 0, j)),
        out_shape=jax.ShapeDtypeStruct((b, tp, MLA_HEADS * MLA_V), BF16),
        compiler_params=_params("parallel", "arbitrary"),
        name="mla_attn",
    )(q, k, v)


def _rope_tables(tp):
    half = MLA_ROPE // 2
    pos = jnp.arange(tp, dtype=F32) - float(ROW0)
    inv_freq = ROPE_THETA ** (-jnp.arange(0, MLA_ROPE, 2, dtype=F32) / MLA_ROPE)
    ang = pos[:, None] * inv_freq[None, :]
    cos, sin = jnp.cos(ang), jnp.sin(ang)
    ones = jnp.ones((tp, MLA_NOPE), F32)
    z_nope = jnp.zeros((tp, MLA_NOPE), F32)
    z_half = jnp.zeros((tp, half), F32)
    z_tail = jnp.zeros((tp, LANES - MLA_QK), F32)
    cos_t = jnp.concatenate([ones, cos, cos, z_tail], axis=1)
    s1 = jnp.concatenate([z_nope, -sin, z_half, z_tail], axis=1)
    s2 = jnp.concatenate([z_nope, z_half, sin, z_tail], axis=1)
    return cos_t, s1, s2


def _mla_layer(h, nw, w_in, q_norm, w_uq, kv_norm, w_ukv, q_head_norm, k_head_norm, w_out):
    b, tp, d = h.shape
    m = b * tp
    nh = MLA_HEADS
    lat = MLA_Q_RANK + MLA_KV_RANK
    w_in_p = jnp.concatenate([
        w_in[:, :lat],
        jnp.zeros((d, MLA_NOPE), F32), w_in[:, lat:], jnp.zeros((d, LANES - MLA_QK), F32),
    ], axis=1).astype(BF16)
    w_uq_p = _pad_heads(w_uq, nh, MLA_QK).astype(BF16)
    w_ukv3 = w_ukv.reshape(MLA_KV_RANK, nh, MLA_NOPE + MLA_V)
    w_uk_p = _pad_heads(w_ukv3[:, :, :MLA_NOPE].reshape(MLA_KV_RANK, nh * MLA_NOPE), nh, MLA_NOPE).astype(BF16)
    w_uv = w_ukv3[:, :, MLA_NOPE:].reshape(MLA_KV_RANK, nh * MLA_V).astype(BF16)
    pad_gain = lambda g: jnp.pad(g, (0, LANES - MLA_QK)).reshape(1, LANES)
    cos, s1, s2 = _rope_tables(tp)
    q, k, v = _mla_proj(h, nw.reshape(1, d), w_in_p, q_norm.reshape(1, -1), w_uq_p,
                        kv_norm.reshape(1, -1), w_uk_p, w_uv,
                        pad_gain(q_head_norm), pad_gain(k_head_norm), cos, s1, s2)
    o = _mla_attn(q, k, v)
    return _proj_res(o.reshape(m, nh * MLA_V), w_out.astype(BF16), h.reshape(m, d)).reshape(b, tp, d)


def kernel(x, meta_tokens, attn_norm, ffn_norm, ff_up, ff_down, gdn_in, gdn_conv, gdn_a_log, gdn_dt_bias, gdn_norm, gdn_out, mlstm_in, mlstm_gate_bias, mlstm_norm, mlstm_out, mla_in, mla_q_norm, mla_uq, mla_kv_norm, mla_ukv, mla_q_head_norm, mla_k_head_norm, mla_out):
    b, t, d = x.shape
    depth = attn_norm.shape[0]
    meta = jnp.broadcast_to(meta_tokens[None].astype(x.dtype), (b, N_META, d))
    h = jnp.concatenate([jnp.zeros((b, LEAD_PAD, d), x.dtype), meta, x], axis=1)
    tp = h.shape[1]
    for layer in range(depth):
        kind, j = layer % N_MIXERS, layer // N_MIXERS
        if kind == 0:
            h = _gdn_layer(h, attn_norm[layer], gdn_in[j], gdn_conv[j], gdn_a_log[j],
                           gdn_dt_bias[j], gdn_norm[j], gdn_out[j])
        elif kind == 1:
            h = _mlstm_layer(h, attn_norm[layer], mlstm_in[j], mlstm_gate_bias[j],
                             mlstm_norm[j], mlstm_out[j])
        else:
            h = _mla_layer(h, attn_norm[layer], mla_in[j], mla_q_norm[j], mla_uq[j],
                           mla_kv_norm[j], mla_ukv[j], mla_q_head_norm[j],
                           mla_k_head_norm[j], mla_out[j])
        h = _mlp(h.reshape(b * tp, d), ffn_norm[layer].reshape(1, d),
                 ff_up[layer].astype(BF16), ff_down[layer].astype(BF16)).reshape(b, tp, d)
    return h[:, LEAD_PAD + N_META:]
```

```python
import functools

import jax
import jax.numpy as jnp
import numpy as np
from jax import lax
from jax.experimental import pallas as pl
from jax.experimental.pallas import tpu as pltpu

F32 = jnp.float32
BF16 = jnp.bfloat16

N_META = 16
CHUNK = 64
LEAD_PAD = (-N_META) % CHUNK
ROW0 = LEAD_PAD
RMS_EPS = 1e-6
N_MIXERS = 3

GDN_QK_HEADS = 8
GDN_V_HEADS = 16
GDN_HEAD_DIM = 128
GDN_CONV = 4
PREP_UNROLL = 4
PREP_HALO = 16
GDN_HEADS_PER_STEP = 8
GDN_CHUNKS_PER_STEP = 3

MLSTM_HEADS = 8
MLSTM_DQK = 64
MLSTM_DV = 128
GATE_SOFTCAP = 15.0
MLSTM_CHUNKS_PER_STEP = 2

MLA_HEADS = 16
MLA_NOPE = 64
MLA_ROPE = 32
MLA_QK = MLA_NOPE + MLA_ROPE
MLA_V = 64
MLA_Q_RANK = 384
MLA_KV_RANK = 256
ROPE_THETA = 10000.0

LANES = 128
NEG_BIG = -1e30
VMEM_LIMIT = 56 * 1024 * 1024


def _pick(n, candidates):
    for c in candidates:
        if n % c == 0:
            return c
    raise ValueError(f"no tile for {n} in {candidates}")


def _params(*sem):
    return pltpu.CompilerParams(dimension_semantics=sem, vmem_limit_bytes=VMEM_LIMIT)


def _norm_rows(x, w):
    ms = jnp.mean(x * x, axis=-1, keepdims=True)
    return x * lax.rsqrt(ms + RMS_EPS) * w


def _softplus(x):
    return jnp.maximum(x, 0.0) + jnp.log1p(jnp.exp(-jnp.abs(x)))


def _dot(a, b):
    return jnp.dot(a, b, preferred_element_type=F32)


def _dot_nt(a, b):
    return lax.dot_general(a, b, (((1,), (1,)), ((), ())), preferred_element_type=F32)


def _dot_tn(a, b):
    return lax.dot_general(a, b, (((0,), (0,)), ((), ())), preferred_element_type=F32)


def _norm_matmul_kernel(x_ref, nw_ref, w_ref, wg_ref, o_ref, g_ref, xn_ref):
    @pl.when(pl.program_id(1) == 0)
    def _():
        xn = _norm_rows(x_ref[...], nw_ref[...]).astype(BF16)
        xn_ref[...] = xn
        g_ref[...] = _dot(xn, wg_ref[...])

    o_ref[...] = _dot(xn_ref[...], w_ref[...]).astype(o_ref.dtype)


def _norm_matmul(x, nw, w, wg):
    m, d = x.shape
    n = w.shape[1]
    tm = _pick(m, (1024, 768, 512, 384, 256, 192, 128, 64))
    tn = _pick(n, (2048, 1024, 512, 256, 128))
    return pl.pallas_call(
        _norm_matmul_kernel,
        grid=(m // tm, n // tn),
        in_specs=[
            pl.BlockSpec((tm, d), lambda i, j: (i, 0)),
            pl.BlockSpec((1, d), lambda i, j: (0, 0)),
            pl.BlockSpec((d, tn), lambda i, j: (0, j)),
            pl.BlockSpec((d, LANES), lambda i, j: (0, 0)),
        ],
        out_specs=[
            pl.BlockSpec((tm, tn), lambda i, j: (i, j)),
            pl.BlockSpec((tm, LANES), lambda i, j: (i, 0)),
        ],
        out_shape=[
            jax.ShapeDtypeStruct((m, n), BF16),
            jax.ShapeDtypeStruct((m, LANES), F32),
        ],
        scratch_shapes=[pltpu.VMEM((tm, d), BF16)],
        compiler_params=_params("parallel", "arbitrary"),
        name="norm_matmul",
    )(x, nw, w, wg)


def _proj_res_kernel(y_ref, w_ref, h_ref, o_ref):
    o_ref[...] = h_ref[...] + _dot(y_ref[...], w_ref[...])


def _proj_res(y, w, h):
    m, k = y.shape
    d = w.shape[1]
    tm = _pick(m, (1024, 768, 512, 384, 256, 192, 128, 64))
    return pl.pallas_call(
        _proj_res_kernel,
        grid=(m // tm,),
        in_specs=[
            pl.BlockSpec((tm, k), lambda i: (i, 0)),
            pl.BlockSpec((k, d), lambda i: (0, 0)),
            pl.BlockSpec((tm, d), lambda i: (i, 0)),
        ],
        out_specs=pl.BlockSpec((tm, d), lambda i: (i, 0)),
        out_shape=jax.ShapeDtypeStruct((m, d), F32),
        compiler_params=_params("parallel"),
        name="proj_res",
    )(y, w, h)


def _mlp_kernel(h_ref, nw_ref, wu_ref, wd_ref, o_ref, xn_ref):
    j = pl.program_id(1)

    @pl.when(j == 0)
    def _():
        h = h_ref[...]
        xn_ref[...] = _norm_rows(h, nw_ref[...]).astype(BF16)
        o_ref[...] = h

    a = jnp.maximum(_dot(xn_ref[...], wu_ref[...]), 0.0)
    o_ref[...] += _dot((a * a).astype(BF16), wd_ref[...])


def _mlp(h, nw, wu, wd):
    m, d = h.shape
    f = wu.shape[1]
    tm = _pick(m, (1024, 768, 512, 384, 256, 192, 128, 64))
    tf = _pick(f, (1024, 512, 256, 128))
    return pl.pallas_call(
        _mlp_kernel,
        grid=(m // tm, f // tf),
        in_specs=[
            pl.BlockSpec((tm, d), lambda i, j: (i, 0)),
            pl.BlockSpec((1, d), lambda i, j: (0, 0)),
            pl.BlockSpec((d, tf), lambda i, j: (0, j)),
            pl.BlockSpec((tf, d), lambda i, j: (j, 0)),
        ],
        out_specs=pl.BlockSpec((tm, d), lambda i, j: (i, 0)),
        out_shape=jax.ShapeDtypeStruct((m, d), F32),
        scratch_shapes=[pltpu.VMEM((tm, d), BF16)],
        compiler_params=_params("parallel", "arbitrary"),
        name="mlp",
    )(h, nw, wu, wd)


def _gdn_prep_kernel(x_ref, w_ref, o_ref, *, n_q_tiles, n_qk_tiles, tc, nc):
    c = pl.program_id(1)
    w = w_ref[...]
    taps = [w[GDN_CONV - 1 - s:GDN_CONV - s, :] for s in range(GDN_CONV)]
    rows = lax.broadcasted_iota(jnp.int32, (CHUNK, tc), 0)

    def conv_silu(xb, lead):
        y = xb[lead:lead + CHUNK] * taps[0]
        for s in range(1, GDN_CONV):
            y = y + xb[lead - s:lead - s + CHUNK] * taps[s]
        half_y = 0.5 * y
        return half_y + half_y * jnp.tanh(half_y)

    def emit(r0, y, normed, scale):
        if not normed:
            o_ref[pl.ds(r0, CHUNK), :] = y.astype(BF16)
            return
        for i in range(tc // GDN_HEAD_DIM):
            yi = y[:, i * GDN_HEAD_DIM:(i + 1) * GDN_HEAD_DIM]
            ss = jnp.sum(yi * yi, axis=-1, keepdims=True)
            o_ref[pl.ds(r0, CHUNK), i * GDN_HEAD_DIM:(i + 1) * GDN_HEAD_DIM] = (
                yi * (lax.rsqrt(ss + RMS_EPS) * scale)).astype(BF16)

    def run(normed, scale):
        x0 = jnp.concatenate([jnp.zeros((PREP_HALO, tc), F32), x_ref[0:CHUNK, :].astype(F32)], axis=0)
        emit(0, jnp.where(rows >= ROW0, conv_silu(x0, PREP_HALO), 0.0), normed, scale)

        def body(ci, carry):
            r0 = pl.multiple_of(ci * CHUNK, CHUNK)
            start = pl.multiple_of(ci * CHUNK - PREP_HALO, PREP_HALO)
            xb = x_ref[pl.ds(start, CHUNK + PREP_HALO), :].astype(F32)
            emit(r0, conv_silu(xb, PREP_HALO), normed, scale)
            return carry

        lax.fori_loop(1, nc, body, 0, unroll=PREP_UNROLL if (nc - 1) % PREP_UNROLL == 0 else 1)

    @pl.when(c < n_qk_tiles)
    def _():
        run(True, jnp.where(c < n_q_tiles, GDN_HEAD_DIM ** -0.5, 1.0).astype(F32))

    @pl.when(c >= n_qk_tiles)
    def _():
        run(False, None)


def _gdn_prep(proj, w_conv):
    b, tp, _ = proj.shape
    cdim = w_conv.shape[1]
    tc = 512
    qk = GDN_QK_HEADS * GDN_HEAD_DIM
    kern = functools.partial(_gdn_prep_kernel, n_q_tiles=qk // tc, n_qk_tiles=2 * qk // tc, tc=tc,
                             nc=tp // CHUNK)
    return pl.pallas_call(
        kern,
        grid=(b, cdim // tc),
        in_specs=[
            pl.BlockSpec((None, tp, tc), lambda i, c: (i, 0, c)),
            pl.BlockSpec((GDN_CONV, tc), lambda i, c: (0, c)),
        ],
        out_specs=pl.BlockSpec((None, tp, tc), lambda i, c: (i, 0, c)),
        out_shape=jax.ShapeDtypeStruct((b, tp, cdim), BF16),
        compiler_params=_params("parallel", "arbitrary"),
        name="gdn_prep",
    )(proj, w_conv)


def _tri_masks():
    ii = lax.broadcasted_iota(jnp.int32, (CHUNK, CHUNK), 0)
    jj = lax.broadcasted_iota(jnp.int32, (CHUNK, CHUNK), 1)
    return ii, jj


def _gdn_gates_kernel(g_ref, gt_ref, alc_ref, dtc_ref, alr_ref, dtr_ref,
                      beta_ref, cumc_ref, cumr_ref, *, nc, hg):
    nh = GDN_V_HEADS
    ii, jj = _tri_masks()
    tril = (ii >= jj).astype(F32)
    triu = (ii <= jj).astype(F32)
    neg_a_c = -jnp.exp(alc_ref[...])
    neg_a_r = -jnp.exp(alr_ref[...])
    for c in range(nc):
        blk = g_ref[c * CHUNK:(c + 1) * CHUNK, :]
        beta = jax.nn.sigmoid(blk[:, 0:nh])
        g = neg_a_c * _softplus(blk[:, nh:2 * nh] + dtc_ref[...])
        gr = neg_a_r * _softplus(gt_ref[:, c * CHUNK:(c + 1) * CHUNK] + dtr_ref[...])
        if c == 0:
            rows = lax.broadcasted_iota(jnp.int32, (CHUNK, nh), 0)
            cols = lax.broadcasted_iota(jnp.int32, (nh, CHUNK), 1)
            beta = jnp.where(rows >= ROW0, beta, 0.0)
            g = jnp.where(rows >= ROW0, g, 0.0)
            gr = jnp.where(cols >= ROW0, gr, 0.0)
        cum = jnp.dot(tril, g, preferred_element_type=F32, precision=lax.Precision.HIGHEST)
        cumr = jnp.dot(gr, triu, preferred_element_type=F32, precision=lax.Precision.HIGHEST)
        cumr = jnp.concatenate([cumr[:nh // 2], cumr[nh // 2:]], axis=1)
        for q in range(nh // hg):
            beta_ref[q, c] = beta[:, q * hg:(q + 1) * hg]
            cumc_ref[q, c] = cum[:, q * hg:(q + 1) * hg]
            cumr_ref[q, c] = cumr[q * hg // 2:(q + 1) * hg // 2, :]


def _gdn_gates(gates, a_log, dt_bias, hg):
    b, tp, _ = gates.shape
    nc = tp // CHUNK
    nh = GDN_V_HEADS
    ng = nh // hg
    kern = functools.partial(_gdn_gates_kernel, nc=nc, hg=hg)
    perm = np.concatenate([np.arange(0, nh, 2), np.arange(1, nh, 2)])
    gates_t = jnp.swapaxes(gates[:, :, nh:2 * nh], 1, 2)[:, perm, :]
    col = jax.ShapeDtypeStruct((b, ng, nc, CHUNK, hg), F32)
    row = jax.ShapeDtypeStruct((b, ng, nc, hg // 2, 2 * CHUNK), F32)
    col_spec = pl.BlockSpec((None, ng, nc, CHUNK, hg), lambda i: (i, 0, 0, 0, 0))
    row_spec = pl.BlockSpec((None, ng, nc, hg // 2, 2 * CHUNK), lambda i: (i, 0, 0, 0, 0))
    small = lambda shape: pl.BlockSpec(shape, lambda i: (0, 0))
    return pl.pallas_call(
        kern,
        grid=(b,),
        in_specs=[
            pl.BlockSpec((None, tp, LANES), lambda i: (i, 0, 0)),
            pl.BlockSpec((None, nh, tp), lambda i: (i, 0, 0)),
            small((1, nh)), small((1, nh)), small((nh, 1)), small((nh, 1)),
        ],
        out_specs=[col_spec, col_spec, row_spec],
        out_shape=[col, col, row],
        compiler_params=_params("parallel"),
        name="gdn_gates",
    )(gates, gates_t, a_log.reshape(1, nh), dt_bias.reshape(1, nh),
      a_log[perm].reshape(nh, 1), dt_bias[perm].reshape(nh, 1))


def _gdn_chunk_kernel(q_ref, k_ref, v_ref, z_ref, beta_ref, cumc_ref, cumr_ref, wn_ref, y_ref,
                      s_ref, u_ref, wq_ref, kd_ref, at_ref, *, nc, hg, cps):
    hd = GDN_HEAD_DIM
    iw = lax.broadcasted_iota(jnp.int32, (CHUNK, hd), 0)
    lane = lax.broadcasted_iota(jnp.int32, (CHUNK, hd), 1)
    jw = lane & (CHUNK - 1)
    first = lane < CHUNK
    causal = iw >= jw
    strict = iw > jw
    eye = (iw == jw).astype(F32)
    blk = [(iw >> l) == (jw >> l) for l in range(1, CHUNK.bit_length())]
    wn = wn_ref[...]
    heads = range(hg)
    pairs = range(hg // 2)

    def bdiag(x):
        zero = jnp.zeros_like(x)
        return jnp.concatenate([jnp.where(first, x, zero), jnp.where(first, zero, x)], axis=0)

    rstack = bdiag

    def prepare(chunks):
        units = [(i, p) for i in range(len(chunks)) for p in pairs]
        r0 = [pl.multiple_of(c * CHUNK, CHUNK) for c in chunks]
        cumc = [cumc_ref[c] for c in chunks]
        betac = [beta_ref[c] for c in chunks]
        cumr = [cumr_ref[c] for c in chunks]
        q = {(i, p): q_ref[pl.ds(r0[i], CHUNK), p * hd:(p + 1) * hd] for i, p in units}
        k = {(i, p): k_ref[pl.ds(r0[i], CHUNK), p * hd:(p + 1) * hd] for i, p in units}
        qkk = {u: _dot_nt(jnp.concatenate([q[u], k[u]], axis=0),
                          jnp.concatenate([k[u], k[u]], axis=0)) for u in units}
        cc = {(i, h): jnp.broadcast_to(cumc[i][:, h:h + 1], (CHUNK, hd))
              for i in range(len(chunks)) for h in heads}
        bc = {(i, h): jnp.broadcast_to(betac[i][:, h:h + 1], (CHUNK, hd))
              for i in range(len(chunks)) for h in heads}
        ccw = {(i, p): jnp.where(first, cc[i, 2 * p], cc[i, 2 * p + 1]) for i, p in units}
        bcw = {(i, p): jnp.where(first, bc[i, 2 * p], bc[i, 2 * p + 1]) for i, p in units}
        decay = {(i, p): jnp.exp(jnp.where(causal, ccw[i, p] - cumr[i][p:p + 1, :], -jnp.inf))
                 for i, p in units}
        a = {u: jnp.where(strict, bcw[u] * qkk[u][CHUNK:] * decay[u], 0.0) for u in units}
        for i, p in units:
            at_ref[i, p] = rstack((qkk[i, p][:CHUNK] * decay[i, p]).astype(BF16))
        t = {u: eye - jnp.where(blk[0], a[u], 0.0) for u in units}
        for lvl in range(1, len(blk)):
            a_off = {u: bdiag(jnp.where(blk[lvl], jnp.where(blk[lvl - 1], 0.0, a[u]), 0.0)
                              .astype(BF16)) for u in units}
            tb = {u: t[u].astype(BF16) for u in units}
            x = {u: _dot(tb[u], a_off[u]).astype(BF16) for u in units}
            t = {u: t[u] - _dot(x[u], bdiag(tb[u])) for u in units}
        ec = {u: jnp.exp(cc[u]) for u in cc}
        rhs = {}
        for i, p in units:
            kf = k[i, p].astype(F32)
            halves = []
            for h in (2 * p, 2 * p + 1):
                v = v_ref[pl.ds(r0[i], CHUNK), h * hd:(h + 1) * hd].astype(F32)
                halves.append(jnp.concatenate([(v * bc[i, h]).astype(BF16),
                                               (kf * (bc[i, h] * ec[i, h])).astype(BF16)], axis=1))
            rhs[i, p] = jnp.concatenate(halves, axis=0)
        sol = {u: _dot(rstack(t[u].astype(BF16)), rhs[u]) for u in units}
        for i, p in units:
            qf = q[i, p].astype(F32)
            kf = k[i, p].astype(F32)
            for r in range(2):
                h = 2 * p + r
                sh = sol[i, p][r * CHUNK:(r + 1) * CHUNK]
                u_ref[i, h] = sh[:, :hd]
                wq_ref[i, h] = jnp.concatenate(
                    [sh[:, hd:].astype(BF16), (qf * ec[i, h]).astype(BF16)], axis=0)
                kd_ref[i, h] = (kf * jnp.exp(cc[i, h][CHUNK - 1:CHUNK, :] - cc[i, h])).astype(BF16)

    def recur(c, slot, s):
        r0 = pl.multiple_of(c * CHUNK, CHUNK)
        cumc = cumc_ref[c]
        ws = [_dot(wq_ref[slot, h], s[h].astype(BF16)) for h in heads]
        v_new = [(u_ref[slot, h] - ws[h][:CHUNK]).astype(BF16) for h in heads]
        upd = [_dot_tn(kd_ref[slot, h], v_new[h]) for h in heads]
        av = [_dot(at_ref[slot, p], jnp.concatenate([v_new[2 * p], v_new[2 * p + 1]], axis=0))
              for p in pairs]
        s_new = [s[h] * jnp.exp(cumc[CHUNK - 1:CHUNK, h:h + 1]) + upd[h] for h in heads]
        for h in heads:
            o = ws[h][CHUNK:] + av[h // 2][(h % 2) * CHUNK:(h % 2 + 1) * CHUNK]
            z = z_ref[pl.ds(r0, CHUNK), h * hd:(h + 1) * hd].astype(F32)
            y = _norm_rows(o, wn) * (z * jax.nn.sigmoid(z))
            y_ref[pl.ds(r0, CHUNK), h * hd:(h + 1) * hd] = y.astype(BF16)
        return s_new

    s_ref[...] = jnp.zeros_like(s_ref)
    prepare([jnp.int32(i) for i in range(min(cps, nc))])

    def body(it, carry):
        c0 = it * cps
        s = [s_ref[h] for h in heads]
        for i in range(cps):
            s = recur(c0 + i, i, s)
        for h in heads:
            s_ref[h] = s[h]
        prepare([jnp.minimum(c0 + cps + i, nc - 1) for i in range(cps)])
        return carry

    lax.fori_loop(0, nc // cps, body, 0)
    s = [s_ref[h] for h in heads]
    for i in range(nc % cps):
        s = recur(jnp.int32(nc - nc % cps + i), i, s)


def _gdn_chunk(qkv, proj, beta, cumc, cumr, w_norm, hg):
    b, tp, _ = qkv.shape
    nc = tp // CHUNK
    hd = GDN_HEAD_DIM
    ng = GDN_V_HEADS // hg
    wqk = hd * hg // 2
    wv = hd * hg
    qk_dim = GDN_QK_HEADS * hd
    v_dim = GDN_V_HEADS * hd
    cps = GDN_CHUNKS_PER_STEP
    kern = functools.partial(_gdn_chunk_kernel, nc=nc, hg=hg, cps=cps)
    col_spec = pl.BlockSpec((None, None, nc, CHUNK, hg), lambda i, j: (i, j, 0, 0, 0))
    row_spec = pl.BlockSpec((None, None, nc, hg // 2, 2 * CHUNK), lambda i, j: (i, j, 0, 0, 0))
    return pl.pallas_call(
        kern,
        grid=(b, ng),
        in_specs=[
            pl.BlockSpec((None, tp, wqk), lambda i, j: (i, 0, j)),
            pl.BlockSpec((None, tp, wqk), lambda i, j: (i, 0, qk_dim // wqk + j)),
            pl.BlockSpec((None, tp, wv), lambda i, j: (i, 0, 2 * qk_dim // wv + j)),
            pl.BlockSpec((None, tp, wv), lambda i, j: (i, 0, (2 * qk_dim + v_dim) // wv + j)),
            col_spec, col_spec, row_spec,
            pl.BlockSpec((1, hd), lambda i, j: (0, 0)),
        ],
        out_specs=pl.BlockSpec((None, tp, wv), lambda i, j: (i, 0, j)),
        out_shape=jax.ShapeDtypeStruct((b, tp, v_dim), BF16),
        scratch_shapes=[
            pltpu.VMEM((hg, hd, hd), F32),
            pltpu.VMEM((cps, hg, CHUNK, hd), F32),
            pltpu.VMEM((cps, hg, 2 * CHUNK, hd), BF16),
            pltpu.VMEM((cps, hg, CHUNK, hd), BF16),
            pltpu.VMEM((cps, hg // 2, 2 * CHUNK, hd), BF16),
        ],
        compiler_params=_params("parallel", "arbitrary"),
        name="gdn_chunk",
    )(qkv, qkv, qkv, proj, beta, cumc, cumr, w_norm.reshape(1, hd))


def _gdn_layer(h, nw, w_in, w_conv, a_log, dt_bias, w_norm, w_out):
    b, tp, d = h.shape
    m = b * tp
    conv_dim = 2 * GDN_QK_HEADS * GDN_HEAD_DIM + GDN_V_HEADS * GDN_HEAD_DIM
    main = conv_dim + GDN_V_HEADS * GDN_HEAD_DIM
    hg = GDN_HEADS_PER_STEP
    w_main = w_in[:, :main].astype(BF16)
    w_gate = jnp.pad(w_in[:, main:], ((0, 0), (0, LANES - 2 * GDN_V_HEADS))).astype(BF16)
    proj, gates = _norm_matmul(h.reshape(m, d), nw.reshape(1, d), w_main, w_gate)
    proj = proj.reshape(b, tp, main)
    gates = gates.reshape(b, tp, LANES)
    beta, cumc, cumr = _gdn_gates(gates, a_log, dt_bias, hg)
    qkv = _gdn_prep(proj, w_conv)
    y = _gdn_chunk(qkv, proj, beta, cumc, cumr, w_norm, hg)
    return _proj_res(y.reshape(m, -1), w_out.astype(BF16), h.reshape(m, d)).reshape(b, tp, d)


def _mlstm_gates_kernel(g_ref, gt_ref, bc_ref, br_ref, cumc_ref, rc_ref, rr_ref, *, nc, hg):
    nh = MLSTM_HEADS
    ii, jj = _tri_masks()
    tril = (ii >= jj).astype(F32)
    triu = (ii <= jj).astype(F32)

    def split(raw, axis):
        capped = GATE_SOFTCAP * jnp.tanh(raw / GATE_SOFTCAP)
        if axis == 1:
            i_pre, f_pre = capped[:, :nh], capped[:, nh:2 * nh]
        else:
            i_pre, f_pre = capped[:nh], capped[nh:2 * nh]
        return i_pre, -_softplus(-f_pre)

    for c in range(nc):
        i_c, lf_c = split(g_ref[c * CHUNK:(c + 1) * CHUNK, 0:2 * nh] + bc_ref[...], 1)
        i_r, lf_r = split(gt_ref[:, c * CHUNK:(c + 1) * CHUNK] + br_ref[...], 0)
        if c == 0:
            rows = lax.broadcasted_iota(jnp.int32, (CHUNK, nh), 0)
            cols = lax.broadcasted_iota(jnp.int32, (nh, CHUNK), 1)
            i_c = jnp.where(rows >= ROW0, i_c, -jnp.inf)
            lf_c = jnp.where(rows >= ROW0, lf_c, 0.0)
            i_r = jnp.where(cols >= ROW0, i_r, -jnp.inf)
            lf_r = jnp.where(cols >= ROW0, lf_r, 0.0)
        cum = jnp.dot(tril, lf_c, preferred_element_type=F32, precision=lax.Precision.HIGHEST)
        cumr = jnp.dot(lf_r, triu, preferred_element_type=F32, precision=lax.Precision.HIGHEST)
        rc = i_c - cum
        rr = i_r - cumr
        for q in range(nh // hg):
            cumc_ref[q, c] = cum[:, q * hg:(q + 1) * hg]
            rc_ref[q, c] = rc[:, q * hg:(q + 1) * hg]
            rr_ref[q, c] = rr[q * hg:(q + 1) * hg, :]


def _mlstm_gates(gates, gates_t, bias, hg):
    b, tp, _ = gates.shape
    nc = tp // CHUNK
    nh = MLSTM_HEADS
    ng = nh // hg
    kern = functools.partial(_mlstm_gates_kernel, nc=nc, hg=hg)
    col = jax.ShapeDtypeStruct((b, ng, nc, CHUNK, hg), F32)
    row = jax.ShapeDtypeStruct((b, ng, nc, hg, CHUNK), F32)
    col_spec = pl.BlockSpec((None, ng, nc, CHUNK, hg), lambda i: (i, 0, 0, 0, 0))
    row_spec = pl.BlockSpec((None, ng, nc, hg, CHUNK), lambda i: (i, 0, 0, 0, 0))
    return pl.pallas_call(
        kern,
        grid=(b,),
        in_specs=[
            pl.BlockSpec((None, tp, LANES), lambda i: (i, 0, 0)),
            pl.BlockSpec((None, 2 * nh, tp), lambda i: (i, 0, 0)),
            pl.BlockSpec((1, 2 * nh), lambda i: (0, 0)),
            pl.BlockSpec((2 * nh, 1), lambda i: (0, 0)),
        ],
        out_specs=[col_spec, col_spec, row_spec],
        out_shape=[col, col, row],
        compiler_params=_params("parallel"),
        name="mlstm_gates",
    )(gates, gates_t, bias.reshape(1, 2 * nh), bias.reshape(2 * nh, 1))


def _mlstm_chunk_kernel(q_ref, k_ref, v_ref, og_ref, cumc_ref, rc_ref, rr_ref, wn_ref, y_ref,
                        c_ref, m_ref, *, nc, hg, cps):
    dv = MLSTM_DV
    ii, jj = _tri_masks()
    causal = ii >= jj
    ones_col = jnp.ones((CHUNK, LANES), BF16)
    heads = range(hg)

    c_ref[...] = jnp.zeros_like(c_ref)
    m_ref[...] = jnp.zeros_like(m_ref)

    def step(chunks):
        n = len(chunks)
        units = [(i, h) for i in range(n) for h in heads]
        r0 = [pl.multiple_of(c * CHUNK, CHUNK) for c in chunks]
        cumc = [cumc_ref[c] for c in chunks]
        rcol = [rc_ref[c] for c in chunks]
        rrow = [rr_ref[c] for c in chunks]
        q = {(i, h): q_ref[pl.ds(r0[i], CHUNK), h * LANES:(h + 1) * LANES] for i, h in units}
        k = {(i, h): k_ref[pl.ds(r0[i], CHUNK), h * LANES:(h + 1) * LANES] for i, h in units}
        v_aug = {(i, h): jnp.concatenate(
            [v_ref[pl.ds(r0[i], CHUNK), h * dv:(h + 1) * dv], ones_col], axis=1) for i, h in units}
        qk = {u: _dot_nt(q[u], k[u]) for u in units}
        cc = {(i, h): jnp.broadcast_to(cumc[i][:, h:h + 1], (CHUNK, LANES)) for i, h in units}
        rc = {(i, h): jnp.broadcast_to(rcol[i][:, h:h + 1], (CHUNK, LANES)) for i, h in units}
        rr = {(i, h): rrow[i][h:h + 1, :] for i, h in units}
        rmax = {u: jnp.max(jnp.where(causal, rr[u], -jnp.inf), axis=-1, keepdims=True) for u in units}
        m_in, keep, kw = {}, {}, {}
        m = [m_ref[h][0:1, :] for h in heads]
        for i, h in units:
            m_in[i, h] = m[h]
            c_last = cc[i, h][CHUNK - 1:CHUNK, :]
            log_keep = c_last + m[h]
            m_new = jnp.maximum(log_keep, c_last + jnp.max(rc[i, h], axis=0, keepdims=True))
            keep[i, h] = jnp.exp(log_keep - m_new)
            w_end = jnp.exp(c_last + rc[i, h] - m_new)
            kw[i, h] = (k[i, h].astype(F32) * w_end).astype(BF16)
            m[h] = m_new
        for h in heads:
            m_ref[h] = jnp.broadcast_to(m[h], m_ref.shape[1:])
        upd = {u: _dot_tn(kw[u], v_aug[u]) for u in units}
        g, w_intra = {}, {}
        for u in units:
            g[u] = jnp.maximum(m_in[u], rmax[u])
            w_intra[u] = (jnp.exp(jnp.where(causal, rr[u] - g[u][:, :CHUNK], -jnp.inf))
                          * qk[u]).astype(BF16)
        wv = {u: _dot(w_intra[u], v_aug[u]) for u in units}
        state = [c_ref[h] for h in heads]
        for i in range(n):
            qc = [_dot(q[i, h], state[h].astype(BF16)) for h in heads]
            for h in heads:
                s_inter = jnp.exp(m_in[i, h] - g[i, h])
                tot = jnp.concatenate([s_inter, s_inter], axis=1) * qc[h] + wv[i, h]
                den = tot[:, dv:]
                inv = 1.0 / jnp.maximum(jnp.abs(den), jnp.exp(-(cc[i, h] + g[i, h])))
                hs = tot[:, :dv] * inv
                og = og_ref[pl.ds(r0[i], CHUNK), h * dv:(h + 1) * dv].astype(F32)
                y = _norm_rows(hs, wn_ref[:, h * dv:(h + 1) * dv]) * jax.nn.sigmoid(og)
                y_ref[pl.ds(r0[i], CHUNK), h * dv:(h + 1) * dv] = y.astype(BF16)
                state[h] = jnp.concatenate([keep[i, h], keep[i, h]], axis=1) * state[h] + upd[i, h]
        for h in heads:
            c_ref[h] = state[h]

    def body(it, carry):
        step([it * cps + i for i in range(cps)])
        return carry

    lax.fori_loop(0, nc // cps, body, 0)
    if nc % cps:
        step([jnp.int32(nc - nc % cps + i) for i in range(nc % cps)])


def _mlstm_chunk(proj, cumc, rc, rr, w_norm, hg):
    b, tp, _ = proj.shape
    nc = tp // CHUNK
    nh = MLSTM_HEADS
    ng = nh // hg
    wb = LANES * hg
    kern = functools.partial(_mlstm_chunk_kernel, nc=nc, hg=hg, cps=MLSTM_CHUNKS_PER_STEP)
    col_spec = pl.BlockSpec((None, None, nc, CHUNK, hg), lambda i, j: (i, j, 0, 0, 0))
    row_spec = pl.BlockSpec((None, None, nc, hg, CHUNK), lambda i, j: (i, j, 0, 0, 0))
    return pl.pallas_call(
        kern,
        grid=(b, ng),
        in_specs=[
            pl.BlockSpec((None, tp, wb), lambda i, j: (i, 0, j)),
            pl.BlockSpec((None, tp, wb), lambda i, j: (i, 0, ng + j)),
            pl.BlockSpec((None, tp, wb), lambda i, j: (i, 0, 2 * ng + j)),
            pl.BlockSpec((None, tp, wb), lambda i, j: (i, 0, 3 * ng + j)),
            col_spec, col_spec, row_spec,
            pl.BlockSpec((1, wb), lambda i, j: (0, j)),
        ],
        out_specs=pl.BlockSpec((None, tp, wb), lambda i, j: (i, 0, j)),
        out_shape=jax.ShapeDtypeStruct((b, tp, nh * MLSTM_DV), BF16),
        scratch_shapes=[
            pltpu.VMEM((hg, LANES, 2 * MLSTM_DV), F32),
            pltpu.VMEM((hg, 8, LANES), F32),
        ],
        compiler_params=_params("parallel", "arbitrary"),
        name="mlstm_chunk",
    )(proj, proj, proj, proj, cumc, rc, rr, w_norm.reshape(1, nh * MLSTM_DV))


def _pad_heads(w, nh, dh):
    d = w.shape[0]
    w = w.reshape(d, nh, dh)
    return jnp.pad(w, ((0, 0), (0, 0), (0, LANES - dh))).reshape(d, nh * LANES)


def _mlstm_layer(h, nw, w_in, gate_bias, w_norm, w_out):
    b, tp, d = h.shape
    m = b * tp
    nh = MLSTM_HEADS
    qk = nh * MLSTM_DQK
    vd = nh * MLSTM_DV
    hg = 4
    w_main = jnp.concatenate([
        _pad_heads(w_in[:, :qk], nh, MLSTM_DQK),
        _pad_heads(w_in[:, qk:2 * qk] * (MLSTM_DQK ** -0.5), nh, MLSTM_DQK),
        w_in[:, 2 * qk:2 * qk + 2 * vd],
    ], axis=1).astype(BF16)
    w_gate = jnp.pad(w_in[:, 2 * qk + 2 * vd:], ((0, 0), (0, LANES - 2 * nh))).astype(BF16)
    proj, gates = _norm_matmul(h.reshape(m, d), nw.reshape(1, d), w_main, w_gate)
    proj = proj.reshape(b, tp, -1)
    gates = gates.reshape(b, tp, LANES)
    gates_t = jnp.swapaxes(gates[:, :, :2 * nh], 1, 2)
    cumc, rc, rr = _mlstm_gates(gates, gates_t, gate_bias, hg)
    y = _mlstm_chunk(proj, cumc, rc, rr, w_norm, hg)
    return _proj_res(y.reshape(m, vd), w_out.astype(BF16), h.reshape(m, d)).reshape(b, tp, d)


def _mla_proj_kernel(h_ref, nw_ref, win_ref, qn_ref, wuq_ref, wuqs_ref, kvn_ref, wuk_ref, wuv_ref,
                     qtab_ref, ktab_ref, q_out, k_out, v_out):
    xn = _norm_rows(h_ref[...], nw_ref[...]).astype(BF16)
    c = _dot(xn, win_ref[...])
    cq = _norm_rows(c[:, :MLA_Q_RANK], qn_ref[...]).astype(BF16)
    lat = MLA_Q_RANK + MLA_KV_RANK
    ckv = _norm_rows(c[:, MLA_Q_RANK:lat], kvn_ref[...]).astype(BF16)
    kr = c[:, lat:lat + LANES]
    kr_swap = c[:, lat + LANES:]
    q = _dot(cq, wuq_ref[...])
    q_swap = _dot(cq, wuqs_ref[...])
    kn = _dot(ckv, wuk_ref[...])
    v_out[...] = _dot(ckv, wuv_ref[...]).astype(BF16)

    def head_norm_rope(x, x_swap, tab_ref):
        ms = jnp.sum(x * x, axis=-1, keepdims=True) * (1.0 / MLA_QK)
        return (x * tab_ref[0] + x_swap * tab_ref[1]) * lax.rsqrt(ms + RMS_EPS)

    for h in range(MLA_HEADS):
        sl = slice(h * LANES, (h + 1) * LANES)
        q_out[:, sl] = head_norm_rope(q[:, sl], q_swap[:, sl], qtab_ref).astype(BF16)
        k_out[:, sl] = head_norm_rope(kn[:, sl] + kr, kr_swap, ktab_ref).astype(BF16)


def _mla_proj(h, nw, w_in, q_norm, w_uq, w_uq_swap, kv_norm, w_uk, w_uv, qtab, ktab):
    b, tp, d = h.shape
    tt = _pick(tp, (704, 352, 192, 64))
    nh = MLA_HEADS
    full = lambda a: pl.BlockSpec(a.shape, lambda i, j: (0,) * a.ndim)
    tab = pl.BlockSpec((2, tt, LANES), lambda i, j: (0, j, 0))
    row = lambda n: pl.BlockSpec((None, tt, n), lambda i, j: (i, j, 0))
    return pl.pallas_call(
        _mla_proj_kernel,
        grid=(b, tp // tt),
        in_specs=[row(d), full(nw), full(w_in), full(q_norm), full(w_uq), full(w_uq_swap),
                  full(kv_norm), full(w_uk), full(w_uv), tab, tab],
        out_specs=[row(nh * LANES), row(nh * LANES), row(nh * MLA_V)],
        out_shape=[
            jax.ShapeDtypeStruct((b, tp, nh * LANES), BF16),
            jax.ShapeDtypeStruct((b, tp, nh * LANES), BF16),
            jax.ShapeDtypeStruct((b, tp, nh * MLA_V), BF16),
        ],
        compiler_params=_params("parallel", "arbitrary"),
        name="mla_proj",
    )(h, nw, w_in, q_norm, w_uq, w_uq_swap, kv_norm, w_uk, w_uv, qtab, ktab)


def _mla_attn_kernel(q_ref, k_ref, v_ref, o_ref, *, tq, nq):
    ri = lax.broadcasted_iota(jnp.int32, (tq, tq), 0)
    ci = lax.broadcasted_iota(jnp.int32, (tq, tq), 1)
    tri = jnp.where(ci <= ri, 0.0, NEG_BIG)
    tri0 = jnp.where(ci >= ROW0, tri, NEG_BIG)
    pad_row = jnp.where(lax.broadcasted_iota(jnp.int32, (1, max(nq - 1, 1) * tq), 1) >= ROW0, 0.0, NEG_BIG)
    lane = lax.broadcasted_iota(jnp.int32, (tq, LANES), 1)
    units = [(qi, r) for qi in range(nq) for r in range(2)]

    def scores(qi, r):
        lo = qi * tq
        hs = slice(r * LANES, (r + 1) * LANES)
        q = q_ref[lo:lo + tq, hs]
        s_diag = _dot_nt(q, k_ref[lo:lo + tq, hs]) + (tri if qi else tri0)
        s_main = _dot_nt(q, k_ref[0:lo, hs]) + pad_row[:, :lo] if qi else None
        return s_main, s_diag

    def attend(qi, s_main, s_diag):
        lo = qi * tq
        m = jnp.max(s_diag, axis=-1, keepdims=True)
        if qi:
            m = jnp.maximum(m, jnp.max(s_main, axis=-1, keepdims=True))
        p = jnp.exp(s_diag - m)
        l = jnp.sum(p, axis=-1, keepdims=True)
        o = _dot(p.astype(BF16), v_ref[lo:lo + tq, :])
        if qi:
            p = jnp.exp(s_main - m)
            l = l + jnp.sum(p, axis=-1, keepdims=True)
            o = o + _dot(p.astype(BF16), v_ref[0:lo, :])
        return o * (1.0 / l)

    nxt = scores(*units[0])
    outs = []
    for n, (qi, r) in enumerate(units):
        cur = nxt
        if n + 1 < len(units):
            nxt = scores(*units[n + 1])
        outs.append(attend(qi, *cur))
        if r == 1:
            o = jnp.where(lane < MLA_V, outs[0], outs[1])
            outs = []
            if qi == 0:
                o = jnp.where(lax.broadcasted_iota(jnp.int32, (tq, LANES), 0) >= ROW0, o, 0.0)
            o_ref[qi * tq:(qi + 1) * tq, :] = o.astype(BF16)


def _mla_attn(q, k, v):
    b, tp, _ = q.shape
    tq = _pick(tp, (192, 64))
    npair = MLA_HEADS // 2
    kern = functools.partial(_mla_attn_kernel, tq=tq, nq=tp // tq)
    return pl.pallas_call(
        kern,
        grid=(b, npair),
        in_specs=[
            pl.BlockSpec((None, tp, 2 * LANES), lambda i, j: (i, 0, j)),
            pl.BlockSpec((None, tp, 2 * LANES), lambda i, j: (i, 0, j)),
            pl.BlockSpec((None, tp, LANES), lambda i, j: (i, 0, j)),
        ],
        out_specs=pl.BlockSpec((None, tp, LANES), lambda i, j: (i, 0, j)),
        out_shape=jax.ShapeDtypeStruct((b, tp, MLA_HEADS * MLA_V), BF16),
        compiler_params=_params("parallel", "arbitrary"),
        name="mla_attn",
    )(q, k, v)


def _swap_rope_halves(a):
    half = MLA_ROPE // 2
    lo, hi = MLA_NOPE, MLA_NOPE + half
    return jnp.concatenate([a[..., :lo], a[..., hi:hi + half], a[..., lo:hi], a[..., hi + half:]], axis=-1)


def _rope_tables(tp, gain, scale):
    half = MLA_ROPE // 2
    pos = jnp.arange(tp, dtype=F32) - float(ROW0)
    inv_freq = ROPE_THETA ** (-jnp.arange(0, MLA_ROPE, 2, dtype=F32) / MLA_ROPE)
    ang = pos[:, None] * inv_freq[None, :]
    cos, sin = jnp.cos(ang), jnp.sin(ang)
    ones = jnp.ones((tp, MLA_NOPE), F32)
    z_nope = jnp.zeros((tp, MLA_NOPE), F32)
    z_tail = jnp.zeros((tp, LANES - MLA_QK), F32)
    cos_t = jnp.concatenate([ones, cos, cos, z_tail], axis=1)
    sin_t = jnp.concatenate([z_nope, -sin, sin, z_tail], axis=1)
    g = jnp.pad(gain, (0, LANES - MLA_QK)).reshape(1, LANES) * scale
    return jnp.stack([cos_t * g, sin_t * _swap_rope_halves(g)])


def _mla_layer(h, nw, w_in, q_norm, w_uq, kv_norm, w_ukv, q_head_norm, k_head_norm, w_out):
    b, tp, d = h.shape
    m = b * tp
    nh = MLA_HEADS
    lat = MLA_Q_RANK + MLA_KV_RANK
    w_kr = jnp.concatenate([jnp.zeros((d, MLA_NOPE), F32), w_in[:, lat:],
                            jnp.zeros((d, LANES - MLA_QK), F32)], axis=1)
    w_in_p = jnp.concatenate([w_in[:, :lat], w_kr, _swap_rope_halves(w_kr)], axis=1).astype(BF16)
    w_uq_p = _pad_heads(w_uq, nh, MLA_QK)
    w_uq_swap = _swap_rope_halves(w_uq_p.reshape(MLA_Q_RANK, nh, LANES)).reshape(MLA_Q_RANK, nh * LANES)
    w_uq_p, w_uq_swap = w_uq_p.astype(BF16), w_uq_swap.astype(BF16)
    w_ukv3 = w_ukv.reshape(MLA_KV_RANK, nh, MLA_NOPE + MLA_V)
    w_uk_p = _pad_heads(w_ukv3[:, :, :MLA_NOPE].reshape(MLA_KV_RANK, nh * MLA_NOPE), nh, MLA_NOPE).astype(BF16)
    w_uv = w_ukv3[:, :, MLA_NOPE:].reshape(MLA_KV_RANK, nh * MLA_V).astype(BF16)
    q, k, v = _mla_proj(h, nw.reshape(1, d), w_in_p, q_norm.reshape(1, -1), w_uq_p, w_uq_swap,
                        kv_norm.reshape(1, -1), w_uk_p, w_uv,
                        _rope_tables(tp, q_head_norm, MLA_QK ** -0.5),
                        _rope_tables(tp, k_head_norm, 1.0))
    o = _mla_attn(q, k, v)
    return _proj_res(o.reshape(m, nh * MLA_V), w_out.astype(BF16), h.reshape(m, d)).reshape(b, tp, d)


def kernel(x, meta_tokens, attn_norm, ffn_norm, ff_up, ff_down, gdn_in, gdn_conv, gdn_a_log, gdn_dt_bias, gdn_norm, gdn_out, mlstm_in, mlstm_gate_bias, mlstm_norm, mlstm_out, mla_in, mla_q_norm, mla_uq, mla_kv_norm, mla_ukv, mla_q_head_norm, mla_k_head_norm, mla_out):
    b, t, d = x.shape
    depth = attn_norm.shape[0]
    meta = jnp.broadcast_to(meta_tokens[None].astype(x.dtype), (b, N_META, d))
    h = jnp.concatenate([jnp.zeros((b, LEAD_PAD, d), x.dtype), meta, x], axis=1)
    tp = h.shape[1]
    for layer in range(depth):
        kind, j = layer % N_MIXERS, layer // N_MIXERS
        if kind == 0:
            h = _gdn_layer(h, attn_norm[layer], gdn_in[j], gdn_conv[j], gdn_a_log[j],
                           gdn_dt_bias[j], gdn_norm[j], gdn_out[j])
        elif kind == 1:
            h = _mlstm_layer(h, attn_norm[layer], mlstm_in[j], mlstm_gate_bias[j],
                             mlstm_norm[j], mlstm_out[j])
        else:
            h = _mla_layer(h, attn_norm[layer], mla_in[j], mla_q_norm[j], mla_uq[j],
                           mla_kv_norm[j], mla_ukv[j], mla_q_head_norm[j],
                           mla_k_head_norm[j], mla_out[j])
        h = _mlp(h.reshape(b * tp, d), ffn_norm[layer].reshape(1, d),
                 ff_up[layer].astype(BF16), ff_down[layer].astype(BF16)).reshape(b, tp, d)
    return h[:, LEAD_PAD + N_META:]
```

```python
import functools

import jax
import jax.numpy as jnp
import numpy as np
from jax import lax
from jax.experimental import pallas as pl
from jax.experimental.pallas import tpu as pltpu

F32 = jnp.float32
BF16 = jnp.bfloat16

N_META = 16
CHUNK = 64
LEAD_PAD = (-N_META) % CHUNK
ROW0 = LEAD_PAD
RMS_EPS = 1e-6
N_MIXERS = 3

GDN_QK_HEADS = 8
GDN_V_HEADS = 16
GDN_HEAD_DIM = 128
GDN_CONV = 4
PREP_UNROLL = 4
PREP_HALO = 16
GDN_HEADS_PER_STEP = 8
GDN_CHUNKS_PER_STEP = 3

MLSTM_HEADS = 8
MLSTM_DQK = 64
MLSTM_DV = 128
GATE_SOFTCAP = 15.0
MLSTM_CHUNKS_PER_STEP = 2

MLA_HEADS = 16
MLA_NOPE = 64
MLA_ROPE = 32
MLA_QK = MLA_NOPE + MLA_ROPE
MLA_V = 64
MLA_Q_RANK = 384
MLA_KV_RANK = 256
ROPE_THETA = 10000.0

LANES = 128
NEG_BIG = -1e30
VMEM_LIMIT = 56 * 1024 * 1024


def _pick(n, candidates):
    for c in candidates:
        if n % c == 0:
            return c
    raise ValueError(f"no tile for {n} in {candidates}")


def _params(*sem):
    return pltpu.CompilerParams(dimension_semantics=sem, vmem_limit_bytes=VMEM_LIMIT)


def _norm_rows(x, w):
    ms = jnp.mean(x * x, axis=-1, keepdims=True)
    return x * lax.rsqrt(ms + RMS_EPS) * w


def _softplus(x):
    return jnp.maximum(x, 0.0) + jnp.log1p(jnp.exp(-jnp.abs(x)))


def _dot(a, b):
    return jnp.dot(a, b, preferred_element_type=F32)


def _dot_nt(a, b):
    return lax.dot_general(a, b, (((1,), (1,)), ((), ())), preferred_element_type=F32)


def _dot_tn(a, b):
    return lax.dot_general(a, b, (((0,), (0,)), ((), ())), preferred_element_type=F32)


def _norm_matmul_kernel(x_ref, nw_ref, w_ref, wg_ref, o_ref, g_ref, xn_ref):
    @pl.when(pl.program_id(1) == 0)
    def _():
        xn = _norm_rows(x_ref[...], nw_ref[...]).astype(BF16)
        xn_ref[...] = xn
        g_ref[...] = _dot(xn, wg_ref[...])

    o_ref[...] = _dot(xn_ref[...], w_ref[...]).astype(o_ref.dtype)


def _norm_matmul(x, nw, w, wg):
    m, d = x.shape
    n = w.shape[1]
    tm = _pick(m, (1024, 768, 512, 384, 256, 192, 128, 64))
    tn = _pick(n, (3072, 2048, 1024, 512, 256, 128))
    return pl.pallas_call(
        _norm_matmul_kernel,
        grid=(m // tm, n // tn),
        in_specs=[
            pl.BlockSpec((tm, d), lambda i, j: (i, 0)),
            pl.BlockSpec((1, d), lambda i, j: (0, 0)),
            pl.BlockSpec((d, tn), lambda i, j: (0, j)),
            pl.BlockSpec((d, LANES), lambda i, j: (0, 0)),
        ],
        out_specs=[
            pl.BlockSpec((tm, tn), lambda i, j: (i, j)),
            pl.BlockSpec((tm, LANES), lambda i, j: (i, 0)),
        ],
        out_shape=[
            jax.ShapeDtypeStruct((m, n), BF16),
            jax.ShapeDtypeStruct((m, LANES), F32),
        ],
        scratch_shapes=[pltpu.VMEM((tm, d), BF16)],
        compiler_params=_params("parallel", "arbitrary"),
        name="norm_matmul",
    )(x, nw, w, wg)


def _out_mlp_kernel(h_ref, y_ref, wo_ref, nw_ref, wu_ref, wd_ref, o_ref, xn_ref):
    j = pl.program_id(1)

    @pl.when(j == 0)
    def _():
        h = h_ref[...] + _dot(y_ref[...], wo_ref[...])
        xn_ref[...] = _norm_rows(h, nw_ref[...]).astype(BF16)
        o_ref[...] = h

    a = jnp.maximum(_dot(xn_ref[...], wu_ref[...]), 0.0)
    o_ref[...] += _dot((a * a).astype(BF16), wd_ref[...])


def _out_mlp(h, y, wo, nw, wu, wd):
    m, d = h.shape
    k = y.shape[1]
    f = wu.shape[1]
    tm = _pick(m, (1024, 768, 512, 384, 256, 192, 128, 64))
    tf = _pick(f, (1024, 512, 256, 128))
    return pl.pallas_call(
        _out_mlp_kernel,
        grid=(m // tm, f // tf),
        in_specs=[
            pl.BlockSpec((tm, d), lambda i, j: (i, 0)),
            pl.BlockSpec((tm, k), lambda i, j: (i, 0)),
            pl.BlockSpec((k, d), lambda i, j: (0, 0)),
            pl.BlockSpec((1, d), lambda i, j: (0, 0)),
            pl.BlockSpec((d, tf), lambda i, j: (0, j)),
            pl.BlockSpec((tf, d), lambda i, j: (j, 0)),
        ],
        out_specs=pl.BlockSpec((tm, d), lambda i, j: (i, 0)),
        out_shape=jax.ShapeDtypeStruct((m, d), F32),
        scratch_shapes=[pltpu.VMEM((tm, d), BF16)],
        compiler_params=_params("parallel", "arbitrary"),
        name="out_mlp",
    )(h, y, wo, nw, wu, wd)


def _gdn_prep_kernel(x_ref, w_ref, o_ref, *, n_q_tiles, n_qk_tiles, tc, nc):
    c = pl.program_id(1)
    w = w_ref[...]
    taps = [w[GDN_CONV - 1 - s:GDN_CONV - s, :] for s in range(GDN_CONV)]
    rows = lax.broadcasted_iota(jnp.int32, (CHUNK, tc), 0)

    def conv_silu(xb, lead):
        y = xb[lead:lead + CHUNK] * taps[0]
        for s in range(1, GDN_CONV):
            y = y + xb[lead - s:lead - s + CHUNK] * taps[s]
        half_y = 0.5 * y
        return half_y + half_y * jnp.tanh(half_y)

    def emit(r0, y, normed, scale):
        if not normed:
            o_ref[pl.ds(r0, CHUNK), :] = y.astype(BF16)
            return
        for i in range(tc // GDN_HEAD_DIM):
            yi = y[:, i * GDN_HEAD_DIM:(i + 1) * GDN_HEAD_DIM]
            ss = jnp.sum(yi * yi, axis=-1, keepdims=True)
            o_ref[pl.ds(r0, CHUNK), i * GDN_HEAD_DIM:(i + 1) * GDN_HEAD_DIM] = (
                yi * (lax.rsqrt(ss + RMS_EPS) * scale)).astype(BF16)

    def run(normed, scale):
        x0 = jnp.concatenate([jnp.zeros((PREP_HALO, tc), F32), x_ref[0:CHUNK, :].astype(F32)], axis=0)
        emit(0, jnp.where(rows >= ROW0, conv_silu(x0, PREP_HALO), 0.0), normed, scale)

        def body(ci, carry):
            r0 = pl.multiple_of(ci * CHUNK, CHUNK)
            start = pl.multiple_of(ci * CHUNK - PREP_HALO, PREP_HALO)
            xb = x_ref[pl.ds(start, CHUNK + PREP_HALO), :].astype(F32)
            emit(r0, conv_silu(xb, PREP_HALO), normed, scale)
            return carry

        lax.fori_loop(1, nc, body, 0, unroll=PREP_UNROLL if (nc - 1) % PREP_UNROLL == 0 else 1)

    @pl.when(c < n_qk_tiles)
    def _():
        run(True, jnp.where(c < n_q_tiles, GDN_HEAD_DIM ** -0.5, 1.0).astype(F32))

    @pl.when(c >= n_qk_tiles)
    def _():
        run(False, None)


def _gdn_prep(proj, w_conv):
    b, tp, _ = proj.shape
    cdim = w_conv.shape[1]
    tc = 512
    qk = GDN_QK_HEADS * GDN_HEAD_DIM
    kern = functools.partial(_gdn_prep_kernel, n_q_tiles=qk // tc, n_qk_tiles=2 * qk // tc, tc=tc,
                             nc=tp // CHUNK)
    return pl.pallas_call(
        kern,
        grid=(b, cdim // tc),
        in_specs=[
            pl.BlockSpec((None, tp, tc), lambda i, c: (i, 0, c)),
            pl.BlockSpec((GDN_CONV, tc), lambda i, c: (0, c)),
        ],
        out_specs=pl.BlockSpec((None, tp, tc), lambda i, c: (i, 0, c)),
        out_shape=jax.ShapeDtypeStruct((b, tp, cdim), BF16),
        compiler_params=_params("parallel", "arbitrary"),
        name="gdn_prep",
    )(proj, w_conv)


def _tri_masks():
    ii = lax.broadcasted_iota(jnp.int32, (CHUNK, CHUNK), 0)
    jj = lax.broadcasted_iota(jnp.int32, (CHUNK, CHUNK), 1)
    return ii, jj


def _gdn_gates_kernel(g_ref, gt_ref, alc_ref, dtc_ref, alr_ref, dtr_ref,
                      beta_ref, cumc_ref, cumr_ref, *, nc, hg):
    nh = GDN_V_HEADS
    ii, jj = _tri_masks()
    tril = (ii >= jj).astype(F32)
    triu = (ii <= jj).astype(F32)
    neg_a_c = -jnp.exp(alc_ref[...])
    neg_a_r = -jnp.exp(alr_ref[...])
    for c in range(nc):
        blk = g_ref[c * CHUNK:(c + 1) * CHUNK, :]
        beta = jax.nn.sigmoid(blk[:, 0:nh])
        g = neg_a_c * _softplus(blk[:, nh:2 * nh] + dtc_ref[...])
        gr = neg_a_r * _softplus(gt_ref[:, c * CHUNK:(c + 1) * CHUNK] + dtr_ref[...])
        if c == 0:
            rows = lax.broadcasted_iota(jnp.int32, (CHUNK, nh), 0)
            cols = lax.broadcasted_iota(jnp.int32, (nh, CHUNK), 1)
            beta = jnp.where(rows >= ROW0, beta, 0.0)
            g = jnp.where(rows >= ROW0, g, 0.0)
            gr = jnp.where(cols >= ROW0, gr, 0.0)
        cum = jnp.dot(tril, g, preferred_element_type=F32, precision=lax.Precision.HIGHEST)
        cumr = jnp.dot(gr, triu, preferred_element_type=F32, precision=lax.Precision.HIGHEST)
        cumr = jnp.concatenate([cumr[:nh // 2], cumr[nh // 2:]], axis=1)
        for q in range(nh // hg):
            beta_ref[q, c] = beta[:, q * hg:(q + 1) * hg]
            cumc_ref[q, c] = cum[:, q * hg:(q + 1) * hg]
            cumr_ref[q, c] = cumr[q * hg // 2:(q + 1) * hg // 2, :]


def _gdn_gates(gates, a_log, dt_bias, hg):
    b, tp, _ = gates.shape
    nc = tp // CHUNK
    nh = GDN_V_HEADS
    ng = nh // hg
    kern = functools.partial(_gdn_gates_kernel, nc=nc, hg=hg)
    perm = np.concatenate([np.arange(0, nh, 2), np.arange(1, nh, 2)])
    gates_t = jnp.swapaxes(gates[:, :, nh:2 * nh], 1, 2)[:, perm, :]
    col = jax.ShapeDtypeStruct((b, ng, nc, CHUNK, hg), F32)
    row = jax.ShapeDtypeStruct((b, ng, nc, hg // 2, 2 * CHUNK), F32)
    col_spec = pl.BlockSpec((None, ng, nc, CHUNK, hg), lambda i: (i, 0, 0, 0, 0))
    row_spec = pl.BlockSpec((None, ng, nc, hg // 2, 2 * CHUNK), lambda i: (i, 0, 0, 0, 0))
    small = lambda shape: pl.BlockSpec(shape, lambda i: (0, 0))
    return pl.pallas_call(
        kern,
        grid=(b,),
        in_specs=[
            pl.BlockSpec((None, tp, LANES), lambda i: (i, 0, 0)),
            pl.BlockSpec((None, nh, tp), lambda i: (i, 0, 0)),
            small((1, nh)), small((1, nh)), small((nh, 1)), small((nh, 1)),
        ],
        out_specs=[col_spec, col_spec, row_spec],
        out_shape=[col, col, row],
        compiler_params=_params("parallel"),
        name="gdn_gates",
    )(gates, gates_t, a_log.reshape(1, nh), dt_bias.reshape(1, nh),
      a_log[perm].reshape(nh, 1), dt_bias[perm].reshape(nh, 1))


def _gdn_chunk_kernel(q_ref, k_ref, v_ref, z_ref, beta_ref, cumc_ref, cumr_ref, wn_ref, y_ref,
                      s_ref, u_ref, wq_ref, kd_ref, at_ref, *, nc, hg, cps):
    hd = GDN_HEAD_DIM
    iw = lax.broadcasted_iota(jnp.int32, (CHUNK, hd), 0)
    lane = lax.broadcasted_iota(jnp.int32, (CHUNK, hd), 1)
    jw = lane & (CHUNK - 1)
    first = lane < CHUNK
    causal = iw >= jw
    strict = iw > jw
    eye = (iw == jw).astype(F32)
    blk = [(iw >> l) == (jw >> l) for l in range(1, CHUNK.bit_length())]
    wn = wn_ref[...]
    heads = range(hg)
    pairs = range(hg // 2)

    def bdiag(x):
        zero = jnp.zeros_like(x)
        return jnp.concatenate([jnp.where(first, x, zero), jnp.where(first, zero, x)], axis=0)

    rstack = bdiag

    def prepare(chunks):
        units = [(i, p) for i in range(len(chunks)) for p in pairs]
        r0 = [pl.multiple_of(c * CHUNK, CHUNK) for c in chunks]
        cumc = [cumc_ref[c] for c in chunks]
        betac = [beta_ref[c] for c in chunks]
        cumr = [cumr_ref[c] for c in chunks]
        q = {(i, p): q_ref[pl.ds(r0[i], CHUNK), p * hd:(p + 1) * hd] for i, p in units}
        k = {(i, p): k_ref[pl.ds(r0[i], CHUNK), p * hd:(p + 1) * hd] for i, p in units}
        qkk = {u: _dot_nt(jnp.concatenate([q[u], k[u]], axis=0),
                          jnp.concatenate([k[u], k[u]], axis=0)) for u in units}
        cc = {(i, h): jnp.broadcast_to(cumc[i][:, h:h + 1], (CHUNK, hd))
              for i in range(len(chunks)) for h in heads}
        bc = {(i, h): jnp.broadcast_to(betac[i][:, h:h + 1], (CHUNK, hd))
              for i in range(len(chunks)) for h in heads}
        ccw = {(i, p): jnp.where(first, cc[i, 2 * p], cc[i, 2 * p + 1]) for i, p in units}
        bcw = {(i, p): jnp.where(first, bc[i, 2 * p], bc[i, 2 * p + 1]) for i, p in units}
        decay = {(i, p): jnp.exp(jnp.where(causal, ccw[i, p] - cumr[i][p:p + 1, :], -jnp.inf))
                 for i, p in units}
        a = {u: jnp.where(strict, bcw[u] * qkk[u][CHUNK:] * decay[u], 0.0) for u in units}
        for i, p in units:
            at_ref[i, p] = rstack((qkk[i, p][:CHUNK] * decay[i, p]).astype(BF16))
        t = {u: eye - jnp.where(blk[0], a[u], 0.0) for u in units}
        for lvl in range(1, len(blk)):
            a_off = {u: bdiag(jnp.where(blk[lvl], jnp.where(blk[lvl - 1], 0.0, a[u]), 0.0)
                              .astype(BF16)) for u in units}
            tb = {u: t[u].astype(BF16) for u in units}
            x = {u: _dot(tb[u], a_off[u]).astype(BF16) for u in units}
            t = {u: t[u] - _dot(x[u], bdiag(tb[u])) for u in units}
        ec = {u: jnp.exp(cc[u]) for u in cc}
        rhs = {}
        for i, p in units:
            kf = k[i, p].astype(F32)
            halves = []
            for h in (2 * p, 2 * p + 1):
                v = v_ref[pl.ds(r0[i], CHUNK), h * hd:(h + 1) * hd].astype(F32)
                halves.append(jnp.concatenate([(v * bc[i, h]).astype(BF16),
                                               (kf * (bc[i, h] * ec[i, h])).astype(BF16)], axis=1))
            rhs[i, p] = jnp.concatenate(halves, axis=0)
        sol = {u: _dot(rstack(t[u].astype(BF16)), rhs[u]) for u in units}
        for i, p in units:
            qf = q[i, p].astype(F32)
            kf = k[i, p].astype(F32)
            for r in range(2):
                h = 2 * p + r
                sh = sol[i, p][r * CHUNK:(r + 1) * CHUNK]
                u_ref[i, h] = sh[:, :hd]
                wq_ref[i, h] = jnp.concatenate(
                    [sh[:, hd:].astype(BF16), (qf * ec[i, h]).astype(BF16)], axis=0)
                kd_ref[i, h] = (kf * jnp.exp(cc[i, h][CHUNK - 1:CHUNK, :] - cc[i, h])).astype(BF16)

    def recur(c, slot, s):
        r0 = pl.multiple_of(c * CHUNK, CHUNK)
        cumc = cumc_ref[c]
        ws = [_dot(wq_ref[slot, h], s[h].astype(BF16)) for h in heads]
        v_new = [(u_ref[slot, h] - ws[h][:CHUNK]).astype(BF16) for h in heads]
        upd = [_dot_tn(kd_ref[slot, h], v_new[h]) for h in heads]
        av = [_dot(at_ref[slot, p], jnp.concatenate([v_new[2 * p], v_new[2 * p + 1]], axis=0))
              for p in pairs]
        s_new = [s[h] * jnp.exp(cumc[CHUNK - 1:CHUNK, h:h + 1]) + upd[h] for h in heads]
        for h in heads:
            o = ws[h][CHUNK:] + av[h // 2][(h % 2) * CHUNK:(h % 2 + 1) * CHUNK]
            z = z_ref[pl.ds(r0, CHUNK), h * hd:(h + 1) * hd].astype(F32)
            y = _norm_rows(o, wn) * (z * jax.nn.sigmoid(z))
            y_ref[pl.ds(r0, CHUNK), h * hd:(h + 1) * hd] = y.astype(BF16)
        return s_new

    s_ref[...] = jnp.zeros_like(s_ref)
    prepare([jnp.int32(i) for i in range(min(cps, nc))])

    def body(it, carry):
        c0 = it * cps
        s = [s_ref[h] for h in heads]
        for i in range(cps):
            s = recur(c0 + i, i, s)
        for h in heads:
            s_ref[h] = s[h]
        prepare([jnp.minimum(c0 + cps + i, nc - 1) for i in range(cps)])
        return carry

    lax.fori_loop(0, nc // cps, body, 0)
    s = [s_ref[h] for h in heads]
    for i in range(nc % cps):
        s = recur(jnp.int32(nc - nc % cps + i), i, s)


def _gdn_chunk(qkv, proj, beta, cumc, cumr, w_norm, hg):
    b, tp, _ = qkv.shape
    nc = tp // CHUNK
    hd = GDN_HEAD_DIM
    ng = GDN_V_HEADS // hg
    wqk = hd * hg // 2
    wv = hd * hg
    qk_dim = GDN_QK_HEADS * hd
    v_dim = GDN_V_HEADS * hd
    cps = GDN_CHUNKS_PER_STEP
    kern = functools.partial(_gdn_chunk_kernel, nc=nc, hg=hg, cps=cps)
    col_spec = pl.BlockSpec((None, None, nc, CHUNK, hg), lambda i, j: (i, j, 0, 0, 0))
    row_spec = pl.BlockSpec((None, None, nc, hg // 2, 2 * CHUNK), lambda i, j: (i, j, 0, 0, 0))
    return pl.pallas_call(
        kern,
        grid=(b, ng),
        in_specs=[
            pl.BlockSpec((None, tp, wqk), lambda i, j: (i, 0, j)),
            pl.BlockSpec((None, tp, wqk), lambda i, j: (i, 0, qk_dim // wqk + j)),
            pl.BlockSpec((None, tp, wv), lambda i, j: (i, 0, 2 * qk_dim // wv + j)),
            pl.BlockSpec((None, tp, wv), lambda i, j: (i, 0, (2 * qk_dim + v_dim) // wv + j)),
            col_spec, col_spec, row_spec,
            pl.BlockSpec((1, hd), lambda i, j: (0, 0)),
        ],
        out_specs=pl.BlockSpec((None, tp, wv), lambda i, j: (i, 0, j)),
        out_shape=jax.ShapeDtypeStruct((b, tp, v_dim), BF16),
        scratch_shapes=[
            pltpu.VMEM((hg, hd, hd), F32),
            pltpu.VMEM((cps, hg, CHUNK, hd), F32),
            pltpu.VMEM((cps, hg, 2 * CHUNK, hd), BF16),
            pltpu.VMEM((cps, hg, CHUNK, hd), BF16),
            pltpu.VMEM((cps, hg // 2, 2 * CHUNK, hd), BF16),
        ],
        compiler_params=_params("parallel", "arbitrary"),
        name="gdn_chunk",
    )(qkv, qkv, qkv, proj, beta, cumc, cumr, w_norm.reshape(1, hd))


def _gdn_layer(h, nw, w_in, w_conv, a_log, dt_bias, w_norm, w_out):
    b, tp, d = h.shape
    m = b * tp
    conv_dim = 2 * GDN_QK_HEADS * GDN_HEAD_DIM + GDN_V_HEADS * GDN_HEAD_DIM
    main = conv_dim + GDN_V_HEADS * GDN_HEAD_DIM
    hg = GDN_HEADS_PER_STEP
    w_main = w_in[:, :main].astype(BF16)
    w_gate = jnp.pad(w_in[:, main:], ((0, 0), (0, LANES - 2 * GDN_V_HEADS))).astype(BF16)
    proj, gates = _norm_matmul(h.reshape(m, d), nw.reshape(1, d), w_main, w_gate)
    proj = proj.reshape(b, tp, main)
    gates = gates.reshape(b, tp, LANES)
    beta, cumc, cumr = _gdn_gates(gates, a_log, dt_bias, hg)
    qkv = _gdn_prep(proj, w_conv)
    y = _gdn_chunk(qkv, proj, beta, cumc, cumr, w_norm, hg)
    return y.reshape(m, -1), w_out.astype(BF16)


def _mlstm_gates_kernel(g_ref, gt_ref, bc_ref, br_ref, cumc_ref, rc_ref, rr_ref, *, nc, hg):
    nh = MLSTM_HEADS
    ii, jj = _tri_masks()
    tril = (ii >= jj).astype(F32)
    triu = (ii <= jj).astype(F32)

    def split(raw, axis):
        capped = GATE_SOFTCAP * jnp.tanh(raw / GATE_SOFTCAP)
        if axis == 1:
            i_pre, f_pre = capped[:, :nh], capped[:, nh:2 * nh]
        else:
            i_pre, f_pre = capped[:nh], capped[nh:2 * nh]
        return i_pre, -_softplus(-f_pre)

    for c in range(nc):
        i_c, lf_c = split(g_ref[c * CHUNK:(c + 1) * CHUNK, 0:2 * nh] + bc_ref[...], 1)
        i_r, lf_r = split(gt_ref[:, c * CHUNK:(c + 1) * CHUNK] + br_ref[...], 0)
        if c == 0:
            rows = lax.broadcasted_iota(jnp.int32, (CHUNK, nh), 0)
            cols = lax.broadcasted_iota(jnp.int32, (nh, CHUNK), 1)
            i_c = jnp.where(rows >= ROW0, i_c, -jnp.inf)
            lf_c = jnp.where(rows >= ROW0, lf_c, 0.0)
            i_r = jnp.where(cols >= ROW0, i_r, -jnp.inf)
            lf_r = jnp.where(cols >= ROW0, lf_r, 0.0)
        cum = jnp.dot(tril, lf_c, preferred_element_type=F32, precision=lax.Precision.HIGHEST)
        cumr = jnp.dot(lf_r, triu, preferred_element_type=F32, precision=lax.Precision.HIGHEST)
        rc = i_c - cum
        rr = i_r - cumr
        for q in range(nh // hg):
            cumc_ref[q, c] = cum[:, q * hg:(q + 1) * hg]
            rc_ref[q, c] = rc[:, q * hg:(q + 1) * hg]
            rr_ref[q, c] = rr[q * hg:(q + 1) * hg, :]


def _mlstm_gates(gates, gates_t, bias, hg):
    b, tp, _ = gates.shape
    nc = tp // CHUNK
    nh = MLSTM_HEADS
    ng = nh // hg
    kern = functools.partial(_mlstm_gates_kernel, nc=nc, hg=hg)
    col = jax.ShapeDtypeStruct((b, ng, nc, CHUNK, hg), F32)
    row = jax.ShapeDtypeStruct((b, ng, nc, hg, CHUNK), F32)
    col_spec = pl.BlockSpec((None, ng, nc, CHUNK, hg), lambda i: (i, 0, 0, 0, 0))
    row_spec = pl.BlockSpec((None, ng, nc, hg, CHUNK), lambda i: (i, 0, 0, 0, 0))
    return pl.pallas_call(
        kern,
        grid=(b,),
        in_specs=[
            pl.BlockSpec((None, tp, LANES), lambda i: (i, 0, 0)),
            pl.BlockSpec((None, 2 * nh, tp), lambda i: (i, 0, 0)),
            pl.BlockSpec((1, 2 * nh), lambda i: (0, 0)),
            pl.BlockSpec((2 * nh, 1), lambda i: (0, 0)),
        ],
        out_specs=[col_spec, col_spec, row_spec],
        out_shape=[col, col, row],
        compiler_params=_params("parallel"),
        name="mlstm_gates",
    )(gates, gates_t, bias.reshape(1, 2 * nh), bias.reshape(2 * nh, 1))


def _mlstm_chunk_kernel(q_ref, k_ref, v_ref, og_ref, cumc_ref, rc_ref, rr_ref, wn_ref, y_ref,
                        c_ref, m_ref, *, nc, hg, cps):
    dv = MLSTM_DV
    ii, jj = _tri_masks()
    causal = ii >= jj
    ones_col = jnp.ones((CHUNK, LANES), BF16)
    heads = range(hg)

    c_ref[...] = jnp.zeros_like(c_ref)
    m_ref[...] = jnp.zeros_like(m_ref)

    def step(chunks):
        n = len(chunks)
        units = [(i, h) for i in range(n) for h in heads]
        r0 = [pl.multiple_of(c * CHUNK, CHUNK) for c in chunks]
        cumc = [cumc_ref[c] for c in chunks]
        rcol = [rc_ref[c] for c in chunks]
        rrow = [rr_ref[c] for c in chunks]
        q = {(i, h): q_ref[pl.ds(r0[i], CHUNK), h * LANES:(h + 1) * LANES] for i, h in units}
        k = {(i, h): k_ref[pl.ds(r0[i], CHUNK), h * LANES:(h + 1) * LANES] for i, h in units}
        v_aug = {(i, h): jnp.concatenate(
            [v_ref[pl.ds(r0[i], CHUNK), h * dv:(h + 1) * dv], ones_col], axis=1) for i, h in units}
        qk = {u: _dot_nt(q[u], k[u]) for u in units}
        cc = {(i, h): jnp.broadcast_to(cumc[i][:, h:h + 1], (CHUNK, LANES)) for i, h in units}
        rc = {(i, h): jnp.broadcast_to(rcol[i][:, h:h + 1], (CHUNK, LANES)) for i, h in units}
        rr = {(i, h): rrow[i][h:h + 1, :] for i, h in units}
        rmax = {u: jnp.max(jnp.where(causal, rr[u], -jnp.inf), axis=-1, keepdims=True) for u in units}
        m_in, keep, kw = {}, {}, {}
        m = [m_ref[h][0:1, :] for h in heads]
        for i, h in units:
            m_in[i, h] = m[h]
            c_last = cc[i, h][CHUNK - 1:CHUNK, :]
            log_keep = c_last + m[h]
            m_new = jnp.maximum(log_keep, c_last + jnp.max(rc[i, h], axis=0, keepdims=True))
            keep[i, h] = jnp.exp(log_keep - m_new)
            w_end = jnp.exp(c_last + rc[i, h] - m_new)
            kw[i, h] = (k[i, h].astype(F32) * w_end).astype(BF16)
            m[h] = m_new
        for h in heads:
            m_ref[h] = jnp.broadcast_to(m[h], m_ref.shape[1:])
        upd = {u: _dot_tn(kw[u], v_aug[u]) for u in units}
        g, w_intra = {}, {}
        for u in units:
            g[u] = jnp.maximum(m_in[u], rmax[u])
            w_intra[u] = (jnp.exp(jnp.where(causal, rr[u] - g[u][:, :CHUNK], -jnp.inf))
                          * qk[u]).astype(BF16)
        wv = {u: _dot(w_intra[u], v_aug[u]) for u in units}
        state = [c_ref[h] for h in heads]
        for i in range(n):
            qc = [_dot(q[i, h], state[h].astype(BF16)) for h in heads]
            for h in heads:
                s_inter = jnp.exp(m_in[i, h] - g[i, h])
                tot = jnp.concatenate([s_inter, s_inter], axis=1) * qc[h] + wv[i, h]
                den = tot[:, dv:]
                inv = 1.0 / jnp.maximum(jnp.abs(den), jnp.exp(-(cc[i, h] + g[i, h])))
                hs = tot[:, :dv] * inv
                og = og_ref[pl.ds(r0[i], CHUNK), h * dv:(h + 1) * dv].astype(F32)
                y = _norm_rows(hs, wn_ref[:, h * dv:(h + 1) * dv]) * jax.nn.sigmoid(og)
                y_ref[pl.ds(r0[i], CHUNK), h * dv:(h + 1) * dv] = y.astype(BF16)
                state[h] = jnp.concatenate([keep[i, h], keep[i, h]], axis=1) * state[h] + upd[i, h]
        for h in heads:
            c_ref[h] = state[h]

    def body(it, carry):
        step([it * cps + i for i in range(cps)])
        return carry

    lax.fori_loop(0, nc // cps, body, 0)
    if nc % cps:
        step([jnp.int32(nc - nc % cps + i) for i in range(nc % cps)])


def _mlstm_chunk(proj, cumc, rc, rr, w_norm, hg):
    b, tp, _ = proj.shape
    nc = tp // CHUNK
    nh = MLSTM_HEADS
    ng = nh // hg
    wb = LANES * hg
    kern = functools.partial(_mlstm_chunk_kernel, nc=nc, hg=hg, cps=MLSTM_CHUNKS_PER_STEP)
    col_spec = pl.BlockSpec((None, None, nc, CHUNK, hg), lambda i, j: (i, j, 0, 0, 0))
    row_spec = pl.BlockSpec((None, None, nc, hg, CHUNK), lambda i, j: (i, j, 0, 0, 0))
    return pl.pallas_call(
        kern,
        grid=(b, ng),
        in_specs=[
            pl.BlockSpec((None, tp, wb), lambda i, j: (i, 0, j)),
            pl.BlockSpec((None, tp, wb), lambda i, j: (i, 0, ng + j)),
            pl.BlockSpec((None, tp, wb), lambda i, j: (i, 0, 2 * ng + j)),
            pl.BlockSpec((None, tp, wb), lambda i, j: (i, 0, 3 * ng + j)),
            col_spec, col_spec, row_spec,
            pl.BlockSpec((1, wb), lambda i, j: (0, j)),
        ],
        out_specs=pl.BlockSpec((None, tp, wb), lambda i, j: (i, 0, j)),
        out_shape=jax.ShapeDtypeStruct((b, tp, nh * MLSTM_DV), BF16),
        scratch_shapes=[
            pltpu.VMEM((hg, LANES, 2 * MLSTM_DV), F32),
            pltpu.VMEM((hg, 8, LANES), F32),
        ],
        compiler_params=_params("parallel", "arbitrary"),
        name="mlstm_chunk",
    )(proj, proj, proj, proj, cumc, rc, rr, w_norm.reshape(1, nh * MLSTM_DV))


def _pad_heads(w, nh, dh):
    d = w.shape[0]
    w = w.reshape(d, nh, dh)
    return jnp.pad(w, ((0, 0), (0, 0), (0, LANES - dh))).reshape(d, nh * LANES)


def _mlstm_layer(h, nw, w_in, gate_bias, w_norm, w_out):
    b, tp, d = h.shape
    m = b * tp
    nh = MLSTM_HEADS
    qk = nh * MLSTM_DQK
    vd = nh * MLSTM_DV
    hg = 4
    w_main = jnp.concatenate([
        _pad_heads(w_in[:, :qk], nh, MLSTM_DQK),
        _pad_heads(w_in[:, qk:2 * qk] * (MLSTM_DQK ** -0.5), nh, MLSTM_DQK),
        w_in[:, 2 * qk:2 * qk + 2 * vd],
    ], axis=1).astype(BF16)
    w_gate = jnp.pad(w_in[:, 2 * qk + 2 * vd:], ((0, 0), (0, LANES - 2 * nh))).astype(BF16)
    proj, gates = _norm_matmul(h.reshape(m, d), nw.reshape(1, d), w_main, w_gate)
    proj = proj.reshape(b, tp, -1)
    gates = gates.reshape(b, tp, LANES)
    gates_t = jnp.swapaxes(gates[:, :, :2 * nh], 1, 2)
    cumc, rc, rr = _mlstm_gates(gates, gates_t, gate_bias, hg)
    y = _mlstm_chunk(proj, cumc, rc, rr, w_norm, hg)
    return y.reshape(m, vd), w_out.astype(BF16)


def _mla_proj_kernel(h_ref, nw_ref, win_ref, qn_ref, wuq_ref, wuqs_ref, kvn_ref, wuk_ref, wuv_ref,
                     qtab_ref, ktab_ref, q_out, k_out, v_out):
    xn = _norm_rows(h_ref[...], nw_ref[...]).astype(BF16)
    c = _dot(xn, win_ref[...])
    cq = _norm_rows(c[:, :MLA_Q_RANK], qn_ref[...]).astype(BF16)
    lat = MLA_Q_RANK + MLA_KV_RANK
    ckv = _norm_rows(c[:, MLA_Q_RANK:lat], kvn_ref[...]).astype(BF16)
    kr = c[:, lat:lat + LANES]
    kr_swap = c[:, lat + LANES:]
    q = _dot(cq, wuq_ref[...])
    q_swap = _dot(cq, wuqs_ref[...])
    kn = _dot(ckv, wuk_ref[...])
    v_out[...] = _dot(ckv, wuv_ref[...]).astype(BF16)

    def head_norm_rope(x, x_swap, tab_ref):
        ms = jnp.sum(x * x, axis=-1, keepdims=True) * (1.0 / MLA_QK)
        return (x * tab_ref[0] + x_swap * tab_ref[1]) * lax.rsqrt(ms + RMS_EPS)

    for h in range(MLA_HEADS):
        sl = slice(h * LANES, (h + 1) * LANES)
        q_out[:, sl] = head_norm_rope(q[:, sl], q_swap[:, sl], qtab_ref).astype(BF16)
        k_out[:, sl] = head_norm_rope(kn[:, sl] + kr, kr_swap, ktab_ref).astype(BF16)


def _mla_proj(h, nw, w_in, q_norm, w_uq, w_uq_swap, kv_norm, w_uk, w_uv, qtab, ktab):
    b, tp, d = h.shape
    tt = _pick(tp, (704, 352, 192, 64))
    nh = MLA_HEADS
    full = lambda a: pl.BlockSpec(a.shape, lambda i, j: (0,) * a.ndim)
    tab = pl.BlockSpec((2, tt, LANES), lambda i, j: (0, j, 0))
    row = lambda n: pl.BlockSpec((None, tt, n), lambda i, j: (i, j, 0))
    return pl.pallas_call(
        _mla_proj_kernel,
        grid=(b, tp // tt),
        in_specs=[row(d), full(nw), full(w_in), full(q_norm), full(w_uq), full(w_uq_swap),
                  full(kv_norm), full(w_uk), full(w_uv), tab, tab],
        out_specs=[row(nh * LANES), row(nh * LANES), row(nh * MLA_V)],
        out_shape=[
            jax.ShapeDtypeStruct((b, tp, nh * LANES), BF16),
            jax.ShapeDtypeStruct((b, tp, nh * LANES), BF16),
            jax.ShapeDtypeStruct((b, tp, nh * MLA_V), BF16),
        ],
        compiler_params=_params("parallel", "arbitrary"),
        name="mla_proj",
    )(h, nw, w_in, q_norm, w_uq, w_uq_swap, kv_norm, w_uk, w_uv, qtab, ktab)


def _mla_attn_kernel(q_ref, k_ref, v_ref, o_ref, *, tq, nq):
    ri = lax.broadcasted_iota(jnp.int32, (tq, tq), 0)
    ci = lax.broadcasted_iota(jnp.int32, (tq, tq), 1)
    tri = jnp.where(ci <= ri, 0.0, NEG_BIG)
    tri0 = jnp.where(ci >= ROW0, tri, NEG_BIG)
    pad_row = jnp.where(lax.broadcasted_iota(jnp.int32, (1, max(nq - 1, 1) * tq), 1) >= ROW0, 0.0, NEG_BIG)
    lane = lax.broadcasted_iota(jnp.int32, (tq, LANES), 1)
    units = [(qi, r) for qi in range(nq) for r in range(2)]

    def scores(qi, r):
        lo = qi * tq
        hs = slice(r * LANES, (r + 1) * LANES)
        q = q_ref[lo:lo + tq, hs]
        s_diag = _dot_nt(q, k_ref[lo:lo + tq, hs]) + (tri if qi else tri0)
        s_main = _dot_nt(q, k_ref[0:lo, hs]) + pad_row[:, :lo] if qi else None
        return s_main, s_diag

    def attend(qi, s_main, s_diag):
        lo = qi * tq
        m = jnp.max(s_diag, axis=-1, keepdims=True)
        if qi:
            m = jnp.maximum(m, jnp.max(s_main, axis=-1, keepdims=True))
        p = jnp.exp(s_diag - m)
        l = jnp.sum(p, axis=-1, keepdims=True)
        o = _dot(p.astype(BF16), v_ref[lo:lo + tq, :])
        if qi:
            p = jnp.exp(s_main - m)
            l = l + jnp.sum(p, axis=-1, keepdims=True)
            o = o + _dot(p.astype(BF16), v_ref[0:lo, :])
        return o * (1.0 / l)

    nxt = scores(*units[0])
    outs = []
    for n, (qi, r) in enumerate(units):
        cur = nxt
        if n + 1 < len(units):
            nxt = scores(*units[n + 1])
        outs.append(attend(qi, *cur))
        if r == 1:
            o = jnp.where(lane < MLA_V, outs[0], outs[1])
            outs = []
            if qi == 0:
                o = jnp.where(lax.broadcasted_iota(jnp.int32, (tq, LANES), 0) >= ROW0, o, 0.0)
            o_ref[qi * tq:(qi + 1) * tq, :] = o.astype(BF16)


def _mla_attn(q, k, v):
    b, tp, _ = q.shape
    tq = _pick(tp, (192, 64))
    npair = MLA_HEADS // 2
    kern = functools.partial(_mla_attn_kernel, tq=tq, nq=tp // tq)
    return pl.pallas_call(
        kern,
        grid=(b, npair),
        in_specs=[
            pl.BlockSpec((None, tp, 2 * LANES), lambda i, j: (i, 0, j)),
            pl.BlockSpec((None, tp, 2 * LANES), lambda i, j: (i, 0, j)),
            pl.BlockSpec((None, tp, LANES), lambda i, j: (i, 0, j)),
        ],
        out_specs=pl.BlockSpec((None, tp, LANES), lambda i, j: (i, 0, j)),
        out_shape=jax.ShapeDtypeStruct((b, tp, MLA_HEADS * MLA_V), BF16),
        compiler_params=_params("parallel", "arbitrary"),
        name="mla_attn",
    )(q, k, v)


def _swap_rope_halves(a):
    half = MLA_ROPE // 2
    lo, hi = MLA_NOPE, MLA_NOPE + half
    return jnp.concatenate([a[..., :lo], a[..., hi:hi + half], a[..., lo:hi], a[..., hi + half:]], axis=-1)


def _rope_tables(tp, gain, scale):
    half = MLA_ROPE // 2
    pos = jnp.arange(tp, dtype=F32) - float(ROW0)
    inv_freq = ROPE_THETA ** (-jnp.arange(0, MLA_ROPE, 2, dtype=F32) / MLA_ROPE)
    ang = pos[:, None] * inv_freq[None, :]
    cos, sin = jnp.cos(ang), jnp.sin(ang)
    ones = jnp.ones((tp, MLA_NOPE), F32)
    z_nope = jnp.zeros((tp, MLA_NOPE), F32)
    z_tail = jnp.zeros((tp, LANES - MLA_QK), F32)
    cos_t = jnp.concatenate([ones, cos, cos, z_tail], axis=1)
    sin_t = jnp.concatenate([z_nope, -sin, sin, z_tail], axis=1)
    g = jnp.pad(gain, (0, LANES - MLA_QK)).reshape(1, LANES) * scale
    return jnp.stack([cos_t * g, sin_t * _swap_rope_halves(g)])


def _mla_layer(h, nw, w_in, q_norm, w_uq, kv_norm, w_ukv, q_head_norm, k_head_norm, w_out):
    b, tp, d = h.shape
    m = b * tp
    nh = MLA_HEADS
    lat = MLA_Q_RANK + MLA_KV_RANK
    w_kr = jnp.concatenate([jnp.zeros((d, MLA_NOPE), F32), w_in[:, lat:],
                            jnp.zeros((d, LANES - MLA_QK), F32)], axis=1)
    w_in_p = jnp.concatenate([w_in[:, :lat], w_kr, _swap_rope_halves(w_kr)], axis=1).astype(BF16)
    w_uq_p = _pad_heads(w_uq, nh, MLA_QK)
    w_uq_swap = _swap_rope_halves(w_uq_p.reshape(MLA_Q_RANK, nh, LANES)).reshape(MLA_Q_RANK, nh * LANES)
    w_uq_p, w_uq_swap = w_uq_p.astype(BF16), w_uq_swap.astype(BF16)
    w_ukv3 = w_ukv.reshape(MLA_KV_RANK, nh, MLA_NOPE + MLA_V)
    w_uk_p = _pad_heads(w_ukv3[:, :, :MLA_NOPE].reshape(MLA_KV_RANK, nh * MLA_NOPE), nh, MLA_NOPE).astype(BF16)
    w_uv = w_ukv3[:, :, MLA_NOPE:].reshape(MLA_KV_RANK, nh * MLA_V).astype(BF16)
    q, k, v = _mla_proj(h, nw.reshape(1, d), w_in_p, q_norm.reshape(1, -1), w_uq_p, w_uq_swap,
                        kv_norm.reshape(1, -1), w_uk_p, w_uv,
                        _rope_tables(tp, q_head_norm, MLA_QK ** -0.5),
                        _rope_tables(tp, k_head_norm, 1.0))
    o = _mla_attn(q, k, v)
    return o.reshape(m, nh * MLA_V), w_out.astype(BF16)


def kernel(x, meta_tokens, attn_norm, ffn_norm, ff_up, ff_down, gdn_in, gdn_conv, gdn_a_log, gdn_dt_bias, gdn_norm, gdn_out, mlstm_in, mlstm_gate_bias, mlstm_norm, mlstm_out, mla_in, mla_q_norm, mla_uq, mla_kv_norm, mla_ukv, mla_q_head_norm, mla_k_head_norm, mla_out):
    b, t, d = x.shape
    depth = attn_norm.shape[0]
    meta = jnp.broadcast_to(meta_tokens[None].astype(x.dtype), (b, N_META, d))
    h = jnp.concatenate([jnp.zeros((b, LEAD_PAD, d), x.dtype), meta, x], axis=1)
    tp = h.shape[1]
    for layer in range(depth):
        kind, j = layer % N_MIXERS, layer // N_MIXERS
        if kind == 0:
            y, w_out = _gdn_layer(h, attn_norm[layer], gdn_in[j], gdn_conv[j], gdn_a_log[j],
                                  gdn_dt_bias[j], gdn_norm[j], gdn_out[j])
        elif kind == 1:
            y, w_out = _mlstm_layer(h, attn_norm[layer], mlstm_in[j], mlstm_gate_bias[j],
                                    mlstm_norm[j], mlstm_out[j])
        else:
            y, w_out = _mla_layer(h, attn_norm[layer], mla_in[j], mla_q_norm[j], mla_uq[j],
                                  mla_kv_norm[j], mla_ukv[j], mla_q_head_norm[j],
                                  mla_k_head_norm[j], mla_out[j])
        h = _out_mlp(h.reshape(b * tp, d), y, w_out, ffn_norm[layer].reshape(1, d),
                     ff_up[layer].astype(BF16), ff_down[layer].astype(BF16)).reshape(b, tp, d)
    return h[:, LEAD_PAD + N_META:]
```

```python
import functools

import jax
import jax.numpy as jnp
import numpy as np
from jax import lax
from jax.experimental import pallas as pl
from jax.experimental.pallas import tpu as pltpu

F32 = jnp.float32
BF16 = jnp.bfloat16

N_META = 16
CHUNK = 64
LEAD_PAD = (-N_META) % CHUNK
ROW0 = LEAD_PAD
RMS_EPS = 1e-6
N_MIXERS = 3

GDN_QK_HEADS = 8
GDN_V_HEADS = 16
GDN_HEAD_DIM = 128
GDN_CONV = 4
PREP_HALO = 16
GDN_HEADS_PER_STEP = 8
GDN_CHUNKS_PER_STEP = 3

MLSTM_HEADS = 8
MLSTM_DQK = 64
MLSTM_DV = 128
GATE_SOFTCAP = 15.0
MLSTM_CHUNKS_PER_STEP = 2

MLA_HEADS = 16
MLA_NOPE = 64
MLA_ROPE = 32
MLA_QK = MLA_NOPE + MLA_ROPE
MLA_V = 64
MLA_Q_RANK = 384
MLA_KV_RANK = 256
ROPE_THETA = 10000.0

LANES = 128
NEG_BIG = -1e30
VMEM_LIMIT = 56 * 1024 * 1024


def _pick(n, candidates):
    for c in candidates:
        if n % c == 0:
            return c
    raise ValueError(f"no tile for {n} in {candidates}")


def _params(*sem):
    return pltpu.CompilerParams(dimension_semantics=sem, vmem_limit_bytes=VMEM_LIMIT)


def _norm_rows(x, w):
    ms = jnp.mean(x * x, axis=-1, keepdims=True)
    return x * lax.rsqrt(ms + RMS_EPS) * w


def _silu(x):
    half_x = 0.5 * x
    return half_x + half_x * jnp.tanh(half_x)


def _softplus(x):
    return jnp.maximum(x, 0.0) + jnp.log1p(jnp.exp(-jnp.abs(x)))


def _dot(a, b):
    return jnp.dot(a, b, preferred_element_type=F32)


def _dot_nt(a, b):
    return lax.dot_general(a, b, (((1,), (1,)), ((), ())), preferred_element_type=F32)


def _dot_tn(a, b):
    return lax.dot_general(a, b, (((0,), (0,)), ((), ())), preferred_element_type=F32)


def _norm_matmul_kernel(x_ref, nw_ref, w_ref, wg_ref, o_ref, g_ref, xn_ref):
    @pl.when(pl.program_id(1) == 0)
    def _():
        xn = _norm_rows(x_ref[...], nw_ref[...]).astype(BF16)
        xn_ref[...] = xn
        g_ref[...] = _dot(xn, wg_ref[...])

    o_ref[...] = _dot(xn_ref[...], w_ref[...]).astype(o_ref.dtype)


def _norm_matmul_plain_kernel(x_ref, nw_ref, w_ref, o_ref, xn_ref):
    @pl.when(pl.program_id(1) == 0)
    def _():
        xn_ref[...] = _norm_rows(x_ref[...], nw_ref[...]).astype(BF16)

    o_ref[...] = _dot(xn_ref[...], w_ref[...]).astype(o_ref.dtype)


def _norm_matmul_plain(x, nw, w):
    m, d = x.shape
    n = w.shape[1]
    tm = _pick(m, (1024, 768, 512, 384, 256, 192, 128, 64))
    tn = _pick(n, (2048, 1024, 512, 256, 128))
    return pl.pallas_call(
        _norm_matmul_plain_kernel,
        grid=(m // tm, n // tn),
        in_specs=[
            pl.BlockSpec((tm, d), lambda i, j: (i, 0)),
            pl.BlockSpec((1, d), lambda i, j: (0, 0)),
            pl.BlockSpec((d, tn), lambda i, j: (0, j)),
        ],
        out_specs=pl.BlockSpec((tm, tn), lambda i, j: (i, j)),
        out_shape=jax.ShapeDtypeStruct((m, n), BF16),
        scratch_shapes=[pltpu.VMEM((tm, d), BF16)],
        compiler_params=_params("parallel", "arbitrary"),
        name="norm_matmul_plain",
    )(x, nw, w)


def _norm_matmul(x, nw, w, wg):
    m, d = x.shape
    n = w.shape[1]
    tm = _pick(m, (1024, 768, 512, 384, 256, 192, 128, 64))
    tn = _pick(n, (3072, 2048, 1024, 512, 256, 128))
    return pl.pallas_call(
        _norm_matmul_kernel,
        grid=(m // tm, n // tn),
        in_specs=[
            pl.BlockSpec((tm, d), lambda i, j: (i, 0)),
            pl.BlockSpec((1, d), lambda i, j: (0, 0)),
            pl.BlockSpec((d, tn), lambda i, j: (0, j)),
            pl.BlockSpec((d, LANES), lambda i, j: (0, 0)),
        ],
        out_specs=[
            pl.BlockSpec((tm, tn), lambda i, j: (i, j)),
            pl.BlockSpec((tm, LANES), lambda i, j: (i, 0)),
        ],
        out_shape=[
            jax.ShapeDtypeStruct((m, n), BF16),
            jax.ShapeDtypeStruct((m, LANES), F32),
        ],
        scratch_shapes=[pltpu.VMEM((tm, d), BF16)],
        compiler_params=_params("parallel", "arbitrary"),
        name="norm_matmul",
    )(x, nw, w, wg)


def _out_mlp_kernel(h_ref, y_ref, wo_ref, nw_ref, wu_ref, wd_ref, o_ref, xn_ref):
    j = pl.program_id(1)

    @pl.when(j == 0)
    def _():
        h = h_ref[...] + _dot(y_ref[...], wo_ref[...])
        xn_ref[...] = _norm_rows(h, nw_ref[...]).astype(BF16)
        o_ref[...] = h

    a = jnp.maximum(_dot(xn_ref[...], wu_ref[...]), 0.0)
    o_ref[...] += _dot((a * a).astype(BF16), wd_ref[...])


def _out_mlp(h, y, wo, nw, wu, wd):
    m, d = h.shape
    k = y.shape[1]
    f = wu.shape[1]
    tm = _pick(m, (1024, 768, 512, 384, 256, 192, 128, 64))
    tf = _pick(f, (1024, 512, 256, 128))
    return pl.pallas_call(
        _out_mlp_kernel,
        grid=(m // tm, f // tf),
        in_specs=[
            pl.BlockSpec((tm, d), lambda i, j: (i, 0)),
            pl.BlockSpec((tm, k), lambda i, j: (i, 0)),
            pl.BlockSpec((k, d), lambda i, j: (0, 0)),
            pl.BlockSpec((1, d), lambda i, j: (0, 0)),
            pl.BlockSpec((d, tf), lambda i, j: (0, j)),
            pl.BlockSpec((tf, d), lambda i, j: (j, 0)),
        ],
        out_specs=pl.BlockSpec((tm, d), lambda i, j: (i, 0)),
        out_shape=jax.ShapeDtypeStruct((m, d), F32),
        scratch_shapes=[pltpu.VMEM((tm, d), BF16)],
        compiler_params=_params("parallel", "arbitrary"),
        name="out_mlp",
    )(h, y, wo, nw, wu, wd)


def _gdn_in_proj_kernel(x_ref, nw_ref, w_ref, wg_ref, wc_ref, o_ref, g_ref, xn_ref, halo_ref, *,
                        tm, tn, tp, n_q_tiles, n_qk_tiles):
    i = pl.program_id(0)
    j = pl.program_id(1)

    @pl.when(j == 0)
    def _():
        xn = _norm_rows(x_ref[...], nw_ref[...]).astype(BF16)
        xn_ref[...] = xn
        g_ref[...] = _dot(xn, wg_ref[...])

    @pl.when(i == 0)
    def _():
        halo_ref[j] = jnp.zeros((PREP_HALO, tn), F32)

    acc = _dot(xn_ref[...], w_ref[...])
    xb = jnp.concatenate([halo_ref[j], acc], axis=0)
    halo_ref[j] = acc[tm - PREP_HALO:, :]
    t = lax.rem(i * tm, tp) + lax.broadcasted_iota(jnp.int32, (tm, LANES), 0)
    for _ in range(tm // tp + 1):
        t = jnp.where(t >= tp, t - tp, t)
    pad = t < ROW0
    normed = j < n_qk_tiles
    scale = jnp.where(j < n_q_tiles, GDN_HEAD_DIM ** -0.5, 1.0).astype(F32)
    w = wc_ref[...]
    for g in range(tn // LANES):
        cols = slice(g * LANES, (g + 1) * LANES)
        y = xb[PREP_HALO:, cols] * w[GDN_CONV - 1:GDN_CONV, cols]
        for s_ in range(1, GDN_CONV):
            y = y + xb[PREP_HALO - s_:PREP_HALO - s_ + tm, cols] * w[GDN_CONV - 1 - s_:GDN_CONV - s_, cols]
        y = jnp.where(pad, 0.0, _silu(y))
        ss = jnp.sum(y * y, axis=-1, keepdims=True)
        y = y * jnp.where(normed, lax.rsqrt(ss + RMS_EPS) * scale, 1.0)
        o_ref[:, cols] = y.astype(BF16)


def _gdn_in_proj(x, nw, w, wg, w_conv, tp):
    m, d = x.shape
    n = w.shape[1]
    qk = GDN_QK_HEADS * GDN_HEAD_DIM
    tm = _pick(m, (1024, 768, 512, 384, 256, 192, 128, 64))
    tn = 1024
    kern = functools.partial(_gdn_in_proj_kernel, tm=tm, tn=tn, tp=tp, n_q_tiles=qk // tn,
                             n_qk_tiles=2 * qk // tn)
    return pl.pallas_call(
        kern,
        grid=(m // tm, n // tn),
        in_specs=[
            pl.BlockSpec((tm, d), lambda i, j: (i, 0)),
            pl.BlockSpec((1, d), lambda i, j: (0, 0)),
            pl.BlockSpec((d, tn), lambda i, j: (0, j)),
            pl.BlockSpec((d, LANES), lambda i, j: (0, 0)),
            pl.BlockSpec((GDN_CONV, tn), lambda i, j: (0, j)),
        ],
        out_specs=[
            pl.BlockSpec((tm, tn), lambda i, j: (i, j)),
            pl.BlockSpec((tm, LANES), lambda i, j: (i, 0)),
        ],
        out_shape=[
            jax.ShapeDtypeStruct((m, n), BF16),
            jax.ShapeDtypeStruct((m, LANES), F32),
        ],
        scratch_shapes=[pltpu.VMEM((tm, d), BF16), pltpu.VMEM((n // tn, PREP_HALO, tn), F32)],
        compiler_params=_params("arbitrary", "arbitrary"),
        name="gdn_in_proj",
    )(x, nw, w, wg, w_conv)


def _tri_masks():
    ii = lax.broadcasted_iota(jnp.int32, (CHUNK, CHUNK), 0)
    jj = lax.broadcasted_iota(jnp.int32, (CHUNK, CHUNK), 1)
    return ii, jj


def _gdn_gates_kernel(g_ref, gt_ref, alc_ref, dtc_ref, alr_ref, dtr_ref,
                      beta_ref, cumc_ref, cumr_ref, *, nc, hg):
    nh = GDN_V_HEADS
    ii, jj = _tri_masks()
    tril = (ii >= jj).astype(F32)
    triu = (ii <= jj).astype(F32)
    neg_a_c = -jnp.exp(alc_ref[...])
    neg_a_r = -jnp.exp(alr_ref[...])
    for c in range(nc):
        blk = g_ref[c * CHUNK:(c + 1) * CHUNK, :]
        beta = jax.nn.sigmoid(blk[:, 0:nh])
        g = neg_a_c * _softplus(blk[:, nh:2 * nh] + dtc_ref[...])
        gr = neg_a_r * _softplus(gt_ref[:, c * CHUNK:(c + 1) * CHUNK] + dtr_ref[...])
        if c == 0:
            rows = lax.broadcasted_iota(jnp.int32, (CHUNK, nh), 0)
            cols = lax.broadcasted_iota(jnp.int32, (nh, CHUNK), 1)
            beta = jnp.where(rows >= ROW0, beta, 0.0)
            g = jnp.where(rows >= ROW0, g, 0.0)
            gr = jnp.where(cols >= ROW0, gr, 0.0)
        cum = jnp.dot(tril, g, preferred_element_type=F32, precision=lax.Precision.HIGHEST)
        cumr = jnp.dot(gr, triu, preferred_element_type=F32, precision=lax.Precision.HIGHEST)
        cumr = jnp.concatenate([cumr[:nh // 2], cumr[nh // 2:]], axis=1)
        for q in range(nh // hg):
            beta_ref[q, c] = beta[:, q * hg:(q + 1) * hg]
            cumc_ref[q, c] = cum[:, q * hg:(q + 1) * hg]
            cumr_ref[q, c] = cumr[q * hg // 2:(q + 1) * hg // 2, :]


def _gdn_gates(gates, a_log, dt_bias, hg):
    b, tp, _ = gates.shape
    nc = tp // CHUNK
    nh = GDN_V_HEADS
    ng = nh // hg
    kern = functools.partial(_gdn_gates_kernel, nc=nc, hg=hg)
    perm = np.concatenate([np.arange(0, nh, 2), np.arange(1, nh, 2)])
    gates_t = jnp.swapaxes(gates[:, :, nh:2 * nh], 1, 2)[:, perm, :]
    col = jax.ShapeDtypeStruct((b, ng, nc, CHUNK, hg), F32)
    row = jax.ShapeDtypeStruct((b, ng, nc, hg // 2, 2 * CHUNK), F32)
    col_spec = pl.BlockSpec((None, ng, nc, CHUNK, hg), lambda i: (i, 0, 0, 0, 0))
    row_spec = pl.BlockSpec((None, ng, nc, hg // 2, 2 * CHUNK), lambda i: (i, 0, 0, 0, 0))
    small = lambda shape: pl.BlockSpec(shape, lambda i: (0, 0))
    return pl.pallas_call(
        kern,
        grid=(b,),
        in_specs=[
            pl.BlockSpec((None, tp, LANES), lambda i: (i, 0, 0)),
            pl.BlockSpec((None, nh, tp), lambda i: (i, 0, 0)),
            small((1, nh)), small((1, nh)), small((nh, 1)), small((nh, 1)),
        ],
        out_specs=[col_spec, col_spec, row_spec],
        out_shape=[col, col, row],
        compiler_params=_params("parallel"),
        name="gdn_gates",
    )(gates, gates_t, a_log.reshape(1, nh), dt_bias.reshape(1, nh),
      a_log[perm].reshape(nh, 1), dt_bias[perm].reshape(nh, 1))


def _gdn_chunk_kernel(q_ref, k_ref, v_ref, z_ref, beta_ref, cumc_ref, cumr_ref, wn_ref, y_ref,
                      s_ref, u_ref, wq_ref, kd_ref, at_ref, *, nc, hg, cps):
    hd = GDN_HEAD_DIM
    iw = lax.broadcasted_iota(jnp.int32, (CHUNK, hd), 0)
    lane = lax.broadcasted_iota(jnp.int32, (CHUNK, hd), 1)
    jw = lane & (CHUNK - 1)
    first = lane < CHUNK
    causal = iw >= jw
    strict = iw > jw
    eye = (iw == jw).astype(F32)
    blk = [(iw >> l) == (jw >> l) for l in range(1, CHUNK.bit_length())]
    ring = [None] + [(((iw >> l) ^ (jw >> l)) == 1) for l in range(1, CHUNK.bit_length() - 1)]
    wn = wn_ref[...]
    heads = range(hg)
    pairs = range(hg // 2)

    def bdiag(x):
        zero = jnp.zeros_like(x)
        return jnp.concatenate([jnp.where(first, x, zero), jnp.where(first, zero, x)], axis=0)

    rstack = bdiag

    def prepare(chunks):
        units = [(i, p) for i in range(len(chunks)) for p in pairs]
        r0 = [pl.multiple_of(c * CHUNK, CHUNK) for c in chunks]
        cumc = [cumc_ref[c] for c in chunks]
        betac = [beta_ref[c] for c in chunks]
        cumr = [cumr_ref[c] for c in chunks]
        q = {(i, p): q_ref[pl.ds(r0[i], CHUNK), p * hd:(p + 1) * hd] for i, p in units}
        k = {(i, p): k_ref[pl.ds(r0[i], CHUNK), p * hd:(p + 1) * hd] for i, p in units}
        qkk = {u: _dot_nt(jnp.concatenate([q[u], k[u]], axis=0),
                          jnp.concatenate([k[u], k[u]], axis=0)) for u in units}
        cc = {(i, h): jnp.broadcast_to(cumc[i][:, h:h + 1], (CHUNK, hd))
              for i in range(len(chunks)) for h in heads}
        bc = {(i, h): jnp.broadcast_to(betac[i][:, h:h + 1], (CHUNK, hd))
              for i in range(len(chunks)) for h in heads}
        ccw = {(i, p): jnp.where(first, cc[i, 2 * p], cc[i, 2 * p + 1]) for i, p in units}
        bcw = {(i, p): jnp.where(first, bc[i, 2 * p], bc[i, 2 * p + 1]) for i, p in units}
        decay = {(i, p): jnp.exp(jnp.where(causal, ccw[i, p] - cumr[i][p:p + 1, :], -jnp.inf))
                 for i, p in units}
        a = {u: jnp.where(strict, bcw[u] * qkk[u][CHUNK:] * decay[u], 0.0) for u in units}
        for i, p in units:
            at_ref[i, p] = rstack((qkk[i, p][:CHUNK] * decay[i, p]).astype(BF16))
        t = {u: eye - jnp.where(blk[0], a[u], 0.0) for u in units}
        for lvl in range(1, len(blk)):
            a_off = {u: bdiag(jnp.where(ring[lvl], a[u], 0.0).astype(BF16)) for u in units}
            tb = {u: t[u].astype(BF16) for u in units}
            x = {u: _dot(tb[u], a_off[u]).astype(BF16) for u in units}
            t = {u: t[u] - _dot(x[u], bdiag(tb[u])) for u in units}
        ec = {u: jnp.exp(cc[u]) for u in cc}
        rhs = {}
        for i, p in units:
            kf = k[i, p].astype(F32)
            halves = []
            for h in (2 * p, 2 * p + 1):
                v = v_ref[pl.ds(r0[i], CHUNK), h * hd:(h + 1) * hd].astype(F32)
                halves.append(jnp.concatenate([(v * bc[i, h]).astype(BF16),
                                               (kf * (bc[i, h] * ec[i, h])).astype(BF16)], axis=1))
            rhs[i, p] = jnp.concatenate(halves, axis=0)
        sol = {u: _dot(rstack(t[u].astype(BF16)), rhs[u]) for u in units}
        for i, p in units:
            qf = q[i, p].astype(F32)
            kf = k[i, p].astype(F32)
            for r in range(2):
                h = 2 * p + r
                sh = sol[i, p][r * CHUNK:(r + 1) * CHUNK]
                u_ref[i, h] = sh[:, :hd]
                wq_ref[i, h] = jnp.concatenate(
                    [sh[:, hd:].astype(BF16), (qf * ec[i, h]).astype(BF16)], axis=0)
                kd_ref[i, h] = (kf * jnp.exp(cc[i, h][CHUNK - 1:CHUNK, :] - cc[i, h])).astype(BF16)

    def recur(c, slot, s):
        r0 = pl.multiple_of(c * CHUNK, CHUNK)
        cumc = cumc_ref[c]
        ws = [_dot(wq_ref[slot, h], s[h].astype(BF16)) for h in heads]
        v_new = [(u_ref[slot, h] - ws[h][:CHUNK]).astype(BF16) for h in heads]
        upd = [_dot_tn(kd_ref[slot, h], v_new[h]) for h in heads]
        av = [_dot(at_ref[slot, p], jnp.concatenate([v_new[2 * p], v_new[2 * p + 1]], axis=0))
              for p in pairs]
        s_new = [s[h] * jnp.exp(cumc[CHUNK - 1:CHUNK, h:h + 1]) + upd[h] for h in heads]
        for h in heads:
            o = ws[h][CHUNK:] + av[h // 2][(h % 2) * CHUNK:(h % 2 + 1) * CHUNK]
            z = z_ref[pl.ds(r0, CHUNK), h * hd:(h + 1) * hd].astype(F32)
            y = _norm_rows(o, wn) * _silu(z)
            y_ref[pl.ds(r0, CHUNK), h * hd:(h + 1) * hd] = y.astype(BF16)
        return s_new

    s_ref[...] = jnp.zeros_like(s_ref)
    prepare([jnp.int32(i) for i in range(min(cps, nc))])

    def body(it, carry):
        c0 = it * cps
        s = [s_ref[h] for h in heads]
        for i in range(cps):
            s = recur(c0 + i, i, s)
        for h in heads:
            s_ref[h] = s[h]
        prepare([jnp.minimum(c0 + cps + i, nc - 1) for i in range(cps)])
        return carry

    lax.fori_loop(0, nc // cps, body, 0)
    s = [s_ref[h] for h in heads]
    for i in range(nc % cps):
        s = recur(jnp.int32(nc - nc % cps + i), i, s)


def _gdn_chunk(qkv, z, beta, cumc, cumr, w_norm, hg):
    b, tp, _ = qkv.shape
    nc = tp // CHUNK
    hd = GDN_HEAD_DIM
    ng = GDN_V_HEADS // hg
    wqk = hd * hg // 2
    wv = hd * hg
    qk_dim = GDN_QK_HEADS * hd
    v_dim = GDN_V_HEADS * hd
    cps = GDN_CHUNKS_PER_STEP
    kern = functools.partial(_gdn_chunk_kernel, nc=nc, hg=hg, cps=cps)
    col_spec = pl.BlockSpec((None, None, nc, CHUNK, hg), lambda i, j: (i, j, 0, 0, 0))
    row_spec = pl.BlockSpec((None, None, nc, hg // 2, 2 * CHUNK), lambda i, j: (i, j, 0, 0, 0))
    return pl.pallas_call(
        kern,
        grid=(b, ng),
        in_specs=[
            pl.BlockSpec((None, tp, wqk), lambda i, j: (i, 0, j)),
            pl.BlockSpec((None, tp, wqk), lambda i, j: (i, 0, qk_dim // wqk + j)),
            pl.BlockSpec((None, tp, wv), lambda i, j: (i, 0, 2 * qk_dim // wv + j)),
            pl.BlockSpec((None, tp, wv), lambda i, j: (i, 0, j)),
            col_spec, col_spec, row_spec,
            pl.BlockSpec((1, hd), lambda i, j: (0, 0)),
        ],
        out_specs=pl.BlockSpec((None, tp, wv), lambda i, j: (i, 0, j)),
        out_shape=jax.ShapeDtypeStruct((b, tp, v_dim), BF16),
        scratch_shapes=[
            pltpu.VMEM((hg, hd, hd), F32),
            pltpu.VMEM((cps, hg, CHUNK, hd), F32),
            pltpu.VMEM((cps, hg, 2 * CHUNK, hd), BF16),
            pltpu.VMEM((cps, hg, CHUNK, hd), BF16),
            pltpu.VMEM((cps, hg // 2, 2 * CHUNK, hd), BF16),
        ],
        compiler_params=_params("parallel", "arbitrary"),
        name="gdn_chunk",
    )(qkv, qkv, qkv, z, beta, cumc, cumr, w_norm.reshape(1, hd))


def _gdn_layer(h, nw, w_in, w_conv, a_log, dt_bias, w_norm, w_out):
    b, tp, d = h.shape
    m = b * tp
    conv_dim = 2 * GDN_QK_HEADS * GDN_HEAD_DIM + GDN_V_HEADS * GDN_HEAD_DIM
    main = conv_dim + GDN_V_HEADS * GDN_HEAD_DIM
    hg = GDN_HEADS_PER_STEP
    w_qkv = w_in[:, :conv_dim].astype(BF16)
    w_z = w_in[:, conv_dim:main].astype(BF16)
    w_gate = jnp.pad(w_in[:, main:], ((0, 0), (0, LANES - 2 * GDN_V_HEADS))).astype(BF16)
    hn = h.reshape(m, d)
    qkv, gates = _gdn_in_proj(hn, nw.reshape(1, d), w_qkv, w_gate, w_conv, tp)
    z = _norm_matmul_plain(hn, nw.reshape(1, d), w_z)
    gates = gates.reshape(b, tp, LANES)
    beta, cumc, cumr = _gdn_gates(gates, a_log, dt_bias, hg)
    y = _gdn_chunk(qkv.reshape(b, tp, conv_dim), z.reshape(b, tp, -1), beta, cumc, cumr, w_norm, hg)
    return y.reshape(m, -1), w_out.astype(BF16)


def _mlstm_gates_kernel(g_ref, gt_ref, bc_ref, br_ref, cumc_ref, rc_ref, rr_ref, *, nc, hg):
    nh = MLSTM_HEADS
    ii, jj = _tri_masks()
    tril = (ii >= jj).astype(F32)
    triu = (ii <= jj).astype(F32)

    def split(raw, axis):
        capped = GATE_SOFTCAP * jnp.tanh(raw / GATE_SOFTCAP)
        if axis == 1:
            i_pre, f_pre = capped[:, :nh], capped[:, nh:2 * nh]
        else:
            i_pre, f_pre = capped[:nh], capped[nh:2 * nh]
        return i_pre, -_softplus(-f_pre)

    for c in range(nc):
        i_c, lf_c = split(g_ref[c * CHUNK:(c + 1) * CHUNK, 0:2 * nh] + bc_ref[...], 1)
        i_r, lf_r = split(gt_ref[:, c * CHUNK:(c + 1) * CHUNK] + br_ref[...], 0)
        if c == 0:
            rows = lax.broadcasted_iota(jnp.int32, (CHUNK, nh), 0)
            cols = lax.broadcasted_iota(jnp.int32, (nh, CHUNK), 1)
            i_c = jnp.where(rows >= ROW0, i_c, -jnp.inf)
            lf_c = jnp.where(rows >= ROW0, lf_c, 0.0)
            i_r = jnp.where(cols >= ROW0, i_r, -jnp.inf)
            lf_r = jnp.where(cols >= ROW0, lf_r, 0.0)
        cum = jnp.dot(tril, lf_c, preferred_element_type=F32, precision=lax.Precision.HIGHEST)
        cumr = jnp.dot(lf_r, triu, preferred_element_type=F32, precision=lax.Precision.HIGHEST)
        rc = i_c - cum
        rr = i_r - cumr
        for q in range(nh // hg):
            cumc_ref[q, c] = cum[:, q * hg:(q + 1) * hg]
            rc_ref[q, c] = rc[:, q * hg:(q + 1) * hg]
            rr_ref[q, c] = rr[q * hg:(q + 1) * hg, :]


def _mlstm_gates(gates, gates_t, bias, hg):
    b, tp, _ = gates.shape
    nc = tp // CHUNK
    nh = MLSTM_HEADS
    ng = nh // hg
    kern = functools.partial(_mlstm_gates_kernel, nc=nc, hg=hg)
    col = jax.ShapeDtypeStruct((b, ng, nc, CHUNK, hg), F32)
    row = jax.ShapeDtypeStruct((b, ng, nc, hg, CHUNK), F32)
    col_spec = pl.BlockSpec((None, ng, nc, CHUNK, hg), lambda i: (i, 0, 0, 0, 0))
    row_spec = pl.BlockSpec((None, ng, nc, hg, CHUNK), lambda i: (i, 0, 0, 0, 0))
    return pl.pallas_call(
        kern,
        grid=(b,),
        in_specs=[
            pl.BlockSpec((None, tp, LANES), lambda i: (i, 0, 0)),
            pl.BlockSpec((None, 2 * nh, tp), lambda i: (i, 0, 0)),
            pl.BlockSpec((1, 2 * nh), lambda i: (0, 0)),
            pl.BlockSpec((2 * nh, 1), lambda i: (0, 0)),
        ],
        out_specs=[col_spec, col_spec, row_spec],
        out_shape=[col, col, row],
        compiler_params=_params("parallel"),
        name="mlstm_gates",
    )(gates, gates_t, bias.reshape(1, 2 * nh), bias.reshape(2 * nh, 1))


def _mlstm_chunk_kernel(q_ref, k_ref, v_ref, og_ref, cumc_ref, rc_ref, rr_ref, wn_ref, y_ref,
                        c_ref, m_ref, *, nc, hg, cps):
    dv = MLSTM_DV
    ii, jj = _tri_masks()
    causal = ii >= jj
    ones_col = jnp.ones((CHUNK, LANES), BF16)
    heads = range(hg)

    c_ref[...] = jnp.zeros_like(c_ref)
    m_ref[...] = jnp.zeros_like(m_ref)

    def step(chunks):
        n = len(chunks)
        units = [(i, h) for i in range(n) for h in heads]
        r0 = [pl.multiple_of(c * CHUNK, CHUNK) for c in chunks]
        cumc = [cumc_ref[c] for c in chunks]
        rcol = [rc_ref[c] for c in chunks]
        rrow = [rr_ref[c] for c in chunks]
        q = {(i, h): q_ref[pl.ds(r0[i], CHUNK), h * LANES:(h + 1) * LANES] for i, h in units}
        k = {(i, h): k_ref[pl.ds(r0[i], CHUNK), h * LANES:(h + 1) * LANES] for i, h in units}
        v_aug = {(i, h): jnp.concatenate(
            [v_ref[pl.ds(r0[i], CHUNK), h * dv:(h + 1) * dv], ones_col], axis=1) for i, h in units}
        qk = {u: _dot_nt(q[u], k[u]) for u in units}
        cc = {(i, h): jnp.broadcast_to(cumc[i][:, h:h + 1], (CHUNK, LANES)) for i, h in units}
        rc = {(i, h): jnp.broadcast_to(rcol[i][:, h:h + 1], (CHUNK, LANES)) for i, h in units}
        rr = {(i, h): rrow[i][h:h + 1, :] for i, h in units}
        rmax = {u: jnp.max(jnp.where(causal, rr[u], -jnp.inf), axis=-1, keepdims=True) for u in units}
        m_in, keep, kw = {}, {}, {}
        m = [m_ref[h][0:1, :] for h in heads]
        for i, h in units:
            m_in[i, h] = m[h]
            c_last = cc[i, h][CHUNK - 1:CHUNK, :]
            log_keep = c_last + m[h]
            m_new = jnp.maximum(log_keep, c_last + jnp.max(rc[i, h], axis=0, keepdims=True))
            keep[i, h] = jnp.exp(log_keep - m_new)
            w_end = jnp.exp(c_last + rc[i, h] - m_new)
            kw[i, h] = (k[i, h].astype(F32) * w_end).astype(BF16)
            m[h] = m_new
        for h in heads:
            m_ref[h] = jnp.broadcast_to(m[h], m_ref.shape[1:])
        upd = {u: _dot_tn(kw[u], v_aug[u]) for u in units}
        g, w_intra = {}, {}
        for u in units:
            g[u] = jnp.maximum(m_in[u], rmax[u])
            w_intra[u] = (jnp.exp(jnp.where(causal, rr[u] - g[u][:, :CHUNK], -jnp.inf))
                          * qk[u]).astype(BF16)
        wv = {u: _dot(w_intra[u], v_aug[u]) for u in units}
        state = [c_ref[h] for h in heads]
        for i in range(n):
            qc = [_dot(q[i, h], state[h].astype(BF16)) for h in heads]
            for h in heads:
                s_inter = jnp.exp(m_in[i, h] - g[i, h])
                tot = jnp.concatenate([s_inter, s_inter], axis=1) * qc[h] + wv[i, h]
                den = tot[:, dv:]
                inv = 1.0 / jnp.maximum(jnp.abs(den), jnp.exp(-(cc[i, h] + g[i, h])))
                hs = tot[:, :dv] * inv
                og = og_ref[pl.ds(r0[i], CHUNK), h * dv:(h + 1) * dv].astype(F32)
                y = _norm_rows(hs, wn_ref[:, h * dv:(h + 1) * dv]) * jax.nn.sigmoid(og)
                y_ref[pl.ds(r0[i], CHUNK), h * dv:(h + 1) * dv] = y.astype(BF16)
                state[h] = jnp.concatenate([keep[i, h], keep[i, h]], axis=1) * state[h] + upd[i, h]
        for h in heads:
            c_ref[h] = state[h]

    def body(it, carry):
        step([it * cps + i for i in range(cps)])
        return carry

    lax.fori_loop(0, nc // cps, body, 0)
    if nc % cps:
        step([jnp.int32(nc - nc % cps + i) for i in range(nc % cps)])


def _mlstm_chunk(proj, cumc, rc, rr, w_norm, hg):
    b, tp, _ = proj.shape
    nc = tp // CHUNK
    nh = MLSTM_HEADS
    ng = nh // hg
    wb = LANES * hg
    kern = functools.partial(_mlstm_chunk_kernel, nc=nc, hg=hg, cps=MLSTM_CHUNKS_PER_STEP)
    col_spec = pl.BlockSpec((None, None, nc, CHUNK, hg), lambda i, j: (i, j, 0, 0, 0))
    row_spec = pl.BlockSpec((None, None, nc, hg, CHUNK), lambda i, j: (i, j, 0, 0, 0))
    return pl.pallas_call(
        kern,
        grid=(b, ng),
        in_specs=[
            pl.BlockSpec((None, tp, wb), lambda i, j: (i, 0, j)),
            pl.BlockSpec((None, tp, wb), lambda i, j: (i, 0, ng + j)),
            pl.BlockSpec((None, tp, wb), lambda i, j: (i, 0, 2 * ng + j)),
            pl.BlockSpec((None, tp, wb), lambda i, j: (i, 0, 3 * ng + j)),
            col_spec, col_spec, row_spec,
            pl.BlockSpec((1, wb), lambda i, j: (0, j)),
        ],
        out_specs=pl.BlockSpec((None, tp, wb), lambda i, j: (i, 0, j)),
        out_shape=jax.ShapeDtypeStruct((b, tp, nh * MLSTM_DV), BF16),
        scratch_shapes=[
            pltpu.VMEM((hg, LANES, 2 * MLSTM_DV), F32),
            pltpu.VMEM((hg, 8, LANES), F32),
        ],
        compiler_params=_params("parallel", "arbitrary"),
        name="mlstm_chunk",
    )(proj, proj, proj, proj, cumc, rc, rr, w_norm.reshape(1, nh * MLSTM_DV))


def _pad_heads(w, nh, dh):
    d = w.shape[0]
    w = w.reshape(d, nh, dh)
    return jnp.pad(w, ((0, 0), (0, 0), (0, LANES - dh))).reshape(d, nh * LANES)


def _mlstm_layer(h, nw, w_in, gate_bias, w_norm, w_out):
    b, tp, d = h.shape
    m = b * tp
    nh = MLSTM_HEADS
    qk = nh * MLSTM_DQK
    vd = nh * MLSTM_DV
    hg = 4
    w_main = jnp.concatenate([
        _pad_heads(w_in[:, :qk], nh, MLSTM_DQK),
        _pad_heads(w_in[:, qk:2 * qk] * (MLSTM_DQK ** -0.5), nh, MLSTM_DQK),
        w_in[:, 2 * qk:2 * qk + 2 * vd],
    ], axis=1).astype(BF16)
    w_gate = jnp.pad(w_in[:, 2 * qk + 2 * vd:], ((0, 0), (0, LANES - 2 * nh))).astype(BF16)
    proj, gates = _norm_matmul(h.reshape(m, d), nw.reshape(1, d), w_main, w_gate)
    proj = proj.reshape(b, tp, -1)
    gates = gates.reshape(b, tp, LANES)
    gates_t = jnp.swapaxes(gates[:, :, :2 * nh], 1, 2)
    cumc, rc, rr = _mlstm_gates(gates, gates_t, gate_bias, hg)
    y = _mlstm_chunk(proj, cumc, rc, rr, w_norm, hg)
    return y.reshape(m, vd), w_out.astype(BF16)


def _mla_proj_kernel(h_ref, nw_ref, win_ref, qn_ref, wuq_ref, wuqs_ref, kvn_ref, wuk_ref, wuv_ref,
                     qtab_ref, ktab_ref, q_out, k_out, v_out):
    xn = _norm_rows(h_ref[...], nw_ref[...]).astype(BF16)
    c = _dot(xn, win_ref[...])
    cq = _norm_rows(c[:, :MLA_Q_RANK], qn_ref[...]).astype(BF16)
    lat = MLA_Q_RANK + MLA_KV_RANK
    ckv = _norm_rows(c[:, MLA_Q_RANK:lat], kvn_ref[...]).astype(BF16)
    kr = c[:, lat:lat + LANES]
    kr_swap = c[:, lat + LANES:]
    q = _dot(cq, wuq_ref[...])
    q_swap = _dot(cq, wuqs_ref[...])
    kn = _dot(ckv, wuk_ref[...])
    v_out[...] = _dot(ckv, wuv_ref[...]).astype(BF16)

    def head_norm_rope(x, x_swap, tab_ref):
        ms = jnp.sum(x * x, axis=-1, keepdims=True) * (1.0 / MLA_QK)
        return (x * tab_ref[0] + x_swap * tab_ref[1]) * lax.rsqrt(ms + RMS_EPS)

    for h in range(MLA_HEADS):
        sl = slice(h * LANES, (h + 1) * LANES)
        q_out[:, sl] = head_norm_rope(q[:, sl], q_swap[:, sl], qtab_ref).astype(BF16)
        k_out[:, sl] = head_norm_rope(kn[:, sl] + kr, kr_swap, ktab_ref).astype(BF16)


def _mla_proj(h, nw, w_in, q_norm, w_uq, w_uq_swap, kv_norm, w_uk, w_uv, qtab, ktab):
    b, tp, d = h.shape
    tt = _pick(tp, (704, 352, 192, 64))
    nh = MLA_HEADS
    full = lambda a: pl.BlockSpec(a.shape, lambda i, j: (0,) * a.ndim)
    tab = pl.BlockSpec((2, tt, LANES), lambda i, j: (0, j, 0))
    row = lambda n: pl.BlockSpec((None, tt, n), lambda i, j: (i, j, 0))
    return pl.pallas_call(
        _mla_proj_kernel,
        grid=(b, tp // tt),
        in_specs=[row(d), full(nw), full(w_in), full(q_norm), full(w_uq), full(w_uq_swap),
                  full(kv_norm), full(w_uk), full(w_uv), tab, tab],
        out_specs=[row(nh * LANES), row(nh * LANES), row(nh * MLA_V)],
        out_shape=[
            jax.ShapeDtypeStruct((b, tp, nh * LANES), BF16),
            jax.ShapeDtypeStruct((b, tp, nh * LANES), BF16),
            jax.ShapeDtypeStruct((b, tp, nh * MLA_V), BF16),
        ],
        compiler_params=_params("parallel", "arbitrary"),
        name="mla_proj",
    )(h, nw, w_in, q_norm, w_uq, w_uq_swap, kv_norm, w_uk, w_uv, qtab, ktab)


def _mla_attn_kernel(q_ref, k_ref, v_ref, o_ref, *, tq, nq):
    ri = lax.broadcasted_iota(jnp.int32, (tq, tq), 0)
    ci = lax.broadcasted_iota(jnp.int32, (tq, tq), 1)
    tri = jnp.where(ci <= ri, 0.0, NEG_BIG)
    tri0 = jnp.where(ci >= ROW0, tri, NEG_BIG)
    pad_row = jnp.where(lax.broadcasted_iota(jnp.int32, (1, max(nq - 1, 1) * tq), 1) >= ROW0, 0.0, NEG_BIG)
    lane = lax.broadcasted_iota(jnp.int32, (tq, LANES), 1)
    units = [(qi, r) for qi in range(nq) for r in range(2)]

    def scores(qi, r):
        lo = qi * tq
        hs = slice(r * LANES, (r + 1) * LANES)
        q = q_ref[lo:lo + tq, hs]
        s_diag = _dot_nt(q, k_ref[lo:lo + tq, hs]) + (tri if qi else tri0)
        s_main = _dot_nt(q, k_ref[0:lo, hs]) + pad_row[:, :lo] if qi else None
        return s_main, s_diag

    def attend(qi, s_main, s_diag):
        lo = qi * tq
        m = jnp.max(s_diag, axis=-1, keepdims=True)
        if qi:
            m = jnp.maximum(m, jnp.max(s_main, axis=-1, keepdims=True))
        p = jnp.exp(s_diag - m)
        l = jnp.sum(p, axis=-1, keepdims=True)
        o = _dot(p.astype(BF16), v_ref[lo:lo + tq, :])
        if qi:
            p = jnp.exp(s_main - m)
            l = l + jnp.sum(p, axis=-1, keepdims=True)
            o = o + _dot(p.astype(BF16), v_ref[0:lo, :])
        return o * (1.0 / l)

    nxt = scores(*units[0])
    outs = []
    for n, (qi, r) in enumerate(units):
        cur = nxt
        if n + 1 < len(units):
            nxt = scores(*units[n + 1])
        outs.append(attend(qi, *cur))
        if r == 1:
            o = jnp.where(lane < MLA_V, outs[0], outs[1])
            outs = []
            if qi == 0:
                o = jnp.where(lax.broadcasted_iota(jnp.int32, (tq, LANES), 0) >= ROW0, o, 0.0)
            o_ref[qi * tq:(qi + 1) * tq, :] = o.astype(BF16)


def _mla_attn(q, k, v):
    b, tp, _ = q.shape
    tq = _pick(tp, (192, 64))
    npair = MLA_HEADS // 2
    kern = functools.partial(_mla_attn_kernel, tq=tq, nq=tp // tq)
    return pl.pallas_call(
        kern,
        grid=(b, npair),
        in_specs=[
            pl.BlockSpec((None, tp, 2 * LANES), lambda i, j: (i, 0, j)),
            pl.BlockSpec((None, tp, 2 * LANES), lambda i, j: (i, 0, j)),
            pl.BlockSpec((None, tp, LANES), lambda i, j: (i, 0, j)),
        ],
        out_specs=pl.BlockSpec((None, tp, LANES), lambda i, j: (i, 0, j)),
        out_shape=jax.ShapeDtypeStruct((b, tp, MLA_HEADS * MLA_V), BF16),
        compiler_params=_params("parallel", "arbitrary"),
        name="mla_attn",
    )(q, k, v)


def _swap_rope_halves(a):
    half = MLA_ROPE // 2
    lo, hi = MLA_NOPE, MLA_NOPE + half
    return jnp.concatenate([a[..., :lo], a[..., hi:hi + half], a[..., lo:hi], a[..., hi + half:]], axis=-1)


def _rope_tables(tp, gain, scale):
    half = MLA_ROPE // 2
    pos = jnp.arange(tp, dtype=F32) - float(ROW0)
    inv_freq = ROPE_THETA ** (-jnp.arange(0, MLA_ROPE, 2, dtype=F32) / MLA_ROPE)
    ang = pos[:, None] * inv_freq[None, :]
    cos, sin = jnp.cos(ang), jnp.sin(ang)
    ones = jnp.ones((tp, MLA_NOPE), F32)
    z_nope = jnp.zeros((tp, MLA_NOPE), F32)
    z_tail = jnp.zeros((tp, LANES - MLA_QK), F32)
    cos_t = jnp.concatenate([ones, cos, cos, z_tail], axis=1)
    sin_t = jnp.concatenate([z_nope, -sin, sin, z_tail], axis=1)
    g = jnp.pad(gain, (0, LANES - MLA_QK)).reshape(1, LANES) * scale
    return jnp.stack([cos_t * g, sin_t * _swap_rope_halves(g)])


def _mla_layer(h, nw, w_in, q_norm, w_uq, kv_norm, w_ukv, q_head_norm, k_head_norm, w_out):
    b, tp, d = h.shape
    m = b * tp
    nh = MLA_HEADS
    lat = MLA_Q_RANK + MLA_KV_RANK
    w_kr = jnp.concatenate([jnp.zeros((d, MLA_NOPE), F32), w_in[:, lat:],
                            jnp.zeros((d, LANES - MLA_QK), F32)], axis=1)
    w_in_p = jnp.concatenate([w_in[:, :lat], w_kr, _swap_rope_halves(w_kr)], axis=1).astype(BF16)
    w_uq_p = _pad_heads(w_uq, nh, MLA_QK)
    w_uq_swap = _swap_rope_halves(w_uq_p.reshape(MLA_Q_RANK, nh, LANES)).reshape(MLA_Q_RANK, nh * LANES)
    w_uq_p, w_uq_swap = w_uq_p.astype(BF16), w_uq_swap.astype(BF16)
    w_ukv3 = w_ukv.reshape(MLA_KV_RANK, nh, MLA_NOPE + MLA_V)
    w_uk_p = _pad_heads(w_ukv3[:, :, :MLA_NOPE].reshape(MLA_KV_RANK, nh * MLA_NOPE), nh, MLA_NOPE).astype(BF16)
    w_uv = w_ukv3[:, :, MLA_NOPE:].reshape(MLA_KV_RANK, nh * MLA_V).astype(BF16)
    q, k, v = _mla_proj(h, nw.reshape(1, d), w_in_p, q_norm.reshape(1, -1), w_uq_p, w_uq_swap,
                        kv_norm.reshape(1, -1), w_uk_p, w_uv,
                        _rope_tables(tp, q_head_norm, MLA_QK ** -0.5),
                        _rope_tables(tp, k_head_norm, 1.0))
    o = _mla_attn(q, k, v)
    return o.reshape(m, nh * MLA_V), w_out.astype(BF16)


def kernel(x, meta_tokens, attn_norm, ffn_norm, ff_up, ff_down, gdn_in, gdn_conv, gdn_a_log, gdn_dt_bias, gdn_norm, gdn_out, mlstm_in, mlstm_gate_bias, mlstm_norm, mlstm_out, mla_in, mla_q_norm, mla_uq, mla_kv_norm, mla_ukv, mla_q_head_norm, mla_k_head_norm, mla_out):
    b, t, d = x.shape
    depth = attn_norm.shape[0]
    meta = jnp.broadcast_to(meta_tokens[None].astype(x.dtype), (b, N_META, d))
    h = jnp.concatenate([jnp.zeros((b, LEAD_PAD, d), x.dtype), meta, x], axis=1)
    tp = h.shape[1]
    for layer in range(depth):
        kind, j = layer % N_MIXERS, layer // N_MIXERS
        if kind == 0:
            y, w_out = _gdn_layer(h, attn_norm[layer], gdn_in[j], gdn_conv[j], gdn_a_log[j],
                                  gdn_dt_bias[j], gdn_norm[j], gdn_out[j])
        elif kind == 1:
            y, w_out = _mlstm_layer(h, attn_norm[layer], mlstm_in[j], mlstm_gate_bias[j],
                                    mlstm_norm[j], mlstm_out[j])
        else:
            y, w_out = _mla_layer(h, attn_norm[layer], mla_in[j], mla_q_norm[j], mla_uq[j],
                                  mla_kv_norm[j], mla_ukv[j], mla_q_head_norm[j],
                                  mla_k_head_norm[j], mla_out[j])
        h = _out_mlp(h.reshape(b * tp, d), y, w_out, ffn_norm[layer].reshape(1, d),
                     ff_up[layer].astype(BF16), ff_down[layer].astype(BF16)).reshape(b, tp, d)
    return h[:, LEAD_PAD + N_META:]
```

```python
import functools

import jax
import jax.numpy as jnp
import numpy as np
from jax import lax
from jax.experimental import pallas as pl
from jax.experimental.pallas import tpu as pltpu

F32 = jnp.float32
BF16 = jnp.bfloat16

N_META = 16
CHUNK = 64
LEAD_PAD = (-N_META) % CHUNK
ROW0 = LEAD_PAD
RMS_EPS = 1e-6
N_MIXERS = 3

GDN_QK_HEADS = 8
GDN_V_HEADS = 16
GDN_HEAD_DIM = 128
GDN_CONV = 4
PREP_HALO = 16
GDN_HEADS_PER_STEP = 8
GDN_CHUNKS_PER_STEP = 3

MLSTM_HEADS = 8
MLSTM_DQK = 64
MLSTM_DV = 128
GATE_SOFTCAP = 15.0
MLSTM_HEADS_PER_STEP = 8
MLSTM_CHUNKS_PER_STEP = 2

MLA_HEADS = 16
MLA_NOPE = 64
MLA_ROPE = 32
MLA_QK = MLA_NOPE + MLA_ROPE
MLA_V = 64
MLA_Q_RANK = 384
MLA_KV_RANK = 256
ROPE_THETA = 10000.0

LANES = 128
SUBLANES = 8
NEG_BIG = -1e30
VMEM_LIMIT = 56 * 1024 * 1024


def _pick(n, candidates):
    for c in candidates:
        if n % c == 0:
            return c
    raise ValueError(f"no tile for {n} in {candidates}")


def _params(*sem):
    return pltpu.CompilerParams(dimension_semantics=sem, vmem_limit_bytes=VMEM_LIMIT)


def _norm_rows(x, w):
    ms = jnp.mean(x * x, axis=-1, keepdims=True)
    return x * lax.rsqrt(ms + RMS_EPS) * w


def _silu(x):
    half_x = 0.5 * x
    return half_x + half_x * jnp.tanh(half_x)


def _softplus(x):
    return jnp.maximum(x, 0.0) + jnp.log1p(jnp.exp(-jnp.abs(x)))


def _dot(a, b):
    return jnp.dot(a, b, preferred_element_type=F32)


def _dot_nt(a, b):
    return lax.dot_general(a, b, (((1,), (1,)), ((), ())), preferred_element_type=F32)


def _dot_tn(a, b):
    return lax.dot_general(a, b, (((0,), (0,)), ((), ())), preferred_element_type=F32)


def _norm_matmul_kernel(x_ref, nw_ref, w_ref, wg_ref, o_ref, g_ref, xn_ref):
    @pl.when(pl.program_id(1) == 0)
    def _():
        xn = _norm_rows(x_ref[...], nw_ref[...]).astype(BF16)
        xn_ref[...] = xn
        g_ref[...] = _dot(xn, wg_ref[...])

    o_ref[...] = _dot(xn_ref[...], w_ref[...]).astype(o_ref.dtype)


def _norm_matmul_plain_kernel(x_ref, nw_ref, w_ref, o_ref, xn_ref):
    @pl.when(pl.program_id(1) == 0)
    def _():
        xn_ref[...] = _norm_rows(x_ref[...], nw_ref[...]).astype(BF16)

    o_ref[...] = _dot(xn_ref[...], w_ref[...]).astype(o_ref.dtype)


def _norm_matmul_plain(x, nw, w):
    m, d = x.shape
    n = w.shape[1]
    tm = _pick(m, (1024, 768, 512, 384, 256, 192, 128, 64))
    tn = _pick(n, (2048, 1024, 512, 256, 128))
    return pl.pallas_call(
        _norm_matmul_plain_kernel,
        grid=(m // tm, n // tn),
        in_specs=[
            pl.BlockSpec((tm, d), lambda i, j: (i, 0)),
            pl.BlockSpec((1, d), lambda i, j: (0, 0)),
            pl.BlockSpec((d, tn), lambda i, j: (0, j)),
        ],
        out_specs=pl.BlockSpec((tm, tn), lambda i, j: (i, j)),
        out_shape=jax.ShapeDtypeStruct((m, n), BF16),
        scratch_shapes=[pltpu.VMEM((tm, d), BF16)],
        compiler_params=_params("parallel", "arbitrary"),
        name="norm_matmul_plain",
    )(x, nw, w)


def _norm_matmul(x, nw, w, wg):
    m, d = x.shape
    n = w.shape[1]
    tm = _pick(m, (1024, 768, 512, 384, 256, 192, 128, 64))
    tn = _pick(n, (3072, 2048, 1024, 512, 256, 128))
    return pl.pallas_call(
        _norm_matmul_kernel,
        grid=(m // tm, n // tn),
        in_specs=[
            pl.BlockSpec((tm, d), lambda i, j: (i, 0)),
            pl.BlockSpec((1, d), lambda i, j: (0, 0)),
            pl.BlockSpec((d, tn), lambda i, j: (0, j)),
            pl.BlockSpec((d, LANES), lambda i, j: (0, 0)),
        ],
        out_specs=[
            pl.BlockSpec((tm, tn), lambda i, j: (i, j)),
            pl.BlockSpec((tm, LANES), lambda i, j: (i, 0)),
        ],
        out_shape=[
            jax.ShapeDtypeStruct((m, n), BF16),
            jax.ShapeDtypeStruct((m, LANES), F32),
        ],
        scratch_shapes=[pltpu.VMEM((tm, d), BF16)],
        compiler_params=_params("parallel", "arbitrary"),
        name="norm_matmul",
    )(x, nw, w, wg)


def _out_mlp_kernel(h_ref, y_ref, wo_ref, nw_ref, wu_ref, wd_ref, o_ref, xn_ref):
    j = pl.program_id(1)

    @pl.when(j == 0)
    def _():
        h = h_ref[...] + _dot(y_ref[...], wo_ref[...])
        xn_ref[...] = _norm_rows(h, nw_ref[...]).astype(BF16)
        o_ref[...] = h

    a = jnp.maximum(_dot(xn_ref[...], wu_ref[...]), 0.0)
    o_ref[...] += _dot((a * a).astype(BF16), wd_ref[...])


def _out_mlp(h, y, wo, nw, wu, wd):
    m, d = h.shape
    k = y.shape[1]
    f = wu.shape[1]
    tm = _pick(m, (1024, 768, 512, 384, 256, 192, 128, 64))
    tf = _pick(f, (1024, 512, 256, 128))
    return pl.pallas_call(
        _out_mlp_kernel,
        grid=(m // tm, f // tf),
        in_specs=[
            pl.BlockSpec((tm, d), lambda i, j: (i, 0)),
            pl.BlockSpec((tm, k), lambda i, j: (i, 0)),
            pl.BlockSpec((k, d), lambda i, j: (0, 0)),
            pl.BlockSpec((1, d), lambda i, j: (0, 0)),
            pl.BlockSpec((d, tf), lambda i, j: (0, j)),
            pl.BlockSpec((tf, d), lambda i, j: (j, 0)),
        ],
        out_specs=pl.BlockSpec((tm, d), lambda i, j: (i, 0)),
        out_shape=jax.ShapeDtypeStruct((m, d), F32),
        scratch_shapes=[pltpu.VMEM((tm, d), BF16)],
        compiler_params=_params("parallel", "arbitrary"),
        name="out_mlp",
    )(h, y, wo, nw, wu, wd)


def _gdn_in_proj_kernel(x_ref, nw_ref, w_ref, wg_ref, wc_ref, o_ref, g_ref, xn_ref, halo_ref, *,
                        tm, tn, tp, n_q_tiles, n_qk_tiles):
    i = pl.program_id(0)
    j = pl.program_id(1)

    @pl.when(j == 0)
    def _():
        xn = _norm_rows(x_ref[...], nw_ref[...]).astype(BF16)
        xn_ref[...] = xn
        g_ref[...] = _dot(xn, wg_ref[...])

    @pl.when(i == 0)
    def _():
        halo_ref[j] = jnp.zeros((PREP_HALO, tn), F32)

    acc = _dot(xn_ref[...], w_ref[...])
    xb = jnp.concatenate([halo_ref[j], acc], axis=0)
    halo_ref[j] = acc[tm - PREP_HALO:, :]
    t = lax.rem(i * tm, tp) + lax.broadcasted_iota(jnp.int32, (tm, LANES), 0)
    for _ in range(tm // tp + 1):
        t = jnp.where(t >= tp, t - tp, t)
    pad = t < ROW0
    normed = j < n_qk_tiles
    scale = jnp.where(j < n_q_tiles, GDN_HEAD_DIM ** -0.5, 1.0).astype(F32)
    w = wc_ref[...]
    for g in range(tn // LANES):
        cols = slice(g * LANES, (g + 1) * LANES)
        y = xb[PREP_HALO:, cols] * w[GDN_CONV - 1:GDN_CONV, cols]
        for s_ in range(1, GDN_CONV):
            y = y + xb[PREP_HALO - s_:PREP_HALO - s_ + tm, cols] * w[GDN_CONV - 1 - s_:GDN_CONV - s_, cols]
        y = jnp.where(pad, 0.0, _silu(y))
        ss = jnp.sum(y * y, axis=-1, keepdims=True)
        y = y * jnp.where(normed, lax.rsqrt(ss + RMS_EPS) * scale, 1.0)
        o_ref[:, cols] = y.astype(BF16)


def _gdn_in_proj(x, nw, w, wg, w_conv, tp):
    m, d = x.shape
    n = w.shape[1]
    qk = GDN_QK_HEADS * GDN_HEAD_DIM
    tm = _pick(m, (1024, 768, 512, 384, 256, 192, 128, 64))
    tn = 1024
    kern = functools.partial(_gdn_in_proj_kernel, tm=tm, tn=tn, tp=tp, n_q_tiles=qk // tn,
                             n_qk_tiles=2 * qk // tn)
    return pl.pallas_call(
        kern,
        grid=(m // tm, n // tn),
        in_specs=[
            pl.BlockSpec((tm, d), lambda i, j: (i, 0)),
            pl.BlockSpec((1, d), lambda i, j: (0, 0)),
            pl.BlockSpec((d, tn), lambda i, j: (0, j)),
            pl.BlockSpec((d, LANES), lambda i, j: (0, 0)),
            pl.BlockSpec((GDN_CONV, tn), lambda i, j: (0, j)),
        ],
        out_specs=[
            pl.BlockSpec((tm, tn), lambda i, j: (i, j)),
            pl.BlockSpec((tm, LANES), lambda i, j: (i, 0)),
        ],
        out_shape=[
            jax.ShapeDtypeStruct((m, n), BF16),
            jax.ShapeDtypeStruct((m, LANES), F32),
        ],
        scratch_shapes=[pltpu.VMEM((tm, d), BF16), pltpu.VMEM((n // tn, PREP_HALO, tn), F32)],
        compiler_params=_params("arbitrary", "arbitrary"),
        name="gdn_in_proj",
    )(x, nw, w, wg, w_conv)


def _tri_masks():
    ii = lax.broadcasted_iota(jnp.int32, (CHUNK, CHUNK), 0)
    jj = lax.broadcasted_iota(jnp.int32, (CHUNK, CHUNK), 1)
    return ii, jj


def _gdn_gates_kernel(g_ref, gt_ref, alc_ref, dtc_ref, alr_ref, dtr_ref,
                      beta_ref, cumc_ref, cumr_ref, *, nc, hg):
    nh = GDN_V_HEADS
    ii, jj = _tri_masks()
    tril = (ii >= jj).astype(F32)
    triu = (ii <= jj).astype(F32)
    neg_a_c = -jnp.exp(alc_ref[...])
    neg_a_r = -jnp.exp(alr_ref[...])
    for c in range(nc):
        blk = g_ref[c * CHUNK:(c + 1) * CHUNK, :]
        beta = jax.nn.sigmoid(blk[:, 0:nh])
        g = neg_a_c * _softplus(blk[:, nh:2 * nh] + dtc_ref[...])
        gr = neg_a_r * _softplus(gt_ref[:, c * CHUNK:(c + 1) * CHUNK] + dtr_ref[...])
        if c == 0:
            rows = lax.broadcasted_iota(jnp.int32, (CHUNK, nh), 0)
            cols = lax.broadcasted_iota(jnp.int32, (nh, CHUNK), 1)
            beta = jnp.where(rows >= ROW0, beta, 0.0)
            g = jnp.where(rows >= ROW0, g, 0.0)
            gr = jnp.where(cols >= ROW0, gr, 0.0)
        cum = jnp.dot(tril, g, preferred_element_type=F32, precision=lax.Precision.HIGHEST)
        cumr = jnp.dot(gr, triu, preferred_element_type=F32, precision=lax.Precision.HIGHEST)
        cumr = jnp.concatenate([cumr[:nh // 2], cumr[nh // 2:]], axis=1)
        for q in range(nh // hg):
            beta_ref[q, c] = beta[:, q * hg:(q + 1) * hg]
            cumc_ref[q, c] = cum[:, q * hg:(q + 1) * hg]
            cumr_ref[q, c] = cumr[q * hg // 2:(q + 1) * hg // 2, :]


def _gdn_gates(gates, a_log, dt_bias, hg):
    b, tp, _ = gates.shape
    nc = tp // CHUNK
    nh = GDN_V_HEADS
    ng = nh // hg
    kern = functools.partial(_gdn_gates_kernel, nc=nc, hg=hg)
    perm = np.concatenate([np.arange(0, nh, 2), np.arange(1, nh, 2)])
    gates_t = jnp.swapaxes(gates[:, :, nh:2 * nh], 1, 2)[:, perm, :]
    col = jax.ShapeDtypeStruct((b, ng, nc, CHUNK, hg), F32)
    row = jax.ShapeDtypeStruct((b, ng, nc, hg // 2, 2 * CHUNK), F32)
    col_spec = pl.BlockSpec((None, ng, nc, CHUNK, hg), lambda i: (i, 0, 0, 0, 0))
    row_spec = pl.BlockSpec((None, ng, nc, hg // 2, 2 * CHUNK), lambda i: (i, 0, 0, 0, 0))
    small = lambda shape: pl.BlockSpec(shape, lambda i: (0, 0))
    return pl.pallas_call(
        kern,
        grid=(b,),
        in_specs=[
            pl.BlockSpec((None, tp, LANES), lambda i: (i, 0, 0)),
            pl.BlockSpec((None, nh, tp), lambda i: (i, 0, 0)),
            small((1, nh)), small((1, nh)), small((nh, 1)), small((nh, 1)),
        ],
        out_specs=[col_spec, col_spec, row_spec],
        out_shape=[col, col, row],
        compiler_params=_params("parallel"),
        name="gdn_gates",
    )(gates, gates_t, a_log.reshape(1, nh), dt_bias.reshape(1, nh),
      a_log[perm].reshape(nh, 1), dt_bias[perm].reshape(nh, 1))


def _gdn_chunk_kernel(q_ref, k_ref, v_ref, z_ref, beta_ref, cumc_ref, cumr_ref, wn_ref, y_ref,
                      s_ref, u_ref, wq_ref, kd_ref, at_ref, *, nc, hg, cps):
    hd = GDN_HEAD_DIM
    iw = lax.broadcasted_iota(jnp.int32, (CHUNK, hd), 0)
    lane = lax.broadcasted_iota(jnp.int32, (CHUNK, hd), 1)
    jw = lane & (CHUNK - 1)
    first = lane < CHUNK
    causal = iw >= jw
    strict = iw > jw
    eye = (iw == jw).astype(F32)
    blk = [(iw >> l) == (jw >> l) for l in range(1, CHUNK.bit_length())]
    ring = [None] + [(((iw >> l) ^ (jw >> l)) == 1) for l in range(1, CHUNK.bit_length() - 1)]
    wn = wn_ref[...]
    heads = range(hg)
    pairs = range(hg // 2)

    def bdiag(x):
        zero = jnp.zeros_like(x)
        return jnp.concatenate([jnp.where(first, x, zero), jnp.where(first, zero, x)], axis=0)

    rstack = bdiag

    def prepare(chunks):
        units = [(i, p) for i in range(len(chunks)) for p in pairs]
        r0 = [pl.multiple_of(c * CHUNK, CHUNK) for c in chunks]
        cumc = [cumc_ref[c] for c in chunks]
        betac = [beta_ref[c] for c in chunks]
        cumr = [cumr_ref[c] for c in chunks]
        q = {(i, p): q_ref[pl.ds(r0[i], CHUNK), p * hd:(p + 1) * hd] for i, p in units}
        k = {(i, p): k_ref[pl.ds(r0[i], CHUNK), p * hd:(p + 1) * hd] for i, p in units}
        qkk = {u: _dot_nt(jnp.concatenate([q[u], k[u]], axis=0),
                          jnp.concatenate([k[u], k[u]], axis=0)) for u in units}
        cc = {(i, h): jnp.broadcast_to(cumc[i][:, h:h + 1], (CHUNK, hd))
              for i in range(len(chunks)) for h in heads}
        bc = {(i, h): jnp.broadcast_to(betac[i][:, h:h + 1], (CHUNK, hd))
              for i in range(len(chunks)) for h in heads}
        ccw = {(i, p): jnp.where(first, cc[i, 2 * p], cc[i, 2 * p + 1]) for i, p in units}
        bcw = {(i, p): jnp.where(first, bc[i, 2 * p], bc[i, 2 * p + 1]) for i, p in units}
        decay = {(i, p): jnp.exp(jnp.where(causal, ccw[i, p] - cumr[i][p:p + 1, :], -jnp.inf))
                 for i, p in units}
        a = {u: jnp.where(strict, bcw[u] * qkk[u][CHUNK:] * decay[u], 0.0) for u in units}
        for i, p in units:
            at_ref[i, p] = rstack((qkk[i, p][:CHUNK] * decay[i, p]).astype(BF16))
        t = {u: eye - jnp.where(blk[0], a[u], 0.0) for u in units}
        for lvl in range(1, len(blk)):
            a_off = {u: bdiag(jnp.where(ring[lvl], a[u], 0.0).astype(BF16)) for u in units}
            tb = {u: t[u].astype(BF16) for u in units}
            x = {u: _dot(tb[u], a_off[u]).astype(BF16) for u in units}
            t = {u: t[u] - _dot(x[u], bdiag(tb[u])) for u in units}
        ec = {u: jnp.exp(cc[u]) for u in cc}
        rhs = {}
        for i, p in units:
            kf = k[i, p].astype(F32)
            halves = []
            for h in (2 * p, 2 * p + 1):
                v = v_ref[pl.ds(r0[i], CHUNK), h * hd:(h + 1) * hd].astype(F32)
                halves.append(jnp.concatenate([(v * bc[i, h]).astype(BF16),
                                               (kf * (bc[i, h] * ec[i, h])).astype(BF16)], axis=1))
            rhs[i, p] = jnp.concatenate(halves, axis=0)
        sol = {u: _dot(rstack(t[u].astype(BF16)), rhs[u]) for u in units}
        for i, p in units:
            qf = q[i, p].astype(F32)
            kf = k[i, p].astype(F32)
            for r in range(2):
                h = 2 * p + r
                sh = sol[i, p][r * CHUNK:(r + 1) * CHUNK]
                u_ref[i, h] = sh[:, :hd]
                wq_ref[i, h] = jnp.concatenate(
                    [sh[:, hd:].astype(BF16), (qf * ec[i, h]).astype(BF16)], axis=0)
                kd_ref[i, h] = (kf * jnp.exp(cc[i, h][CHUNK - 1:CHUNK, :] - cc[i, h])).astype(BF16)

    def recur(c, slot, s):
        r0 = pl.multiple_of(c * CHUNK, CHUNK)
        cumc = cumc_ref[c]
        ws = [_dot(wq_ref[slot, h], s[h].astype(BF16)) for h in heads]
        v_new = [(u_ref[slot, h] - ws[h][:CHUNK]).astype(BF16) for h in heads]
        upd = [_dot_tn(kd_ref[slot, h], v_new[h]) for h in heads]
        av = [_dot(at_ref[slot, p], jnp.concatenate([v_new[2 * p], v_new[2 * p + 1]], axis=0))
              for p in pairs]
        s_new = [s[h] * jnp.exp(cumc[CHUNK - 1:CHUNK, h:h + 1]) + upd[h] for h in heads]
        for h in heads:
            o = ws[h][CHUNK:] + av[h // 2][(h % 2) * CHUNK:(h % 2 + 1) * CHUNK]
            z = z_ref[pl.ds(r0, CHUNK), h * hd:(h + 1) * hd].astype(F32)
            y = _norm_rows(o, wn) * _silu(z)
            y_ref[pl.ds(r0, CHUNK), h * hd:(h + 1) * hd] = y.astype(BF16)
        return s_new

    s_ref[...] = jnp.zeros_like(s_ref)
    prepare([jnp.int32(i) for i in range(min(cps, nc))])

    def body(it, carry):
        c0 = it * cps
        s = [s_ref[h] for h in heads]
        for i in range(cps):
            s = recur(c0 + i, i, s)
        for h in heads:
            s_ref[h] = s[h]
        prepare([jnp.minimum(c0 + cps + i, nc - 1) for i in range(cps)])
        return carry

    lax.fori_loop(0, nc // cps, body, 0)
    s = [s_ref[h] for h in heads]
    for i in range(nc % cps):
        s = recur(jnp.int32(nc - nc % cps + i), i, s)


def _gdn_chunk(qkv, z, beta, cumc, cumr, w_norm, hg):
    b, tp, _ = qkv.shape
    nc = tp // CHUNK
    hd = GDN_HEAD_DIM
    ng = GDN_V_HEADS // hg
    wqk = hd * hg // 2
    wv = hd * hg
    qk_dim = GDN_QK_HEADS * hd
    v_dim = GDN_V_HEADS * hd
    cps = GDN_CHUNKS_PER_STEP
    kern = functools.partial(_gdn_chunk_kernel, nc=nc, hg=hg, cps=cps)
    col_spec = pl.BlockSpec((None, None, nc, CHUNK, hg), lambda i, j: (i, j, 0, 0, 0))
    row_spec = pl.BlockSpec((None, None, nc, hg // 2, 2 * CHUNK), lambda i, j: (i, j, 0, 0, 0))
    return pl.pallas_call(
        kern,
        grid=(b, ng),
        in_specs=[
            pl.BlockSpec((None, tp, wqk), lambda i, j: (i, 0, j)),
            pl.BlockSpec((None, tp, wqk), lambda i, j: (i, 0, qk_dim // wqk + j)),
            pl.BlockSpec((None, tp, wv), lambda i, j: (i, 0, 2 * qk_dim // wv + j)),
            pl.BlockSpec((None, tp, wv), lambda i, j: (i, 0, j)),
            col_spec, col_spec, row_spec,
            pl.BlockSpec((1, hd), lambda i, j: (0, 0)),
        ],
        out_specs=pl.BlockSpec((None, tp, wv), lambda i, j: (i, 0, j)),
        out_shape=jax.ShapeDtypeStruct((b, tp, v_dim), BF16),
        scratch_shapes=[
            pltpu.VMEM((hg, hd, hd), F32),
            pltpu.VMEM((cps, hg, CHUNK, hd), F32),
            pltpu.VMEM((cps, hg, 2 * CHUNK, hd), BF16),
            pltpu.VMEM((cps, hg, CHUNK, hd), BF16),
            pltpu.VMEM((cps, hg // 2, 2 * CHUNK, hd), BF16),
        ],
        compiler_params=_params("parallel", "arbitrary"),
        name="gdn_chunk",
    )(qkv, qkv, qkv, z, beta, cumc, cumr, w_norm.reshape(1, hd))


def _gdn_layer(h, nw, w_in, w_conv, a_log, dt_bias, w_norm, w_out):
    b, tp, d = h.shape
    m = b * tp
    conv_dim = 2 * GDN_QK_HEADS * GDN_HEAD_DIM + GDN_V_HEADS * GDN_HEAD_DIM
    main = conv_dim + GDN_V_HEADS * GDN_HEAD_DIM
    hg = GDN_HEADS_PER_STEP
    w_qkv = w_in[:, :conv_dim].astype(BF16)
    w_z = w_in[:, conv_dim:main].astype(BF16)
    w_gate = jnp.pad(w_in[:, main:], ((0, 0), (0, LANES - 2 * GDN_V_HEADS))).astype(BF16)
    hn = h.reshape(m, d)
    qkv, gates = _gdn_in_proj(hn, nw.reshape(1, d), w_qkv, w_gate, w_conv, tp)
    z = _norm_matmul_plain(hn, nw.reshape(1, d), w_z)
    gates = gates.reshape(b, tp, LANES)
    beta, cumc, cumr = _gdn_gates(gates, a_log, dt_bias, hg)
    y = _gdn_chunk(qkv.reshape(b, tp, conv_dim), z.reshape(b, tp, -1), beta, cumc, cumr, w_norm, hg)
    return y.reshape(m, -1), w_out.astype(BF16)


def _mlstm_gates_kernel(g_ref, gt_ref, bc_ref, br_ref, cumc_ref, rc_ref, rr_ref, *, nc, hg):
    nh = MLSTM_HEADS
    ii, jj = _tri_masks()
    tril = (ii >= jj).astype(F32)
    triu = (ii <= jj).astype(F32)

    def split(raw, axis):
        capped = GATE_SOFTCAP * jnp.tanh(raw / GATE_SOFTCAP)
        if axis == 1:
            i_pre, f_pre = capped[:, :nh], capped[:, nh:2 * nh]
        else:
            i_pre, f_pre = capped[:nh], capped[nh:2 * nh]
        return i_pre, -_softplus(-f_pre)

    for c in range(nc):
        i_c, lf_c = split(g_ref[c * CHUNK:(c + 1) * CHUNK, 0:2 * nh] + bc_ref[...], 1)
        i_r, lf_r = split(gt_ref[:, c * CHUNK:(c + 1) * CHUNK] + br_ref[...], 0)
        if c == 0:
            rows = lax.broadcasted_iota(jnp.int32, (CHUNK, nh), 0)
            cols = lax.broadcasted_iota(jnp.int32, (nh, CHUNK), 1)
            i_c = jnp.where(rows >= ROW0, i_c, -jnp.inf)
            lf_c = jnp.where(rows >= ROW0, lf_c, 0.0)
            i_r = jnp.where(cols >= ROW0, i_r, -jnp.inf)
            lf_r = jnp.where(cols >= ROW0, lf_r, 0.0)
        cum = jnp.dot(tril, lf_c, preferred_element_type=F32, precision=lax.Precision.HIGHEST)
        cumr = jnp.dot(lf_r, triu, preferred_element_type=F32, precision=lax.Precision.HIGHEST)
        rc = i_c - cum
        rr = i_r - cumr
        for q in range(nh // hg):
            cumc_ref[q, c] = cum[:, q * hg:(q + 1) * hg]
            rc_ref[q, c] = rc[:, q * hg:(q + 1) * hg]
            rr_ref[q, c] = rr[q * hg:(q + 1) * hg, :]


def _mlstm_gates(gates, gates_t, bias, hg):
    b, tp, _ = gates.shape
    nc = tp // CHUNK
    nh = MLSTM_HEADS
    ng = nh // hg
    kern = functools.partial(_mlstm_gates_kernel, nc=nc, hg=hg)
    col = jax.ShapeDtypeStruct((b, ng, nc, CHUNK, hg), F32)
    row = jax.ShapeDtypeStruct((b, ng, nc, hg, CHUNK), F32)
    col_spec = pl.BlockSpec((None, ng, nc, CHUNK, hg), lambda i: (i, 0, 0, 0, 0))
    row_spec = pl.BlockSpec((None, ng, nc, hg, CHUNK), lambda i: (i, 0, 0, 0, 0))
    return pl.pallas_call(
        kern,
        grid=(b,),
        in_specs=[
            pl.BlockSpec((None, tp, LANES), lambda i: (i, 0, 0)),
            pl.BlockSpec((None, 2 * nh, tp), lambda i: (i, 0, 0)),
            pl.BlockSpec((1, 2 * nh), lambda i: (0, 0)),
            pl.BlockSpec((2 * nh, 1), lambda i: (0, 0)),
        ],
        out_specs=[col_spec, col_spec, row_spec],
        out_shape=[col, col, row],
        compiler_params=_params("parallel"),
        name="mlstm_gates",
    )(gates, gates_t, bias.reshape(1, 2 * nh), bias.reshape(2 * nh, 1))


def _mlstm_chunk_kernel(q_ref, k_ref, v_ref, og_ref, cumc_ref, rc_ref, rr_ref, wn_ref, y_ref,
                        c_ref, m_ref, *, nc, hg, cps):
    dv = MLSTM_DV
    ii, jj = _tri_masks()
    causal = ii >= jj
    ones_col = jnp.ones((CHUNK, LANES), BF16)
    heads = range(hg)

    c_ref[...] = jnp.zeros_like(c_ref)
    m_ref[...] = jnp.zeros_like(m_ref)

    def step(chunks):
        n = len(chunks)
        units = [(i, h) for i in range(n) for h in heads]
        r0 = [pl.multiple_of(c * CHUNK, CHUNK) for c in chunks]
        cumc = [cumc_ref[c] for c in chunks]
        rcol = [rc_ref[c] for c in chunks]
        rrow = [rr_ref[c] for c in chunks]
        q = {(i, h): q_ref[pl.ds(r0[i], CHUNK), h * LANES:(h + 1) * LANES] for i, h in units}
        k = {(i, h): k_ref[pl.ds(r0[i], CHUNK), h * LANES:(h + 1) * LANES] for i, h in units}
        v_aug = {(i, h): jnp.concatenate(
            [v_ref[pl.ds(r0[i], CHUNK), h * dv:(h + 1) * dv], ones_col], axis=1) for i, h in units}
        qk = {u: _dot_nt(q[u], k[u]) for u in units}
        cc = {(i, h): jnp.broadcast_to(cumc[i][:, h:h + 1], (CHUNK, LANES)) for i, h in units}
        rc = {(i, h): jnp.broadcast_to(rcol[i][:, h:h + 1], (CHUNK, LANES)) for i, h in units}
        rr = {(i, h): rrow[i][h:h + 1, :] for i, h in units}
        rmax = {u: jnp.max(jnp.where(causal, rr[u], -jnp.inf), axis=-1, keepdims=True) for u in units}
        m_in, keep, kw = {}, {}, {}
        m = [m_ref[h][0:1, :] for h in heads]
        for i, h in units:
            m_in[i, h] = m[h]
            c_last = cc[i, h][CHUNK - 1:CHUNK, :]
            log_keep = c_last + m[h]
            m_new = jnp.maximum(log_keep, c_last + jnp.max(rc[i, h], axis=0, keepdims=True))
            keep[i, h] = jnp.exp(log_keep - m_new)
            w_end = jnp.exp(c_last + rc[i, h] - m_new)
            kw[i, h] = (k[i, h].astype(F32) * w_end).astype(BF16)
            m[h] = m_new
        for h in heads:
            m_ref[h] = jnp.broadcast_to(m[h], m_ref.shape[1:])
        upd = {u: _dot_tn(kw[u], v_aug[u]) for u in units}
        g, w_intra = {}, {}
        for u in units:
            g[u] = jnp.maximum(m_in[u], rmax[u])
            w_intra[u] = (jnp.exp(jnp.where(causal, rr[u] - g[u][:, :CHUNK], -jnp.inf))
                          * qk[u]).astype(BF16)
        wv = {u: _dot(w_intra[u], v_aug[u]) for u in units}
        state = [c_ref[h] for h in heads]
        for i in range(n):
            qc = [_dot(q[i, h], state[h].astype(BF16)) for h in heads]
            for h in heads:
                s_inter = jnp.exp(m_in[i, h] - g[i, h])
                tot = jnp.concatenate([s_inter, s_inter], axis=1) * qc[h] + wv[i, h]
                den = tot[:, dv:]
                inv = 1.0 / jnp.maximum(jnp.abs(den), jnp.exp(-(cc[i, h] + g[i, h])))
                hs = tot[:, :dv] * inv
                og = og_ref[pl.ds(r0[i], CHUNK), h * dv:(h + 1) * dv].astype(F32)
                y = _norm_rows(hs, wn_ref[:, h * dv:(h + 1) * dv]) * jax.nn.sigmoid(og)
                y_ref[pl.ds(r0[i], CHUNK), h * dv:(h + 1) * dv] = y.astype(BF16)
                state[h] = jnp.concatenate([keep[i, h], keep[i, h]], axis=1) * state[h] + upd[i, h]
        for h in heads:
            c_ref[h] = state[h]

    def body(it, carry):
        step([it * cps + i for i in range(cps)])
        return carry

    lax.fori_loop(0, nc // cps, body, 0)
    if nc % cps:
        step([jnp.int32(nc - nc % cps + i) for i in range(nc % cps)])


def _mlstm_chunk(proj, cumc, rc, rr, w_norm, hg):
    b, tp, _ = proj.shape
    nc = tp // CHUNK
    nh = MLSTM_HEADS
    ng = nh // hg
    wb = LANES * hg
    kern = functools.partial(_mlstm_chunk_kernel, nc=nc, hg=hg, cps=MLSTM_CHUNKS_PER_STEP)
    col_spec = pl.BlockSpec((None, None, nc, CHUNK, hg), lambda i, j: (i, j, 0, 0, 0))
    row_spec = pl.BlockSpec((None, None, nc, hg, CHUNK), lambda i, j: (i, j, 0, 0, 0))
    return pl.pallas_call(
        kern,
        grid=(b, ng),
        in_specs=[
            pl.BlockSpec((None, tp, wb), lambda i, j: (i, 0, j)),
            pl.BlockSpec((None, tp, wb), lambda i, j: (i, 0, ng + j)),
            pl.BlockSpec((None, tp, wb), lambda i, j: (i, 0, 2 * ng + j)),
            pl.BlockSpec((None, tp, wb), lambda i, j: (i, 0, 3 * ng + j)),
            col_spec, col_spec, row_spec,
            pl.BlockSpec((1, wb), lambda i, j: (0, j)),
        ],
        out_specs=pl.BlockSpec((None, tp, wb), lambda i, j: (i, 0, j)),
        out_shape=jax.ShapeDtypeStruct((b, tp, nh * MLSTM_DV), BF16),
        scratch_shapes=[
            pltpu.VMEM((hg, LANES, 2 * MLSTM_DV), F32),
            pltpu.VMEM((hg, SUBLANES, LANES), F32),
        ],
        compiler_params=_params("parallel", "arbitrary"),
        name="mlstm_chunk",
    )(proj, proj, proj, proj, cumc, rc, rr, w_norm.reshape(1, nh * MLSTM_DV))


def _pad_heads(w, nh, dh):
    d = w.shape[0]
    w = w.reshape(d, nh, dh)
    return jnp.pad(w, ((0, 0), (0, 0), (0, LANES - dh))).reshape(d, nh * LANES)


def _mlstm_layer(h, nw, w_in, gate_bias, w_norm, w_out):
    b, tp, d = h.shape
    m = b * tp
    nh = MLSTM_HEADS
    qk = nh * MLSTM_DQK
    vd = nh * MLSTM_DV
    hg = MLSTM_HEADS_PER_STEP
    w_main = jnp.concatenate([
        _pad_heads(w_in[:, :qk], nh, MLSTM_DQK),
        _pad_heads(w_in[:, qk:2 * qk] * (MLSTM_DQK ** -0.5), nh, MLSTM_DQK),
        w_in[:, 2 * qk:2 * qk + 2 * vd],
    ], axis=1).astype(BF16)
    w_gate = jnp.pad(w_in[:, 2 * qk + 2 * vd:], ((0, 0), (0, LANES - 2 * nh))).astype(BF16)
    proj, gates = _norm_matmul(h.reshape(m, d), nw.reshape(1, d), w_main, w_gate)
    proj = proj.reshape(b, tp, -1)
    gates = gates.reshape(b, tp, LANES)
    gates_t = jnp.swapaxes(gates[:, :, :2 * nh], 1, 2)
    cumc, rc, rr = _mlstm_gates(gates, gates_t, gate_bias, hg)
    y = _mlstm_chunk(proj, cumc, rc, rr, w_norm, hg)
    return y.reshape(m, vd), w_out.astype(BF16)


def _mla_proj_kernel(h_ref, nw_ref, win_ref, qn_ref, wuq_ref, wuqs_ref, kvn_ref, wuk_ref, wuv_ref,
                     qtab_ref, ktab_ref, q_out, k_out, v_out):
    xn = _norm_rows(h_ref[...], nw_ref[...]).astype(BF16)
    c = _dot(xn, win_ref[...])
    cq = _norm_rows(c[:, :MLA_Q_RANK], qn_ref[...]).astype(BF16)
    lat = MLA_Q_RANK + MLA_KV_RANK
    ckv = _norm_rows(c[:, MLA_Q_RANK:lat], kvn_ref[...]).astype(BF16)
    kr = c[:, lat:lat + LANES]
    kr_swap = c[:, lat + LANES:]
    q = _dot(cq, wuq_ref[...])
    q_swap = _dot(cq, wuqs_ref[...])
    kn = _dot(ckv, wuk_ref[...])
    v_out[...] = _dot(ckv, wuv_ref[...]).astype(BF16)

    def head_norm_rope(x, x_swap, tab_ref):
        ms = jnp.sum(x * x, axis=-1, keepdims=True) * (1.0 / MLA_QK)
        return (x * tab_ref[0] + x_swap * tab_ref[1]) * lax.rsqrt(ms + RMS_EPS)

    for h in range(MLA_HEADS):
        sl = slice(h * LANES, (h + 1) * LANES)
        q_out[:, sl] = head_norm_rope(q[:, sl], q_swap[:, sl], qtab_ref).astype(BF16)
        k_out[:, sl] = head_norm_rope(kn[:, sl] + kr, kr_swap, ktab_ref).astype(BF16)


def _mla_proj(h, nw, w_in, q_norm, w_uq, w_uq_swap, kv_norm, w_uk, w_uv, qtab, ktab):
    b, tp, d = h.shape
    tt = _pick(tp, (704, 352, 192, 64))
    nh = MLA_HEADS
    full = lambda a: pl.BlockSpec(a.shape, lambda i, j: (0,) * a.ndim)
    tab = pl.BlockSpec((2, tt, LANES), lambda i, j: (0, j, 0))
    row = lambda n: pl.BlockSpec((None, tt, n), lambda i, j: (i, j, 0))
    return pl.pallas_call(
        _mla_proj_kernel,
        grid=(b, tp // tt),
        in_specs=[row(d), full(nw), full(w_in), full(q_norm), full(w_uq), full(w_uq_swap),
                  full(kv_norm), full(w_uk), full(w_uv), tab, tab],
        out_specs=[row(nh * LANES), row(nh * LANES), row(nh * MLA_V)],
        out_shape=[
            jax.ShapeDtypeStruct((b, tp, nh * LANES), BF16),
            jax.ShapeDtypeStruct((b, tp, nh * LANES), BF16),
            jax.ShapeDtypeStruct((b, tp, nh * MLA_V), BF16),
        ],
        compiler_params=_params("parallel", "arbitrary"),
        name="mla_proj",
    )(h, nw, w_in, q_norm, w_uq, w_uq_swap, kv_norm, w_uk, w_uv, qtab, ktab)


def _mla_attn_kernel(q_ref, k_ref, v_ref, o_ref, *, tq, nq):
    ri = lax.broadcasted_iota(jnp.int32, (tq, tq), 0)
    ci = lax.broadcasted_iota(jnp.int32, (tq, tq), 1)
    tri = jnp.where(ci <= ri, 0.0, NEG_BIG)
    tri0 = jnp.where(ci >= ROW0, tri, NEG_BIG)
    pad_row = jnp.where(lax.broadcasted_iota(jnp.int32, (1, max(nq - 1, 1) * tq), 1) >= ROW0, 0.0, NEG_BIG)
    lane = lax.broadcasted_iota(jnp.int32, (tq, LANES), 1)
    units = [(qi, r) for qi in range(nq) for r in range(2)]

    def scores(qi, r):
        lo = qi * tq
        hs = slice(r * LANES, (r + 1) * LANES)
        q = q_ref[lo:lo + tq, hs]
        s_diag = _dot_nt(q, k_ref[lo:lo + tq, hs]) + (tri if qi else tri0)
        s_main = _dot_nt(q, k_ref[0:lo, hs]) + pad_row[:, :lo] if qi else None
        return s_main, s_diag

    def attend(qi, s_main, s_diag):
        lo = qi * tq
        m = jnp.max(s_diag, axis=-1, keepdims=True)
        if qi:
            m = jnp.maximum(m, jnp.max(s_main, axis=-1, keepdims=True))
        p = jnp.exp(s_diag - m)
        l = jnp.sum(p, axis=-1, keepdims=True)
        o = _dot(p.astype(BF16), v_ref[lo:lo + tq, :])
        if qi:
            p = jnp.exp(s_main - m)
            l = l + jnp.sum(p, axis=-1, keepdims=True)
            o = o + _dot(p.astype(BF16), v_ref[0:lo, :])
        return o * (1.0 / l)

    nxt = scores(*units[0])
    outs = []
    for n, (qi, r) in enumerate(units):
        cur = nxt
        if n + 1 < len(units):
            nxt = scores(*units[n + 1])
        outs.append(attend(qi, *cur))
        if r == 1:
            o = jnp.where(lane < MLA_V, outs[0], outs[1])
            outs = []
            if qi == 0:
                o = jnp.where(lax.broadcasted_iota(jnp.int32, (tq, LANES), 0) >= ROW0, o, 0.0)
            o_ref[qi * tq:(qi + 1) * tq, :] = o.astype(BF16)


def _mla_attn(q, k, v):
    b, tp, _ = q.shape
    tq = _pick(tp, (192, 64))
    npair = MLA_HEADS // 2
    kern = functools.partial(_mla_attn_kernel, tq=tq, nq=tp // tq)
    return pl.pallas_call(
        kern,
        grid=(b, npair),
        in_specs=[
            pl.BlockSpec((None, tp, 2 * LANES), lambda i, j: (i, 0, j)),
            pl.BlockSpec((None, tp, 2 * LANES), lambda i, j: (i, 0, j)),
            pl.BlockSpec((None, tp, LANES), lambda i, j: (i, 0, j)),
        ],
        out_specs=pl.BlockSpec((None, tp, LANES), lambda i, j: (i, 0, j)),
        out_shape=jax.ShapeDtypeStruct((b, tp, MLA_HEADS * MLA_V), BF16),
        compiler_params=_params("parallel", "arbitrary"),
        name="mla_attn",
    )(q, k, v)


def _swap_rope_halves(a):
    half = MLA_ROPE // 2
    lo, hi = MLA_NOPE, MLA_NOPE + half
    return jnp.concatenate([a[..., :lo], a[..., hi:hi + half], a[..., lo:hi], a[..., hi + half:]], axis=-1)


def _rope_tables(tp, gain, scale):
    half = MLA_ROPE // 2
    pos = jnp.arange(tp, dtype=F32) - float(ROW0)
    inv_freq = ROPE_THETA ** (-jnp.arange(0, MLA_ROPE, 2, dtype=F32) / MLA_ROPE)
    ang = pos[:, None] * inv_freq[None, :]
    cos, sin = jnp.cos(ang), jnp.sin(ang)
    ones = jnp.ones((tp, MLA_NOPE), F32)
    z_nope = jnp.zeros((tp, MLA_NOPE), F32)
    z_tail = jnp.zeros((tp, LANES - MLA_QK), F32)
    cos_t = jnp.concatenate([ones, cos, cos, z_tail], axis=1)
    sin_t = jnp.concatenate([z_nope, -sin, sin, z_tail], axis=1)
    g = jnp.pad(gain, (0, LANES - MLA_QK)).reshape(1, LANES) * scale
    return jnp.stack([cos_t * g, sin_t * _swap_rope_halves(g)])


def _mla_layer(h, nw, w_in, q_norm, w_uq, kv_norm, w_ukv, q_head_norm, k_head_norm, w_out):
    b, tp, d = h.shape
    m = b * tp
    nh = MLA_HEADS
    lat = MLA_Q_RANK + MLA_KV_RANK
    w_kr = jnp.concatenate([jnp.zeros((d, MLA_NOPE), F32), w_in[:, lat:],
                            jnp.zeros((d, LANES - MLA_QK), F32)], axis=1)
    w_in_p = jnp.concatenate([w_in[:, :lat], w_kr, _swap_rope_halves(w_kr)], axis=1).astype(BF16)
    w_uq_p = _pad_heads(w_uq, nh, MLA_QK)
    w_uq_swap = _swap_rope_halves(w_uq_p.reshape(MLA_Q_RANK, nh, LANES)).reshape(MLA_Q_RANK, nh * LANES)
    w_uq_p, w_uq_swap = w_uq_p.astype(BF16), w_uq_swap.astype(BF16)
    w_ukv3 = w_ukv.reshape(MLA_KV_RANK, nh, MLA_NOPE + MLA_V)
    w_uk_p = _pad_heads(w_ukv3[:, :, :MLA_NOPE].reshape(MLA_KV_RANK, nh * MLA_NOPE), nh, MLA_NOPE).astype(BF16)
    w_uv = w_ukv3[:, :, MLA_NOPE:].reshape(MLA_KV_RANK, nh * MLA_V).astype(BF16)
    q, k, v = _mla_proj(h, nw.reshape(1, d), w_in_p, q_norm.reshape(1, -1), w_uq_p, w_uq_swap,
                        kv_norm.reshape(1, -1), w_uk_p, w_uv,
                        _rope_tables(tp, q_head_norm, MLA_QK ** -0.5),
                        _rope_tables(tp, k_head_norm, 1.0))
    o = _mla_attn(q, k, v)
    return o.reshape(m, nh * MLA_V), w_out.astype(BF16)


def kernel(x, meta_tokens, attn_norm, ffn_norm, ff_up, ff_down, gdn_in, gdn_conv, gdn_a_log, gdn_dt_bias, gdn_norm, gdn_out, mlstm_in, mlstm_gate_bias, mlstm_norm, mlstm_out, mla_in, mla_q_norm, mla_uq, mla_kv_norm, mla_ukv, mla_q_head_norm, mla_k_head_norm, mla_out):
    b, t, d = x.shape
    depth = attn_norm.shape[0]
    meta = jnp.broadcast_to(meta_tokens[None].astype(x.dtype), (b, N_META, d))
    h = jnp.concatenate([jnp.zeros((b, LEAD_PAD, d), x.dtype), meta, x], axis=1)
    tp = h.shape[1]
    for layer in range(depth):
        kind, j = layer % N_MIXERS, layer // N_MIXERS
        if kind == 0:
            y, w_out = _gdn_layer(h, attn_norm[layer], gdn_in[j], gdn_conv[j], gdn_a_log[j],
                                  gdn_dt_bias[j], gdn_norm[j], gdn_out[j])
        elif kind == 1:
            y, w_out = _mlstm_layer(h, attn_norm[layer], mlstm_in[j], mlstm_gate_bias[j],
                                    mlstm_norm[j], mlstm_out[j])
        else:
            y, w_out = _mla_layer(h, attn_norm[layer], mla_in[j], mla_q_norm[j], mla_uq[j],
                                  mla_kv_norm[j], mla_ukv[j], mla_q_head_norm[j],
                                  mla_k_head_norm[j], mla_out[j])
        h = _out_mlp(h.reshape(b * tp, d), y, w_out, ffn_norm[layer].reshape(1, d),
                     ff_up[layer].astype(BF16), ff_down[layer].astype(BF16)).reshape(b, tp, d)
    return h[:, LEAD_PAD + N_META:]
```

```python
import functools

import jax
import jax.numpy as jnp
import numpy as np
from jax import lax
from jax.experimental import pallas as pl
from jax.experimental.pallas import tpu as pltpu

F32 = jnp.float32
BF16 = jnp.bfloat16

N_META = 16
CHUNK = 64
LEAD_PAD = (-N_META) % CHUNK
ROW0 = LEAD_PAD
RMS_EPS = 1e-6
N_MIXERS = 3

GDN_QK_HEADS = 8
GDN_V_HEADS = 16
GDN_HEAD_DIM = 128
GDN_CONV = 4
PREP_HALO = 16
GDN_HEADS_PER_STEP = 8
GDN_CHUNKS_PER_STEP = 4

MLSTM_HEADS = 8
MLSTM_DQK = 64
MLSTM_DV = 128
GATE_SOFTCAP = 15.0
MLSTM_HEADS_PER_STEP = 8
MLSTM_CHUNKS_PER_STEP = 2

MLA_HEADS = 16
MLA_NOPE = 64
MLA_ROPE = 32
MLA_QK = MLA_NOPE + MLA_ROPE
MLA_V = 64
MLA_Q_RANK = 384
MLA_KV_RANK = 256
ROPE_THETA = 10000.0

LANES = 128
SUBLANES = 8
NEG_BIG = -1e30
VMEM_LIMIT = 56 * 1024 * 1024


def _pick(n, candidates):
    for c in candidates:
        if n % c == 0:
            return c
    raise ValueError(f"no tile for {n} in {candidates}")


def _params(*sem):
    return pltpu.CompilerParams(dimension_semantics=sem, vmem_limit_bytes=VMEM_LIMIT)


def _norm_rows(x, w):
    ms = jnp.mean(x * x, axis=-1, keepdims=True)
    return x * lax.rsqrt(ms + RMS_EPS) * w


def _silu(x):
    half_x = 0.5 * x
    return half_x + half_x * jnp.tanh(half_x)


def _softplus(x):
    return jnp.maximum(x, 0.0) + jnp.log1p(jnp.exp(-jnp.abs(x)))


def _dot(a, b):
    return jnp.dot(a, b, preferred_element_type=F32)


def _dot_nt(a, b):
    return lax.dot_general(a, b, (((1,), (1,)), ((), ())), preferred_element_type=F32)


def _dot_tn(a, b):
    return lax.dot_general(a, b, (((0,), (0,)), ((), ())), preferred_element_type=F32)


def _norm_matmul_kernel(x_ref, nw_ref, w_ref, wg_ref, o_ref, g_ref, xn_ref):
    @pl.when(pl.program_id(1) == 0)
    def _():
        xn = _norm_rows(x_ref[...], nw_ref[...]).astype(BF16)
        xn_ref[...] = xn
        g_ref[...] = _dot(xn, wg_ref[...])

    o_ref[...] = _dot(xn_ref[...], w_ref[...]).astype(o_ref.dtype)


def _norm_matmul_plain_kernel(x_ref, nw_ref, w_ref, o_ref, xn_ref):
    @pl.when(pl.program_id(1) == 0)
    def _():
        xn_ref[...] = _norm_rows(x_ref[...], nw_ref[...]).astype(BF16)

    o_ref[...] = _dot(xn_ref[...], w_ref[...]).astype(o_ref.dtype)


def _norm_matmul_plain(x, nw, w):
    m, d = x.shape
    n = w.shape[1]
    tm = _pick(m, (1024, 768, 512, 384, 256, 192, 128, 64))
    tn = _pick(n, (2048, 1024, 512, 256, 128))
    return pl.pallas_call(
        _norm_matmul_plain_kernel,
        grid=(m // tm, n // tn),
        in_specs=[
            pl.BlockSpec((tm, d), lambda i, j: (i, 0)),
            pl.BlockSpec((1, d), lambda i, j: (0, 0)),
            pl.BlockSpec((d, tn), lambda i, j: (0, j)),
        ],
        out_specs=pl.BlockSpec((tm, tn), lambda i, j: (i, j)),
        out_shape=jax.ShapeDtypeStruct((m, n), BF16),
        scratch_shapes=[pltpu.VMEM((tm, d), BF16)],
        compiler_params=_params("parallel", "arbitrary"),
        name="norm_matmul_plain",
    )(x, nw, w)


def _norm_matmul(x, nw, w, wg):
    m, d = x.shape
    n = w.shape[1]
    tm = _pick(m, (1024, 768, 512, 384, 256, 192, 128, 64))
    tn = _pick(n, (3072, 2048, 1024, 512, 256, 128))
    return pl.pallas_call(
        _norm_matmul_kernel,
        grid=(m // tm, n // tn),
        in_specs=[
            pl.BlockSpec((tm, d), lambda i, j: (i, 0)),
            pl.BlockSpec((1, d), lambda i, j: (0, 0)),
            pl.BlockSpec((d, tn), lambda i, j: (0, j)),
            pl.BlockSpec((d, LANES), lambda i, j: (0, 0)),
        ],
        out_specs=[
            pl.BlockSpec((tm, tn), lambda i, j: (i, j)),
            pl.BlockSpec((tm, LANES), lambda i, j: (i, 0)),
        ],
        out_shape=[
            jax.ShapeDtypeStruct((m, n), BF16),
            jax.ShapeDtypeStruct((m, LANES), F32),
        ],
        scratch_shapes=[pltpu.VMEM((tm, d), BF16)],
        compiler_params=_params("parallel", "arbitrary"),
        name="norm_matmul",
    )(x, nw, w, wg)


def _out_mlp_kernel(h_ref, y_ref, wo_ref, nw_ref, wu_ref, wd_ref, o_ref, xn_ref):
    j = pl.program_id(1)

    @pl.when(j == 0)
    def _():
        h = h_ref[...] + _dot(y_ref[...], wo_ref[...])
        xn_ref[...] = _norm_rows(h, nw_ref[...]).astype(BF16)
        o_ref[...] = h

    a = jnp.maximum(_dot(xn_ref[...], wu_ref[...]), 0.0)
    o_ref[...] += _dot((a * a).astype(BF16), wd_ref[...])


def _out_mlp(h, y, wo, nw, wu, wd):
    m, d = h.shape
    k = y.shape[1]
    f = wu.shape[1]
    tm = _pick(m, (1024, 768, 512, 384, 256, 192, 128, 64))
    tf = _pick(f, (1024, 512, 256, 128))
    return pl.pallas_call(
        _out_mlp_kernel,
        grid=(m // tm, f // tf),
        in_specs=[
            pl.BlockSpec((tm, d), lambda i, j: (i, 0)),
            pl.BlockSpec((tm, k), lambda i, j: (i, 0)),
            pl.BlockSpec((k, d), lambda i, j: (0, 0)),
            pl.BlockSpec((1, d), lambda i, j: (0, 0)),
            pl.BlockSpec((d, tf), lambda i, j: (0, j)),
            pl.BlockSpec((tf, d), lambda i, j: (j, 0)),
        ],
        out_specs=pl.BlockSpec((tm, d), lambda i, j: (i, 0)),
        out_shape=jax.ShapeDtypeStruct((m, d), F32),
        scratch_shapes=[pltpu.VMEM((tm, d), BF16)],
        compiler_params=_params("parallel", "arbitrary"),
        name="out_mlp",
    )(h, y, wo, nw, wu, wd)


def _gdn_in_proj_kernel(x_ref, nw_ref, w_ref, wg_ref, wc_ref, o_ref, g_ref, xn_ref, halo_ref, *,
                        tm, tn, tp, n_q_tiles, n_qk_tiles):
    i = pl.program_id(0)
    j = pl.program_id(1)

    @pl.when(j == 0)
    def _():
        xn = _norm_rows(x_ref[...], nw_ref[...]).astype(BF16)
        xn_ref[...] = xn
        g_ref[...] = _dot(xn, wg_ref[...])

    @pl.when(i == 0)
    def _():
        halo_ref[j] = jnp.zeros((PREP_HALO, tn), F32)

    acc = _dot(xn_ref[...], w_ref[...])
    xb = jnp.concatenate([halo_ref[j], acc], axis=0)
    halo_ref[j] = acc[tm - PREP_HALO:, :]
    t = lax.rem(i * tm, tp) + lax.broadcasted_iota(jnp.int32, (tm, LANES), 0)
    for _ in range(tm // tp + 1):
        t = jnp.where(t >= tp, t - tp, t)
    pad = t < ROW0
    normed = j < n_qk_tiles
    scale = jnp.where(j < n_q_tiles, GDN_HEAD_DIM ** -0.5, 1.0).astype(F32)
    w = wc_ref[...]
    for g in range(tn // LANES):
        cols = slice(g * LANES, (g + 1) * LANES)
        y = xb[PREP_HALO:, cols] * w[GDN_CONV - 1:GDN_CONV, cols]
        for s_ in range(1, GDN_CONV):
            y = y + xb[PREP_HALO - s_:PREP_HALO - s_ + tm, cols] * w[GDN_CONV - 1 - s_:GDN_CONV - s_, cols]
        y = jnp.where(pad, 0.0, _silu(y))
        ss = jnp.sum(y * y, axis=-1, keepdims=True)
        y = y * jnp.where(normed, lax.rsqrt(ss + RMS_EPS) * scale, 1.0)
        o_ref[:, cols] = y.astype(BF16)


def _gdn_in_proj(x, nw, w, wg, w_conv, tp):
    m, d = x.shape
    n = w.shape[1]
    qk = GDN_QK_HEADS * GDN_HEAD_DIM
    tm = _pick(m, (1024, 768, 512, 384, 256, 192, 128, 64))
    tn = 1024
    kern = functools.partial(_gdn_in_proj_kernel, tm=tm, tn=tn, tp=tp, n_q_tiles=qk // tn,
                             n_qk_tiles=2 * qk // tn)
    return pl.pallas_call(
        kern,
        grid=(m // tm, n // tn),
        in_specs=[
            pl.BlockSpec((tm, d), lambda i, j: (i, 0)),
            pl.BlockSpec((1, d), lambda i, j: (0, 0)),
            pl.BlockSpec((d, tn), lambda i, j: (0, j)),
            pl.BlockSpec((d, LANES), lambda i, j: (0, 0)),
            pl.BlockSpec((GDN_CONV, tn), lambda i, j: (0, j)),
        ],
        out_specs=[
            pl.BlockSpec((tm, tn), lambda i, j: (i, j)),
            pl.BlockSpec((tm, LANES), lambda i, j: (i, 0)),
        ],
        out_shape=[
            jax.ShapeDtypeStruct((m, n), BF16),
            jax.ShapeDtypeStruct((m, LANES), F32),
        ],
        scratch_shapes=[pltpu.VMEM((tm, d), BF16), pltpu.VMEM((n // tn, PREP_HALO, tn), F32)],
        compiler_params=_params("arbitrary", "arbitrary"),
        name="gdn_in_proj",
    )(x, nw, w, wg, w_conv)


def _tri_masks():
    ii = lax.broadcasted_iota(jnp.int32, (CHUNK, CHUNK), 0)
    jj = lax.broadcasted_iota(jnp.int32, (CHUNK, CHUNK), 1)
    return ii, jj


def _gdn_gates_kernel(g_ref, gt_ref, alc_ref, dtc_ref, alr_ref, dtr_ref,
                      beta_ref, cumc_ref, cumr_ref, *, nc, hg):
    nh = GDN_V_HEADS
    ii, jj = _tri_masks()
    tril = (ii >= jj).astype(F32)
    triu = (ii <= jj).astype(F32)
    neg_a_c = -jnp.exp(alc_ref[...])
    neg_a_r = -jnp.exp(alr_ref[...])
    for c in range(nc):
        blk = g_ref[c * CHUNK:(c + 1) * CHUNK, :]
        beta = jax.nn.sigmoid(blk[:, 0:nh])
        g = neg_a_c * _softplus(blk[:, nh:2 * nh] + dtc_ref[...])
        gr = neg_a_r * _softplus(gt_ref[:, c * CHUNK:(c + 1) * CHUNK] + dtr_ref[...])
        if c == 0:
            rows = lax.broadcasted_iota(jnp.int32, (CHUNK, nh), 0)
            cols = lax.broadcasted_iota(jnp.int32, (nh, CHUNK), 1)
            beta = jnp.where(rows >= ROW0, beta, 0.0)
            g = jnp.where(rows >= ROW0, g, 0.0)
            gr = jnp.where(cols >= ROW0, gr, 0.0)
        cum = jnp.dot(tril, g, preferred_element_type=F32, precision=lax.Precision.HIGHEST)
        cumr = jnp.dot(gr, triu, preferred_element_type=F32, precision=lax.Precision.HIGHEST)
        cumr = jnp.concatenate([cumr[:nh // 2], cumr[nh // 2:]], axis=1)
        for q in range(nh // hg):
            beta_ref[q, c] = beta[:, q * hg:(q + 1) * hg]
            cumc_ref[q, c] = cum[:, q * hg:(q + 1) * hg]
            cumr_ref[q, c] = cumr[q * hg // 2:(q + 1) * hg // 2, :]


def _gdn_gates(gates, a_log, dt_bias, hg):
    b, tp, _ = gates.shape
    nc = tp // CHUNK
    nh = GDN_V_HEADS
    ng = nh // hg
    kern = functools.partial(_gdn_gates_kernel, nc=nc, hg=hg)
    perm = np.concatenate([np.arange(0, nh, 2), np.arange(1, nh, 2)])
    gates_t = jnp.swapaxes(gates[:, :, nh:2 * nh], 1, 2)[:, perm, :]
    col = jax.ShapeDtypeStruct((b, ng, nc, CHUNK, hg), F32)
    row = jax.ShapeDtypeStruct((b, ng, nc, hg // 2, 2 * CHUNK), F32)
    col_spec = pl.BlockSpec((None, ng, nc, CHUNK, hg), lambda i: (i, 0, 0, 0, 0))
    row_spec = pl.BlockSpec((None, ng, nc, hg // 2, 2 * CHUNK), lambda i: (i, 0, 0, 0, 0))
    small = lambda shape: pl.BlockSpec(shape, lambda i: (0, 0))
    return pl.pallas_call(
        kern,
        grid=(b,),
        in_specs=[
            pl.BlockSpec((None, tp, LANES), lambda i: (i, 0, 0)),
            pl.BlockSpec((None, nh, tp), lambda i: (i, 0, 0)),
            small((1, nh)), small((1, nh)), small((nh, 1)), small((nh, 1)),
        ],
        out_specs=[col_spec, col_spec, row_spec],
        out_shape=[col, col, row],
        compiler_params=_params("parallel"),
        name="gdn_gates",
    )(gates, gates_t, a_log.reshape(1, nh), dt_bias.reshape(1, nh),
      a_log[perm].reshape(nh, 1), dt_bias[perm].reshape(nh, 1))


def _gdn_chunk_kernel(q_ref, k_ref, v_ref, z_ref, beta_ref, cumc_ref, cumr_ref, wn_ref, y_ref,
                      s_ref, u_ref, wq_ref, kd_ref, at_ref, *, nc, hg, cps):
    hd = GDN_HEAD_DIM
    iw = lax.broadcasted_iota(jnp.int32, (CHUNK, hd), 0)
    lane = lax.broadcasted_iota(jnp.int32, (CHUNK, hd), 1)
    jw = lane & (CHUNK - 1)
    first = lane < CHUNK
    causal = iw >= jw
    strict = iw > jw
    eye = (iw == jw).astype(F32)
    blk = [(iw >> l) == (jw >> l) for l in range(1, CHUNK.bit_length())]
    ring = [None] + [(((iw >> l) ^ (jw >> l)) == 1) for l in range(1, CHUNK.bit_length() - 1)]
    wn = wn_ref[...]
    heads = range(hg)
    pairs = range(hg // 2)

    def bdiag(x):
        zero = jnp.zeros_like(x)
        return jnp.concatenate([jnp.where(first, x, zero), jnp.where(first, zero, x)], axis=0)

    rstack = bdiag

    def prepare(chunks):
        units = [(i, p) for i in range(len(chunks)) for p in pairs]
        r0 = [pl.multiple_of(c * CHUNK, CHUNK) for c in chunks]
        cumc = [cumc_ref[c] for c in chunks]
        betac = [beta_ref[c] for c in chunks]
        cumr = [cumr_ref[c] for c in chunks]
        q = {(i, p): q_ref[pl.ds(r0[i], CHUNK), p * hd:(p + 1) * hd] for i, p in units}
        k = {(i, p): k_ref[pl.ds(r0[i], CHUNK), p * hd:(p + 1) * hd] for i, p in units}
        qkk = {u: _dot_nt(jnp.concatenate([q[u], k[u]], axis=0),
                          jnp.concatenate([k[u], k[u]], axis=0)) for u in units}
        cc = {(i, h): jnp.broadcast_to(cumc[i][:, h:h + 1], (CHUNK, hd))
              for i in range(len(chunks)) for h in heads}
        bc = {(i, h): jnp.broadcast_to(betac[i][:, h:h + 1], (CHUNK, hd))
              for i in range(len(chunks)) for h in heads}
        ccw = {(i, p): jnp.where(first, cc[i, 2 * p], cc[i, 2 * p + 1]) for i, p in units}
        bcw = {(i, p): jnp.where(first, bc[i, 2 * p], bc[i, 2 * p + 1]) for i, p in units}
        decay = {(i, p): jnp.exp(jnp.where(causal, ccw[i, p] - cumr[i][p:p + 1, :], -jnp.inf))
                 for i, p in units}
        a = {u: jnp.where(strict, bcw[u] * qkk[u][CHUNK:] * decay[u], 0.0) for u in units}
        for i, p in units:
            at_ref[i, p] = rstack((qkk[i, p][:CHUNK] * decay[i, p]).astype(BF16))
        t = {u: eye - jnp.where(blk[0], a[u], 0.0) for u in units}
        for lvl in range(1, len(blk)):
            a_off = {u: bdiag(jnp.where(ring[lvl], a[u], 0.0).astype(BF16)) for u in units}
            tb = {u: t[u].astype(BF16) for u in units}
            x = {u: _dot(tb[u], a_off[u]).astype(BF16) for u in units}
            t = {u: t[u] - _dot(x[u], bdiag(tb[u])) for u in units}
        ec = {u: jnp.exp(cc[u]) for u in cc}
        rhs = {}
        for i, p in units:
            kf = k[i, p].astype(F32)
            halves = []
            for h in (2 * p, 2 * p + 1):
                v = v_ref[pl.ds(r0[i], CHUNK), h * hd:(h + 1) * hd].astype(F32)
                halves.append(jnp.concatenate([(v * bc[i, h]).astype(BF16),
                                               (kf * (bc[i, h] * ec[i, h])).astype(BF16)], axis=1))
            rhs[i, p] = jnp.concatenate(halves, axis=0)
        sol = {u: _dot(rstack(t[u].astype(BF16)), rhs[u]) for u in units}
        for i, p in units:
            qf = q[i, p].astype(F32)
            kf = k[i, p].astype(F32)
            for r in range(2):
                h = 2 * p + r
                sh = sol[i, p][r * CHUNK:(r + 1) * CHUNK]
                u_ref[i, h] = sh[:, :hd]
                wq_ref[i, h] = jnp.concatenate(
                    [sh[:, hd:].astype(BF16), (qf * ec[i, h]).astype(BF16)], axis=0)
                kd_ref[i, h] = (kf * jnp.exp(cc[i, h][CHUNK - 1:CHUNK, :] - cc[i, h])).astype(BF16)

    def recur(c, slot, s):
        r0 = pl.multiple_of(c * CHUNK, CHUNK)
        cumc = cumc_ref[c]
        ws = [_dot(wq_ref[slot, h], s[h].astype(BF16)) for h in heads]
        v_new = [(u_ref[slot, h] - ws[h][:CHUNK]).astype(BF16) for h in heads]
        upd = [_dot_tn(kd_ref[slot, h], v_new[h]) for h in heads]
        av = [_dot(at_ref[slot, p], jnp.concatenate([v_new[2 * p], v_new[2 * p + 1]], axis=0))
              for p in pairs]
        s_new = [s[h] * jnp.exp(cumc[CHUNK - 1:CHUNK, h:h + 1]) + upd[h] for h in heads]
        for h in heads:
            o = ws[h][CHUNK:] + av[h // 2][(h % 2) * CHUNK:(h % 2 + 1) * CHUNK]
            z = z_ref[pl.ds(r0, CHUNK), h * hd:(h + 1) * hd].astype(F32)
            y = _norm_rows(o, wn) * _silu(z)
            y_ref[pl.ds(r0, CHUNK), h * hd:(h + 1) * hd] = y.astype(BF16)
        return s_new

    s_ref[...] = jnp.zeros_like(s_ref)
    prepare([jnp.int32(i) for i in range(min(cps, nc))])

    def body(it, carry):
        c0 = it * cps
        s = [s_ref[h] for h in heads]
        for i in range(cps):
            s = recur(c0 + i, i, s)
        for h in heads:
            s_ref[h] = s[h]
        prepare([jnp.minimum(c0 + cps + i, nc - 1) for i in range(cps)])
        return carry

    lax.fori_loop(0, nc // cps, body, 0)
    s = [s_ref[h] for h in heads]
    for i in range(nc % cps):
        s = recur(jnp.int32(nc - nc % cps + i), i, s)


def _gdn_chunk(qkv, z, beta, cumc, cumr, w_norm, hg):
    b, tp, _ = qkv.shape
    nc = tp // CHUNK
    hd = GDN_HEAD_DIM
    ng = GDN_V_HEADS // hg
    wqk = hd * hg // 2
    wv = hd * hg
    qk_dim = GDN_QK_HEADS * hd
    v_dim = GDN_V_HEADS * hd
    cps = GDN_CHUNKS_PER_STEP
    kern = functools.partial(_gdn_chunk_kernel, nc=nc, hg=hg, cps=cps)
    col_spec = pl.BlockSpec((None, None, nc, CHUNK, hg), lambda i, j: (i, j, 0, 0, 0))
    row_spec = pl.BlockSpec((None, None, nc, hg // 2, 2 * CHUNK), lambda i, j: (i, j, 0, 0, 0))
    return pl.pallas_call(
        kern,
        grid=(b, ng),
        in_specs=[
            pl.BlockSpec((None, tp, wqk), lambda i, j: (i, 0, j)),
            pl.BlockSpec((None, tp, wqk), lambda i, j: (i, 0, qk_dim // wqk + j)),
            pl.BlockSpec((None, tp, wv), lambda i, j: (i, 0, 2 * qk_dim // wv + j)),
            pl.BlockSpec((None, tp, wv), lambda i, j: (i, 0, j)),
            col_spec, col_spec, row_spec,
            pl.BlockSpec((1, hd), lambda i, j: (0, 0)),
        ],
        out_specs=pl.BlockSpec((None, tp, wv), lambda i, j: (i, 0, j)),
        out_shape=jax.ShapeDtypeStruct((b, tp, v_dim), BF16),
        scratch_shapes=[
            pltpu.VMEM((hg, hd, hd), F32),
            pltpu.VMEM((cps, hg, CHUNK, hd), F32),
            pltpu.VMEM((cps, hg, 2 * CHUNK, hd), BF16),
            pltpu.VMEM((cps, hg, CHUNK, hd), BF16),
            pltpu.VMEM((cps, hg // 2, 2 * CHUNK, hd), BF16),
        ],
        compiler_params=_params("parallel", "arbitrary"),
        name="gdn_chunk",
    )(qkv, qkv, qkv, z, beta, cumc, cumr, w_norm.reshape(1, hd))


def _gdn_layer(h, nw, w_in, w_conv, a_log, dt_bias, w_norm, w_out):
    b, tp, d = h.shape
    m = b * tp
    conv_dim = 2 * GDN_QK_HEADS * GDN_HEAD_DIM + GDN_V_HEADS * GDN_HEAD_DIM
    main = conv_dim + GDN_V_HEADS * GDN_HEAD_DIM
    hg = GDN_HEADS_PER_STEP
    w_qkv = w_in[:, :conv_dim].astype(BF16)
    w_z = w_in[:, conv_dim:main].astype(BF16)
    w_gate = jnp.pad(w_in[:, main:], ((0, 0), (0, LANES - 2 * GDN_V_HEADS))).astype(BF16)
    hn = h.reshape(m, d)
    qkv, gates = _gdn_in_proj(hn, nw.reshape(1, d), w_qkv, w_gate, w_conv, tp)
    z = _norm_matmul_plain(hn, nw.reshape(1, d), w_z)
    gates = gates.reshape(b, tp, LANES)
    beta, cumc, cumr = _gdn_gates(gates, a_log, dt_bias, hg)
    y = _gdn_chunk(qkv.reshape(b, tp, conv_dim), z.reshape(b, tp, -1), beta, cumc, cumr, w_norm, hg)
    return y.reshape(m, -1), w_out.astype(BF16)


def _mlstm_gates_kernel(g_ref, gt_ref, bc_ref, br_ref, cumc_ref, rc_ref, rr_ref, *, nc, hg):
    nh = MLSTM_HEADS
    ii, jj = _tri_masks()
    tril = (ii >= jj).astype(F32)
    triu = (ii <= jj).astype(F32)

    def split(raw, axis):
        capped = GATE_SOFTCAP * jnp.tanh(raw / GATE_SOFTCAP)
        if axis == 1:
            i_pre, f_pre = capped[:, :nh], capped[:, nh:2 * nh]
        else:
            i_pre, f_pre = capped[:nh], capped[nh:2 * nh]
        return i_pre, -_softplus(-f_pre)

    for c in range(nc):
        i_c, lf_c = split(g_ref[c * CHUNK:(c + 1) * CHUNK, 0:2 * nh] + bc_ref[...], 1)
        i_r, lf_r = split(gt_ref[:, c * CHUNK:(c + 1) * CHUNK] + br_ref[...], 0)
        if c == 0:
            rows = lax.broadcasted_iota(jnp.int32, (CHUNK, nh), 0)
            cols = lax.broadcasted_iota(jnp.int32, (nh, CHUNK), 1)
            i_c = jnp.where(rows >= ROW0, i_c, -jnp.inf)
            lf_c = jnp.where(rows >= ROW0, lf_c, 0.0)
            i_r = jnp.where(cols >= ROW0, i_r, -jnp.inf)
            lf_r = jnp.where(cols >= ROW0, lf_r, 0.0)
        cum = jnp.dot(tril, lf_c, preferred_element_type=F32, precision=lax.Precision.HIGHEST)
        cumr = jnp.dot(lf_r, triu, preferred_element_type=F32, precision=lax.Precision.HIGHEST)
        rc = i_c - cum
        rr = i_r - cumr
        for q in range(nh // hg):
            cumc_ref[q, c] = cum[:, q * hg:(q + 1) * hg]
            rc_ref[q, c] = rc[:, q * hg:(q + 1) * hg]
            rr_ref[q, c] = rr[q * hg:(q + 1) * hg, :]


def _mlstm_gates(gates, gates_t, bias, hg):
    b, tp, _ = gates.shape
    nc = tp // CHUNK
    nh = MLSTM_HEADS
    ng = nh // hg
    kern = functools.partial(_mlstm_gates_kernel, nc=nc, hg=hg)
    col = jax.ShapeDtypeStruct((b, ng, nc, CHUNK, hg), F32)
    row = jax.ShapeDtypeStruct((b, ng, nc, hg, CHUNK), F32)
    col_spec = pl.BlockSpec((None, ng, nc, CHUNK, hg), lambda i: (i, 0, 0, 0, 0))
    row_spec = pl.BlockSpec((None, ng, nc, hg, CHUNK), lambda i: (i, 0, 0, 0, 0))
    return pl.pallas_call(
        kern,
        grid=(b,),
        in_specs=[
            pl.BlockSpec((None, tp, LANES), lambda i: (i, 0, 0)),
            pl.BlockSpec((None, 2 * nh, tp), lambda i: (i, 0, 0)),
            pl.BlockSpec((1, 2 * nh), lambda i: (0, 0)),
            pl.BlockSpec((2 * nh, 1), lambda i: (0, 0)),
        ],
        out_specs=[col_spec, col_spec, row_spec],
        out_shape=[col, col, row],
        compiler_params=_params("parallel"),
        name="mlstm_gates",
    )(gates, gates_t, bias.reshape(1, 2 * nh), bias.reshape(2 * nh, 1))


def _mlstm_chunk_kernel(q_ref, k_ref, v_ref, og_ref, cumc_ref, rc_ref, rr_ref, wn_ref, y_ref,
                        c_ref, m_ref, *, nc, hg, cps):
    dv = MLSTM_DV
    ii, jj = _tri_masks()
    causal = ii >= jj
    ones_col = jnp.ones((CHUNK, LANES), BF16)
    heads = range(hg)

    c_ref[...] = jnp.zeros_like(c_ref)
    m_ref[...] = jnp.zeros_like(m_ref)

    def step(chunks):
        n = len(chunks)
        units = [(i, h) for i in range(n) for h in heads]
        r0 = [pl.multiple_of(c * CHUNK, CHUNK) for c in chunks]
        cumc = [cumc_ref[c] for c in chunks]
        rcol = [rc_ref[c] for c in chunks]
        rrow = [rr_ref[c] for c in chunks]
        q = {(i, h): q_ref[pl.ds(r0[i], CHUNK), h * LANES:(h + 1) * LANES] for i, h in units}
        k = {(i, h): k_ref[pl.ds(r0[i], CHUNK), h * LANES:(h + 1) * LANES] for i, h in units}
        v_aug = {(i, h): jnp.concatenate(
            [v_ref[pl.ds(r0[i], CHUNK), h * dv:(h + 1) * dv], ones_col], axis=1) for i, h in units}
        qk = {u: _dot_nt(q[u], k[u]) for u in units}
        cc = {(i, h): jnp.broadcast_to(cumc[i][:, h:h + 1], (CHUNK, LANES)) for i, h in units}
        rc = {(i, h): jnp.broadcast_to(rcol[i][:, h:h + 1], (CHUNK, LANES)) for i, h in units}
        rr = {(i, h): rrow[i][h:h + 1, :] for i, h in units}
        rmax = {u: jnp.max(jnp.where(causal, rr[u], -jnp.inf), axis=-1, keepdims=True) for u in units}
        m_in, keep, kw = {}, {}, {}
        m = [m_ref[h][0:1, :] for h in heads]
        for i, h in units:
            m_in[i, h] = m[h]
            c_last = cc[i, h][CHUNK - 1:CHUNK, :]
            log_keep = c_last + m[h]
            m_new = jnp.maximum(log_keep, c_last + jnp.max(rc[i, h], axis=0, keepdims=True))
            keep[i, h] = jnp.exp(log_keep - m_new)
            w_end = jnp.exp(c_last + rc[i, h] - m_new)
            kw[i, h] = (k[i, h].astype(F32) * w_end).astype(BF16)
            m[h] = m_new
        for h in heads:
            m_ref[h] = jnp.broadcast_to(m[h], m_ref.shape[1:])
        upd = {u: _dot_tn(kw[u], v_aug[u]) for u in units}
        g, w_intra = {}, {}
        for u in units:
            g[u] = jnp.maximum(m_in[u], rmax[u])
            w_intra[u] = (jnp.exp(jnp.where(causal, rr[u] - g[u][:, :CHUNK], -jnp.inf))
                          * qk[u]).astype(BF16)
        wv = {u: _dot(w_intra[u], v_aug[u]) for u in units}
        state = [c_ref[h] for h in heads]
        for i in range(n):
            qc = [_dot(q[i, h], state[h].astype(BF16)) for h in heads]
            for h in heads:
                s_inter = jnp.exp(m_in[i, h] - g[i, h])
                tot = jnp.concatenate([s_inter, s_inter], axis=1) * qc[h] + wv[i, h]
                den = tot[:, dv:]
                inv = 1.0 / jnp.maximum(jnp.abs(den), jnp.exp(-(cc[i, h] + g[i, h])))
                hs = tot[:, :dv] * inv
                og = og_ref[pl.ds(r0[i], CHUNK), h * dv:(h + 1) * dv].astype(F32)
                y = _norm_rows(hs, wn_ref[:, h * dv:(h + 1) * dv]) * jax.nn.sigmoid(og)
                y_ref[pl.ds(r0[i], CHUNK), h * dv:(h + 1) * dv] = y.astype(BF16)
                state[h] = jnp.concatenate([keep[i, h], keep[i, h]], axis=1) * state[h] + upd[i, h]
        for h in heads:
            c_ref[h] = state[h]

    def body(it, carry):
        step([it * cps + i for i in range(cps)])
        return carry

    lax.fori_loop(0, nc // cps, body, 0)
    if nc % cps:
        step([jnp.int32(nc - nc % cps + i) for i in range(nc % cps)])


def _mlstm_chunk(proj, cumc, rc, rr, w_norm, hg):
    b, tp, _ = proj.shape
    nc = tp // CHUNK
    nh = MLSTM_HEADS
    ng = nh // hg
    wb = LANES * hg
    kern = functools.partial(_mlstm_chunk_kernel, nc=nc, hg=hg, cps=MLSTM_CHUNKS_PER_STEP)
    col_spec = pl.BlockSpec((None, None, nc, CHUNK, hg), lambda i, j: (i, j, 0, 0, 0))
    row_spec = pl.BlockSpec((None, None, nc, hg, CHUNK), lambda i, j: (i, j, 0, 0, 0))
    return pl.pallas_call(
        kern,
        grid=(b, ng),
        in_specs=[
            pl.BlockSpec((None, tp, wb), lambda i, j: (i, 0, j)),
            pl.BlockSpec((None, tp, wb), lambda i, j: (i, 0, ng + j)),
            pl.BlockSpec((None, tp, wb), lambda i, j: (i, 0, 2 * ng + j)),
            pl.BlockSpec((None, tp, wb), lambda i, j: (i, 0, 3 * ng + j)),
            col_spec, col_spec, row_spec,
            pl.BlockSpec((1, wb), lambda i, j: (0, j)),
        ],
        out_specs=pl.BlockSpec((None, tp, wb), lambda i, j: (i, 0, j)),
        out_shape=jax.ShapeDtypeStruct((b, tp, nh * MLSTM_DV), BF16),
        scratch_shapes=[
            pltpu.VMEM((hg, LANES, 2 * MLSTM_DV), F32),
            pltpu.VMEM((hg, SUBLANES, LANES), F32),
        ],
        compiler_params=_params("parallel", "arbitrary"),
        name="mlstm_chunk",
    )(proj, proj, proj, proj, cumc, rc, rr, w_norm.reshape(1, nh * MLSTM_DV))


def _pad_heads(w, nh, dh):
    d = w.shape[0]
    w = w.reshape(d, nh, dh)
    return jnp.pad(w, ((0, 0), (0, 0), (0, LANES - dh))).reshape(d, nh * LANES)


def _mlstm_layer(h, nw, w_in, gate_bias, w_norm, w_out):
    b, tp, d = h.shape
    m = b * tp
    nh = MLSTM_HEADS
    qk = nh * MLSTM_DQK
    vd = nh * MLSTM_DV
    hg = MLSTM_HEADS_PER_STEP
    w_main = jnp.concatenate([
        _pad_heads(w_in[:, :qk], nh, MLSTM_DQK),
        _pad_heads(w_in[:, qk:2 * qk] * (MLSTM_DQK ** -0.5), nh, MLSTM_DQK),
        w_in[:, 2 * qk:2 * qk + 2 * vd],
    ], axis=1).astype(BF16)
    w_gate = jnp.pad(w_in[:, 2 * qk + 2 * vd:], ((0, 0), (0, LANES - 2 * nh))).astype(BF16)
    proj, gates = _norm_matmul(h.reshape(m, d), nw.reshape(1, d), w_main, w_gate)
    proj = proj.reshape(b, tp, -1)
    gates = gates.reshape(b, tp, LANES)
    gates_t = jnp.swapaxes(gates[:, :, :2 * nh], 1, 2)
    cumc, rc, rr = _mlstm_gates(gates, gates_t, gate_bias, hg)
    y = _mlstm_chunk(proj, cumc, rc, rr, w_norm, hg)
    return y.reshape(m, vd), w_out.astype(BF16)


def _mla_proj_kernel(h_ref, nw_ref, win_ref, qn_ref, wuq_ref, wuqs_ref, kvn_ref, wuk_ref, wuv_ref,
                     qtab_ref, ktab_ref, q_out, k_out, v_out):
    xn = _norm_rows(h_ref[...], nw_ref[...]).astype(BF16)
    c = _dot(xn, win_ref[...])
    cq = _norm_rows(c[:, :MLA_Q_RANK], qn_ref[...]).astype(BF16)
    lat = MLA_Q_RANK + MLA_KV_RANK
    ckv = _norm_rows(c[:, MLA_Q_RANK:lat], kvn_ref[...]).astype(BF16)
    kr = c[:, lat:lat + LANES]
    kr_swap = c[:, lat + LANES:]
    q = _dot(cq, wuq_ref[...])
    q_swap = _dot(cq, wuqs_ref[...])
    kn = _dot(ckv, wuk_ref[...])
    v_out[...] = _dot(ckv, wuv_ref[...]).astype(BF16)

    def head_norm_rope(x, x_swap, tab_ref):
        ms = jnp.sum(x * x, axis=-1, keepdims=True) * (1.0 / MLA_QK)
        return (x * tab_ref[0] + x_swap * tab_ref[1]) * lax.rsqrt(ms + RMS_EPS)

    for h in range(MLA_HEADS):
        sl = slice(h * LANES, (h + 1) * LANES)
        q_out[:, sl] = head_norm_rope(q[:, sl], q_swap[:, sl], qtab_ref).astype(BF16)
        k_out[:, sl] = head_norm_rope(kn[:, sl] + kr, kr_swap, ktab_ref).astype(BF16)


def _mla_proj(h, nw, w_in, q_norm, w_uq, w_uq_swap, kv_norm, w_uk, w_uv, qtab, ktab):
    b, tp, d = h.shape
    tt = _pick(tp, (704, 352, 192, 64))
    nh = MLA_HEADS
    full = lambda a: pl.BlockSpec(a.shape, lambda i, j: (0,) * a.ndim)
    tab = pl.BlockSpec((2, tt, LANES), lambda i, j: (0, j, 0))
    row = lambda n: pl.BlockSpec((None, tt, n), lambda i, j: (i, j, 0))
    return pl.pallas_call(
        _mla_proj_kernel,
        grid=(b, tp // tt),
        in_specs=[row(d), full(nw), full(w_in), full(q_norm), full(w_uq), full(w_uq_swap),
                  full(kv_norm), full(w_uk), full(w_uv), tab, tab],
        out_specs=[row(nh * LANES), row(nh * LANES), row(nh * MLA_V)],
        out_shape=[
            jax.ShapeDtypeStruct((b, tp, nh * LANES), BF16),
            jax.ShapeDtypeStruct((b, tp, nh * LANES), BF16),
            jax.ShapeDtypeStruct((b, tp, nh * MLA_V), BF16),
        ],
        compiler_params=_params("parallel", "arbitrary"),
        name="mla_proj",
    )(h, nw, w_in, q_norm, w_uq, w_uq_swap, kv_norm, w_uk, w_uv, qtab, ktab)


def _mla_attn_kernel(q_ref, k_ref, v_ref, o_ref, *, tq, nq):
    ri = lax.broadcasted_iota(jnp.int32, (tq, tq), 0)
    ci = lax.broadcasted_iota(jnp.int32, (tq, tq), 1)
    tri = jnp.where(ci <= ri, 0.0, NEG_BIG)
    tri0 = jnp.where(ci >= ROW0, tri, NEG_BIG)
    pad_row = jnp.where(lax.broadcasted_iota(jnp.int32, (1, max(nq - 1, 1) * tq), 1) >= ROW0, 0.0, NEG_BIG)
    lane = lax.broadcasted_iota(jnp.int32, (tq, LANES), 1)
    units = [(qi, r) for qi in range(nq) for r in range(2)]

    def scores(qi, r):
        lo = qi * tq
        hs = slice(r * LANES, (r + 1) * LANES)
        q = q_ref[lo:lo + tq, hs]
        s_diag = _dot_nt(q, k_ref[lo:lo + tq, hs]) + (tri if qi else tri0)
        s_main = _dot_nt(q, k_ref[0:lo, hs]) + pad_row[:, :lo] if qi else None
        return s_main, s_diag

    def attend(qi, s_main, s_diag):
        lo = qi * tq
        m = jnp.max(s_diag, axis=-1, keepdims=True)
        if qi:
            m = jnp.maximum(m, jnp.max(s_main, axis=-1, keepdims=True))
        p = jnp.exp(s_diag - m)
        l = jnp.sum(p, axis=-1, keepdims=True)
        o = _dot(p.astype(BF16), v_ref[lo:lo + tq, :])
        if qi:
            p = jnp.exp(s_main - m)
            l = l + jnp.sum(p, axis=-1, keepdims=True)
            o = o + _dot(p.astype(BF16), v_ref[0:lo, :])
        return o * (1.0 / l)

    nxt = scores(*units[0])
    outs = []
    for n, (qi, r) in enumerate(units):
        cur = nxt
        if n + 1 < len(units):
            nxt = scores(*units[n + 1])
        outs.append(attend(qi, *cur))
        if r == 1:
            o = jnp.where(lane < MLA_V, outs[0], outs[1])
            outs = []
            if qi == 0:
                o = jnp.where(lax.broadcasted_iota(jnp.int32, (tq, LANES), 0) >= ROW0, o, 0.0)
            o_ref[qi * tq:(qi + 1) * tq, :] = o.astype(BF16)


def _mla_attn(q, k, v):
    b, tp, _ = q.shape
    tq = _pick(tp, (352, 192, 64))
    npair = MLA_HEADS // 2
    kern = functools.partial(_mla_attn_kernel, tq=tq, nq=tp // tq)
    return pl.pallas_call(
        kern,
        grid=(b, npair),
        in_specs=[
            pl.BlockSpec((None, tp, 2 * LANES), lambda i, j: (i, 0, j)),
            pl.BlockSpec((None, tp, 2 * LANES), lambda i, j: (i, 0, j)),
            pl.BlockSpec((None, tp, LANES), lambda i, j: (i, 0, j)),
        ],
        out_specs=pl.BlockSpec((None, tp, LANES), lambda i, j: (i, 0, j)),
        out_shape=jax.ShapeDtypeStruct((b, tp, MLA_HEADS * MLA_V), BF16),
        compiler_params=_params("parallel", "arbitrary"),
        name="mla_attn",
    )(q, k, v)


def _swap_rope_halves(a):
    half = MLA_ROPE // 2
    lo, hi = MLA_NOPE, MLA_NOPE + half
    return jnp.concatenate([a[..., :lo], a[..., hi:hi + half], a[..., lo:hi], a[..., hi + half:]], axis=-1)


def _rope_tables(tp, gain, scale):
    half = MLA_ROPE // 2
    pos = jnp.arange(tp, dtype=F32) - float(ROW0)
    inv_freq = ROPE_THETA ** (-jnp.arange(0, MLA_ROPE, 2, dtype=F32) / MLA_ROPE)
    ang = pos[:, None] * inv_freq[None, :]
    cos, sin = jnp.cos(ang), jnp.sin(ang)
    ones = jnp.ones((tp, MLA_NOPE), F32)
    z_nope = jnp.zeros((tp, MLA_NOPE), F32)
    z_tail = jnp.zeros((tp, LANES - MLA_QK), F32)
    cos_t = jnp.concatenate([ones, cos, cos, z_tail], axis=1)
    sin_t = jnp.concatenate([z_nope, -sin, sin, z_tail], axis=1)
    g = jnp.pad(gain, (0, LANES - MLA_QK)).reshape(1, LANES) * scale
    return jnp.stack([cos_t * g, sin_t * _swap_rope_halves(g)])


def _mla_layer(h, nw, w_in, q_norm, w_uq, kv_norm, w_ukv, q_head_norm, k_head_norm, w_out):
    b, tp, d = h.shape
    m = b * tp
    nh = MLA_HEADS
    lat = MLA_Q_RANK + MLA_KV_RANK
    w_kr = jnp.concatenate([jnp.zeros((d, MLA_NOPE), F32), w_in[:, lat:],
                            jnp.zeros((d, LANES - MLA_QK), F32)], axis=1)
    w_in_p = jnp.concatenate([w_in[:, :lat], w_kr, _swap_rope_halves(w_kr)], axis=1).astype(BF16)
    w_uq_p = _pad_heads(w_uq, nh, MLA_QK)
    w_uq_swap = _swap_rope_halves(w_uq_p.reshape(MLA_Q_RANK, nh, LANES)).reshape(MLA_Q_RANK, nh * LANES)
    w_uq_p, w_uq_swap = w_uq_p.astype(BF16), w_uq_swap.astype(BF16)
    w_ukv3 = w_ukv.reshape(MLA_KV_RANK, nh, MLA_NOPE + MLA_V)
    w_uk_p = _pad_heads(w_ukv3[:, :, :MLA_NOPE].reshape(MLA_KV_RANK, nh * MLA_NOPE), nh, MLA_NOPE).astype(BF16)
    w_uv = w_ukv3[:, :, MLA_NOPE:].reshape(MLA_KV_RANK, nh * MLA_V).astype(BF16)
    q, k, v = _mla_proj(h, nw.reshape(1, d), w_in_p, q_norm.reshape(1, -1), w_uq_p, w_uq_swap,
                        kv_norm.reshape(1, -1), w_uk_p, w_uv,
                        _rope_tables(tp, q_head_norm, MLA_QK ** -0.5),
                        _rope_tables(tp, k_head_norm, 1.0))
    o = _mla_attn(q, k, v)
    return o.reshape(m, nh * MLA_V), w_out.astype(BF16)


def kernel(x, meta_tokens, attn_norm, ffn_norm, ff_up, ff_down, gdn_in, gdn_conv, gdn_a_log, gdn_dt_bias, gdn_norm, gdn_out, mlstm_in, mlstm_gate_bias, mlstm_norm, mlstm_out, mla_in, mla_q_norm, mla_uq, mla_kv_norm, mla_ukv, mla_q_head_norm, mla_k_head_norm, mla_out):
    b, t, d = x.shape
    depth = attn_norm.shape[0]
    meta = jnp.broadcast_to(meta_tokens[None].astype(x.dtype), (b, N_META, d))
    h = jnp.concatenate([jnp.zeros((b, LEAD_PAD, d), x.dtype), meta, x], axis=1)
    tp = h.shape[1]
    for layer in range(depth):
        kind, j = layer % N_MIXERS, layer // N_MIXERS
        if kind == 0:
            y, w_out = _gdn_layer(h, attn_norm[layer], gdn_in[j], gdn_conv[j], gdn_a_log[j],
                                  gdn_dt_bias[j], gdn_norm[j], gdn_out[j])
        elif kind == 1:
            y, w_out = _mlstm_layer(h, attn_norm[layer], mlstm_in[j], mlstm_gate_bias[j],
                                    mlstm_norm[j], mlstm_out[j])
        else:
            y, w_out = _mla_layer(h, attn_norm[layer], mla_in[j], mla_q_norm[j], mla_uq[j],
                                  mla_kv_norm[j], mla_ukv[j], mla_q_head_norm[j],
                                  mla_k_head_norm[j], mla_out[j])
        h = _out_mlp(h.reshape(b * tp, d), y, w_out, ffn_norm[layer].reshape(1, d),
                     ff_up[layer].astype(BF16), ff_down[layer].astype(BF16)).reshape(b, tp, d)
    return h[:, LEAD_PAD + N_META:]
```

```python
import functools

import jax
import jax.numpy as jnp
import numpy as np
from jax import lax
from jax.experimental import pallas as pl
from jax.experimental.pallas import tpu as pltpu

F32 = jnp.float32
BF16 = jnp.bfloat16

N_META = 16
CHUNK = 64
LEAD_PAD = (-N_META) % CHUNK
ROW0 = LEAD_PAD
RMS_EPS = 1e-6
N_MIXERS = 3

GDN_QK_HEADS = 8
GDN_V_HEADS = 16
GDN_HEAD_DIM = 128
GDN_CONV = 4
PREP_HALO = 16
GDN_HEADS_PER_STEP = 8
GDN_CHUNKS_PER_STEP = 5

MLSTM_HEADS = 8
MLSTM_DQK = 64
MLSTM_DV = 128
GATE_SOFTCAP = 15.0
MLSTM_HEADS_PER_STEP = 8
MLSTM_CHUNKS_PER_STEP = 2

MLA_HEADS = 16
MLA_NOPE = 64
MLA_ROPE = 32
MLA_QK = MLA_NOPE + MLA_ROPE
MLA_V = 64
MLA_Q_RANK = 384
MLA_KV_RANK = 256
ROPE_THETA = 10000.0

LANES = 128
SUBLANES = 8
NEG_BIG = -1e30
VMEM_LIMIT = 56 * 1024 * 1024


def _pick(n, candidates):
    for c in candidates:
        if n % c == 0:
            return c
    raise ValueError(f"no tile for {n} in {candidates}")


def _params(*sem):
    return pltpu.CompilerParams(dimension_semantics=sem, vmem_limit_bytes=VMEM_LIMIT)


def _norm_rows(x, w):
    ms = jnp.mean(x * x, axis=-1, keepdims=True)
    return x * lax.rsqrt(ms + RMS_EPS) * w


def _silu(x):
    half_x = 0.5 * x
    return half_x + half_x * jnp.tanh(half_x)


def _softplus(x):
    return jnp.maximum(x, 0.0) + jnp.log1p(jnp.exp(-jnp.abs(x)))


def _dot(a, b):
    return jnp.dot(a, b, preferred_element_type=F32)


def _dot_nt(a, b):
    return lax.dot_general(a, b, (((1,), (1,)), ((), ())), preferred_element_type=F32)


def _dot_tn(a, b):
    return lax.dot_general(a, b, (((0,), (0,)), ((), ())), preferred_element_type=F32)


def _norm_matmul_kernel(x_ref, nw_ref, w_ref, wg_ref, o_ref, g_ref, xn_ref):
    @pl.when(pl.program_id(1) == 0)
    def _():
        xn = _norm_rows(x_ref[...], nw_ref[...]).astype(BF16)
        xn_ref[...] = xn
        g_ref[...] = _dot(xn, wg_ref[...])

    o_ref[...] = _dot(xn_ref[...], w_ref[...]).astype(o_ref.dtype)


def _norm_matmul_plain_kernel(x_ref, nw_ref, w_ref, o_ref, xn_ref):
    @pl.when(pl.program_id(1) == 0)
    def _():
        xn_ref[...] = _norm_rows(x_ref[...], nw_ref[...]).astype(BF16)

    o_ref[...] = _dot(xn_ref[...], w_ref[...]).astype(o_ref.dtype)


def _norm_matmul_plain(x, nw, w):
    m, d = x.shape
    n = w.shape[1]
    tm = _pick(m, (1024, 768, 512, 384, 256, 192, 128, 64))
    tn = _pick(n, (2048, 1024, 512, 256, 128))
    return pl.pallas_call(
        _norm_matmul_plain_kernel,
        grid=(m // tm, n // tn),
        in_specs=[
            pl.BlockSpec((tm, d), lambda i, j: (i, 0)),
            pl.BlockSpec((1, d), lambda i, j: (0, 0)),
            pl.BlockSpec((d, tn), lambda i, j: (0, j)),
        ],
        out_specs=pl.BlockSpec((tm, tn), lambda i, j: (i, j)),
        out_shape=jax.ShapeDtypeStruct((m, n), BF16),
        scratch_shapes=[pltpu.VMEM((tm, d), BF16)],
        compiler_params=_params("parallel", "arbitrary"),
        name="norm_matmul_plain",
    )(x, nw, w)


def _norm_matmul(x, nw, w, wg):
    m, d = x.shape
    n = w.shape[1]
    tm = _pick(m, (1024, 768, 512, 384, 256, 192, 128, 64))
    tn = _pick(n, (3072, 2048, 1024, 512, 256, 128))
    return pl.pallas_call(
        _norm_matmul_kernel,
        grid=(m // tm, n // tn),
        in_specs=[
            pl.BlockSpec((tm, d), lambda i, j: (i, 0)),
            pl.BlockSpec((1, d), lambda i, j: (0, 0)),
            pl.BlockSpec((d, tn), lambda i, j: (0, j)),
            pl.BlockSpec((d, LANES), lambda i, j: (0, 0)),
        ],
        out_specs=[
            pl.BlockSpec((tm, tn), lambda i, j: (i, j)),
            pl.BlockSpec((tm, LANES), lambda i, j: (i, 0)),
        ],
        out_shape=[
            jax.ShapeDtypeStruct((m, n), BF16),
            jax.ShapeDtypeStruct((m, LANES), F32),
        ],
        scratch_shapes=[pltpu.VMEM((tm, d), BF16)],
        compiler_params=_params("parallel", "arbitrary"),
        name="norm_matmul",
    )(x, nw, w, wg)


def _out_mlp_kernel(h_ref, y_ref, wo_ref, nw_ref, wu_ref, wd_ref, o_ref, xn_ref):
    j = pl.program_id(1)

    @pl.when(j == 0)
    def _():
        h = h_ref[...] + _dot(y_ref[...], wo_ref[...])
        xn_ref[...] = _norm_rows(h, nw_ref[...]).astype(BF16)
        o_ref[...] = h

    a = jnp.maximum(_dot(xn_ref[...], wu_ref[...]), 0.0)
    o_ref[...] += _dot((a * a).astype(BF16), wd_ref[...])


def _out_mlp(h, y, wo, nw, wu, wd):
    m, d = h.shape
    k = y.shape[1]
    f = wu.shape[1]
    tm = _pick(m, (1024, 768, 512, 384, 256, 192, 128, 64))
    tf = _pick(f, (1024, 512, 256, 128))
    return pl.pallas_call(
        _out_mlp_kernel,
        grid=(m // tm, f // tf),
        in_specs=[
            pl.BlockSpec((tm, d), lambda i, j: (i, 0)),
            pl.BlockSpec((tm, k), lambda i, j: (i, 0)),
            pl.BlockSpec((k, d), lambda i, j: (0, 0)),
            pl.BlockSpec((1, d), lambda i, j: (0, 0)),
            pl.BlockSpec((d, tf), lambda i, j: (0, j)),
            pl.BlockSpec((tf, d), lambda i, j: (j, 0)),
        ],
        out_specs=pl.BlockSpec((tm, d), lambda i, j: (i, 0)),
        out_shape=jax.ShapeDtypeStruct((m, d), F32),
        scratch_shapes=[pltpu.VMEM((tm, d), BF16)],
        compiler_params=_params("parallel", "arbitrary"),
        name="out_mlp",
    )(h, y, wo, nw, wu, wd)


def _gdn_in_proj_kernel(x_ref, nw_ref, w_ref, wg_ref, wc_ref, o_ref, g_ref, xn_ref, halo_ref, *,
                        tm, tn, tp, n_q_tiles, n_qk_tiles):
    i = pl.program_id(0)
    j = pl.program_id(1)

    @pl.when(j == 0)
    def _():
        xn = _norm_rows(x_ref[...], nw_ref[...]).astype(BF16)
        xn_ref[...] = xn
        g_ref[...] = _dot(xn, wg_ref[...])

    @pl.when(i == 0)
    def _():
        halo_ref[j] = jnp.zeros((PREP_HALO, tn), F32)

    acc = _dot(xn_ref[...], w_ref[...])
    xb = jnp.concatenate([halo_ref[j], acc], axis=0)
    halo_ref[j] = acc[tm - PREP_HALO:, :]
    t = lax.rem(i * tm, tp) + lax.broadcasted_iota(jnp.int32, (tm, LANES), 0)
    for _ in range(tm // tp + 1):
        t = jnp.where(t >= tp, t - tp, t)
    pad = t < ROW0
    normed = j < n_qk_tiles
    scale = jnp.where(j < n_q_tiles, GDN_HEAD_DIM ** -0.5, 1.0).astype(F32)
    w = wc_ref[...]
    for g in range(tn // LANES):
        cols = slice(g * LANES, (g + 1) * LANES)
        y = xb[PREP_HALO:, cols] * w[GDN_CONV - 1:GDN_CONV, cols]
        for s_ in range(1, GDN_CONV):
            y = y + xb[PREP_HALO - s_:PREP_HALO - s_ + tm, cols] * w[GDN_CONV - 1 - s_:GDN_CONV - s_, cols]
        y = jnp.where(pad, 0.0, _silu(y))
        ss = jnp.sum(y * y, axis=-1, keepdims=True)
        y = y * jnp.where(normed, lax.rsqrt(ss + RMS_EPS) * scale, 1.0)
        o_ref[:, cols] = y.astype(BF16)


def _gdn_in_proj(x, nw, w, wg, w_conv, tp):
    m, d = x.shape
    n = w.shape[1]
    qk = GDN_QK_HEADS * GDN_HEAD_DIM
    tm = _pick(m, (1024, 768, 512, 384, 256, 192, 128, 64))
    tn = 1024
    kern = functools.partial(_gdn_in_proj_kernel, tm=tm, tn=tn, tp=tp, n_q_tiles=qk // tn,
                             n_qk_tiles=2 * qk // tn)
    return pl.pallas_call(
        kern,
        grid=(m // tm, n // tn),
        in_specs=[
            pl.BlockSpec((tm, d), lambda i, j: (i, 0)),
            pl.BlockSpec((1, d), lambda i, j: (0, 0)),
            pl.BlockSpec((d, tn), lambda i, j: (0, j)),
            pl.BlockSpec((d, LANES), lambda i, j: (0, 0)),
            pl.BlockSpec((GDN_CONV, tn), lambda i, j: (0, j)),
        ],
        out_specs=[
            pl.BlockSpec((tm, tn), lambda i, j: (i, j)),
            pl.BlockSpec((tm, LANES), lambda i, j: (i, 0)),
        ],
        out_shape=[
            jax.ShapeDtypeStruct((m, n), BF16),
            jax.ShapeDtypeStruct((m, LANES), F32),
        ],
        scratch_shapes=[pltpu.VMEM((tm, d), BF16), pltpu.VMEM((n // tn, PREP_HALO, tn), F32)],
        compiler_params=_params("arbitrary", "arbitrary"),
        name="gdn_in_proj",
    )(x, nw, w, wg, w_conv)


def _tri_masks():
    ii = lax.broadcasted_iota(jnp.int32, (CHUNK, CHUNK), 0)
    jj = lax.broadcasted_iota(jnp.int32, (CHUNK, CHUNK), 1)
    return ii, jj


def _gdn_gates_kernel(g_ref, gt_ref, alc_ref, dtc_ref, alr_ref, dtr_ref,
                      beta_ref, cumc_ref, cumr_ref, *, nc, hg):
    nh = GDN_V_HEADS
    ii, jj = _tri_masks()
    tril = (ii >= jj).astype(F32)
    triu = (ii <= jj).astype(F32)
    neg_a_c = -jnp.exp(alc_ref[...])
    neg_a_r = -jnp.exp(alr_ref[...])
    for c in range(nc):
        blk = g_ref[c * CHUNK:(c + 1) * CHUNK, :]
        beta = jax.nn.sigmoid(blk[:, 0:nh])
        g = neg_a_c * _softplus(blk[:, nh:2 * nh] + dtc_ref[...])
        gr = neg_a_r * _softplus(gt_ref[:, c * CHUNK:(c + 1) * CHUNK] + dtr_ref[...])
        if c == 0:
            rows = lax.broadcasted_iota(jnp.int32, (CHUNK, nh), 0)
            cols = lax.broadcasted_iota(jnp.int32, (nh, CHUNK), 1)
            beta = jnp.where(rows >= ROW0, beta, 0.0)
            g = jnp.where(rows >= ROW0, g, 0.0)
            gr = jnp.where(cols >= ROW0, gr, 0.0)
        cum = jnp.dot(tril, g, preferred_element_type=F32, precision=lax.Precision.HIGHEST)
        cumr = jnp.dot(gr, triu, preferred_element_type=F32, precision=lax.Precision.HIGHEST)
        cumr = jnp.concatenate([cumr[:nh // 2], cumr[nh // 2:]], axis=1)
        for q in range(nh // hg):
            beta_ref[q, c] = beta[:, q * hg:(q + 1) * hg]
            cumc_ref[q, c] = cum[:, q * hg:(q + 1) * hg]
            cumr_ref[q, c] = cumr[q * hg // 2:(q + 1) * hg // 2, :]


def _gdn_gates(gates, a_log, dt_bias, hg):
    b, tp, _ = gates.shape
    nc = tp // CHUNK
    nh = GDN_V_HEADS
    ng = nh // hg
    kern = functools.partial(_gdn_gates_kernel, nc=nc, hg=hg)
    perm = np.concatenate([np.arange(0, nh, 2), np.arange(1, nh, 2)])
    gates_t = jnp.swapaxes(gates[:, :, nh:2 * nh], 1, 2)[:, perm, :]
    col = jax.ShapeDtypeStruct((b, ng, nc, CHUNK, hg), F32)
    row = jax.ShapeDtypeStruct((b, ng, nc, hg // 2, 2 * CHUNK), F32)
    col_spec = pl.BlockSpec((None, ng, nc, CHUNK, hg), lambda i: (i, 0, 0, 0, 0))
    row_spec = pl.BlockSpec((None, ng, nc, hg // 2, 2 * CHUNK), lambda i: (i, 0, 0, 0, 0))
    small = lambda shape: pl.BlockSpec(shape, lambda i: (0, 0))
    return pl.pallas_call(
        kern,
        grid=(b,),
        in_specs=[
            pl.BlockSpec((None, tp, LANES), lambda i: (i, 0, 0)),
            pl.BlockSpec((None, nh, tp), lambda i: (i, 0, 0)),
            small((1, nh)), small((1, nh)), small((nh, 1)), small((nh, 1)),
        ],
        out_specs=[col_spec, col_spec, row_spec],
        out_shape=[col, col, row],
        compiler_params=_params("parallel"),
        name="gdn_gates",
    )(gates, gates_t, a_log.reshape(1, nh), dt_bias.reshape(1, nh),
      a_log[perm].reshape(nh, 1), dt_bias[perm].reshape(nh, 1))


def _gdn_chunk_kernel(q_ref, k_ref, v_ref, z_ref, beta_ref, cumc_ref, cumr_ref, wn_ref, y_ref,
                      s_ref, u_ref, wq_ref, kd_ref, at_ref, *, nc, hg, cps):
    hd = GDN_HEAD_DIM
    iw = lax.broadcasted_iota(jnp.int32, (CHUNK, hd), 0)
    lane = lax.broadcasted_iota(jnp.int32, (CHUNK, hd), 1)
    jw = lane & (CHUNK - 1)
    first = lane < CHUNK
    causal = iw >= jw
    strict = iw > jw
    eye = (iw == jw).astype(F32)
    blk = [(iw >> l) == (jw >> l) for l in range(1, CHUNK.bit_length())]
    ring = [None] + [(((iw >> l) ^ (jw >> l)) == 1) for l in range(1, CHUNK.bit_length() - 1)]
    wn = wn_ref[...]
    heads = range(hg)
    pairs = range(hg // 2)

    def bdiag(x):
        zero = jnp.zeros_like(x)
        return jnp.concatenate([jnp.where(first, x, zero), jnp.where(first, zero, x)], axis=0)

    rstack = bdiag

    def prepare(chunks):
        units = [(i, p) for i in range(len(chunks)) for p in pairs]
        r0 = [pl.multiple_of(c * CHUNK, CHUNK) for c in chunks]
        cumc = [cumc_ref[c] for c in chunks]
        betac = [beta_ref[c] for c in chunks]
        cumr = [cumr_ref[c] for c in chunks]
        q = {(i, p): q_ref[pl.ds(r0[i], CHUNK), p * hd:(p + 1) * hd] for i, p in units}
        k = {(i, p): k_ref[pl.ds(r0[i], CHUNK), p * hd:(p + 1) * hd] for i, p in units}
        qkk = {u: _dot_nt(jnp.concatenate([q[u], k[u]], axis=0),
                          jnp.concatenate([k[u], k[u]], axis=0)) for u in units}
        cc = {(i, h): jnp.broadcast_to(cumc[i][:, h:h + 1], (CHUNK, hd))
              for i in range(len(chunks)) for h in heads}
        bc = {(i, h): jnp.broadcast_to(betac[i][:, h:h + 1], (CHUNK, hd))
              for i in range(len(chunks)) for h in heads}
        ccw = {(i, p): jnp.where(first, cc[i, 2 * p], cc[i, 2 * p + 1]) for i, p in units}
        bcw = {(i, p): jnp.where(first, bc[i, 2 * p], bc[i, 2 * p + 1]) for i, p in units}
        decay = {(i, p): jnp.exp(jnp.where(causal, ccw[i, p] - cumr[i][p:p + 1, :], -jnp.inf))
                 for i, p in units}
        a = {u: jnp.where(strict, bcw[u] * qkk[u][CHUNK:] * decay[u], 0.0) for u in units}
        for i, p in units:
            at_ref[i, p] = rstack((qkk[i, p][:CHUNK] * decay[i, p]).astype(BF16))
        t = {u: eye - jnp.where(blk[0], a[u], 0.0) for u in units}
        for lvl in range(1, len(blk)):
            a_off = {u: bdiag(jnp.where(ring[lvl], a[u], 0.0).astype(BF16)) for u in units}
            tb = {u: t[u].astype(BF16) for u in units}
            x = {u: _dot(tb[u], a_off[u]).astype(BF16) for u in units}
            t = {u: t[u] - _dot(x[u], bdiag(tb[u])) for u in units}
        ec = {u: jnp.exp(cc[u]) for u in cc}
        rhs = {}
        for i, p in units:
            kf = k[i, p].astype(F32)
            halves = []
            for h in (2 * p, 2 * p + 1):
                v = v_ref[pl.ds(r0[i], CHUNK), h * hd:(h + 1) * hd].astype(F32)
                halves.append(jnp.concatenate([(v * bc[i, h]).astype(BF16),
                                               (kf * (bc[i, h] * ec[i, h])).astype(BF16)], axis=1))
            rhs[i, p] = jnp.concatenate(halves, axis=0)
        sol = {u: _dot(rstack(t[u].astype(BF16)), rhs[u]) for u in units}
        for i, p in units:
            qf = q[i, p].astype(F32)
            kf = k[i, p].astype(F32)
            for r in range(2):
                h = 2 * p + r
                sh = sol[i, p][r * CHUNK:(r + 1) * CHUNK]
                u_ref[i, h] = sh[:, :hd]
                wq_ref[i, h] = jnp.concatenate(
                    [sh[:, hd:].astype(BF16), (qf * ec[i, h]).astype(BF16)], axis=0)
                kd_ref[i, h] = (kf * jnp.exp(cc[i, h][CHUNK - 1:CHUNK, :] - cc[i, h])).astype(BF16)

    def recur(c, slot, s):
        r0 = pl.multiple_of(c * CHUNK, CHUNK)
        cumc = cumc_ref[c]
        ws = [_dot(wq_ref[slot, h], s[h].astype(BF16)) for h in heads]
        v_new = [(u_ref[slot, h] - ws[h][:CHUNK]).astype(BF16) for h in heads]
        upd = [_dot_tn(kd_ref[slot, h], v_new[h]) for h in heads]
        av = [_dot(at_ref[slot, p], jnp.concatenate([v_new[2 * p], v_new[2 * p + 1]], axis=0))
              for p in pairs]
        s_new = [s[h] * jnp.exp(cumc[CHUNK - 1:CHUNK, h:h + 1]) + upd[h] for h in heads]
        for h in heads:
            o = ws[h][CHUNK:] + av[h // 2][(h % 2) * CHUNK:(h % 2 + 1) * CHUNK]
            z = z_ref[pl.ds(r0, CHUNK), h * hd:(h + 1) * hd].astype(F32)
            y = _norm_rows(o, wn) * _silu(z)
            y_ref[pl.ds(r0, CHUNK), h * hd:(h + 1) * hd] = y.astype(BF16)
        return s_new

    s_ref[...] = jnp.zeros_like(s_ref)
    prepare([jnp.int32(i) for i in range(min(cps, nc))])

    def body(it, carry):
        c0 = it * cps
        s = [s_ref[h] for h in heads]
        for i in range(cps):
            s = recur(c0 + i, i, s)
        for h in heads:
            s_ref[h] = s[h]
        prepare([jnp.minimum(c0 + cps + i, nc - 1) for i in range(cps)])
        return carry

    lax.fori_loop(0, nc // cps, body, 0)
    s = [s_ref[h] for h in heads]
    for i in range(nc % cps):
        s = recur(jnp.int32(nc - nc % cps + i), i, s)


def _gdn_chunk(qkv, z, beta, cumc, cumr, w_norm, hg):
    b, tp, _ = qkv.shape
    nc = tp // CHUNK
    hd = GDN_HEAD_DIM
    ng = GDN_V_HEADS // hg
    wqk = hd * hg // 2
    wv = hd * hg
    qk_dim = GDN_QK_HEADS * hd
    v_dim = GDN_V_HEADS * hd
    cps = GDN_CHUNKS_PER_STEP
    kern = functools.partial(_gdn_chunk_kernel, nc=nc, hg=hg, cps=cps)
    col_spec = pl.BlockSpec((None, None, nc, CHUNK, hg), lambda i, j: (i, j, 0, 0, 0))
    row_spec = pl.BlockSpec((None, None, nc, hg // 2, 2 * CHUNK), lambda i, j: (i, j, 0, 0, 0))
    return pl.pallas_call(
        kern,
        grid=(b, ng),
        in_specs=[
            pl.BlockSpec((None, tp, wqk), lambda i, j: (i, 0, j)),
            pl.BlockSpec((None, tp, wqk), lambda i, j: (i, 0, qk_dim // wqk + j)),
            pl.BlockSpec((None, tp, wv), lambda i, j: (i, 0, 2 * qk_dim // wv + j)),
            pl.BlockSpec((None, tp, wv), lambda i, j: (i, 0, j)),
            col_spec, col_spec, row_spec,
            pl.BlockSpec((1, hd), lambda i, j: (0, 0)),
        ],
        out_specs=pl.BlockSpec((None, tp, wv), lambda i, j: (i, 0, j)),
        out_shape=jax.ShapeDtypeStruct((b, tp, v_dim), BF16),
        scratch_shapes=[
            pltpu.VMEM((hg, hd, hd), F32),
            pltpu.VMEM((cps, hg, CHUNK, hd), F32),
            pltpu.VMEM((cps, hg, 2 * CHUNK, hd), BF16),
            pltpu.VMEM((cps, hg, CHUNK, hd), BF16),
            pltpu.VMEM((cps, hg // 2, 2 * CHUNK, hd), BF16),
        ],
        compiler_params=_params("parallel", "arbitrary"),
        name="gdn_chunk",
    )(qkv, qkv, qkv, z, beta, cumc, cumr, w_norm.reshape(1, hd))


def _gdn_layer(h, nw, w_in, w_conv, a_log, dt_bias, w_norm, w_out):
    b, tp, d = h.shape
    m = b * tp
    conv_dim = 2 * GDN_QK_HEADS * GDN_HEAD_DIM + GDN_V_HEADS * GDN_HEAD_DIM
    main = conv_dim + GDN_V_HEADS * GDN_HEAD_DIM
    hg = GDN_HEADS_PER_STEP
    w_qkv = w_in[:, :conv_dim].astype(BF16)
    w_z = w_in[:, conv_dim:main].astype(BF16)
    w_gate = jnp.pad(w_in[:, main:], ((0, 0), (0, LANES - 2 * GDN_V_HEADS))).astype(BF16)
    hn = h.reshape(m, d)
    qkv, gates = _gdn_in_proj(hn, nw.reshape(1, d), w_qkv, w_gate, w_conv, tp)
    z = _norm_matmul_plain(hn, nw.reshape(1, d), w_z)
    gates = gates.reshape(b, tp, LANES)
    beta, cumc, cumr = _gdn_gates(gates, a_log, dt_bias, hg)
    y = _gdn_chunk(qkv.reshape(b, tp, conv_dim), z.reshape(b, tp, -1), beta, cumc, cumr, w_norm, hg)
    return y.reshape(m, -1), w_out.astype(BF16)


def _mlstm_gates_kernel(g_ref, gt_ref, bc_ref, br_ref, cumc_ref, rc_ref, rr_ref, *, nc, hg):
    nh = MLSTM_HEADS
    ii, jj = _tri_masks()
    tril = (ii >= jj).astype(F32)
    triu = (ii <= jj).astype(F32)

    def split(raw, axis):
        capped = GATE_SOFTCAP * jnp.tanh(raw / GATE_SOFTCAP)
        if axis == 1:
            i_pre, f_pre = capped[:, :nh], capped[:, nh:2 * nh]
        else:
            i_pre, f_pre = capped[:nh], capped[nh:2 * nh]
        return i_pre, -_softplus(-f_pre)

    for c in range(nc):
        i_c, lf_c = split(g_ref[c * CHUNK:(c + 1) * CHUNK, 0:2 * nh] + bc_ref[...], 1)
        i_r, lf_r = split(gt_ref[:, c * CHUNK:(c + 1) * CHUNK] + br_ref[...], 0)
        if c == 0:
            rows = lax.broadcasted_iota(jnp.int32, (CHUNK, nh), 0)
            cols = lax.broadcasted_iota(jnp.int32, (nh, CHUNK), 1)
            i_c = jnp.where(rows >= ROW0, i_c, -jnp.inf)
            lf_c = jnp.where(rows >= ROW0, lf_c, 0.0)
            i_r = jnp.where(cols >= ROW0, i_r, -jnp.inf)
            lf_r = jnp.where(cols >= ROW0, lf_r, 0.0)
        cum = jnp.dot(tril, lf_c, preferred_element_type=F32, precision=lax.Precision.HIGHEST)
        cumr = jnp.dot(lf_r, triu, preferred_element_type=F32, precision=lax.Precision.HIGHEST)
        rc = i_c - cum
        rr = i_r - cumr
        for q in range(nh // hg):
            cumc_ref[q, c] = cum[:, q * hg:(q + 1) * hg]
            rc_ref[q, c] = rc[:, q * hg:(q + 1) * hg]
            rr_ref[q, c] = rr[q * hg:(q + 1) * hg, :]


def _mlstm_gates(gates, gates_t, bias, hg):
    b, tp, _ = gates.shape
    nc = tp // CHUNK
    nh = MLSTM_HEADS
    ng = nh // hg
    kern = functools.partial(_mlstm_gates_kernel, nc=nc, hg=hg)
    col = jax.ShapeDtypeStruct((b, ng, nc, CHUNK, hg), F32)
    row = jax.ShapeDtypeStruct((b, ng, nc, hg, CHUNK), F32)
    col_spec = pl.BlockSpec((None, ng, nc, CHUNK, hg), lambda i: (i, 0, 0, 0, 0))
    row_spec = pl.BlockSpec((None, ng, nc, hg, CHUNK), lambda i: (i, 0, 0, 0, 0))
    return pl.pallas_call(
        kern,
        grid=(b,),
        in_specs=[
            pl.BlockSpec((None, tp, LANES), lambda i: (i, 0, 0)),
            pl.BlockSpec((None, 2 * nh, tp), lambda i: (i, 0, 0)),
            pl.BlockSpec((1, 2 * nh), lambda i: (0, 0)),
            pl.BlockSpec((2 * nh, 1), lambda i: (0, 0)),
        ],
        out_specs=[col_spec, col_spec, row_spec],
        out_shape=[col, col, row],
        compiler_params=_params("parallel"),
        name="mlstm_gates",
    )(gates, gates_t, bias.reshape(1, 2 * nh), bias.reshape(2 * nh, 1))


def _mlstm_chunk_kernel(q_ref, k_ref, v_ref, og_ref, cumc_ref, rc_ref, rr_ref, wn_ref, y_ref,
                        c_ref, m_ref, *, nc, hg, cps):
    dv = MLSTM_DV
    ii, jj = _tri_masks()
    causal = ii >= jj
    ones_col = jnp.ones((CHUNK, LANES), BF16)
    heads = range(hg)

    c_ref[...] = jnp.zeros_like(c_ref)
    m_ref[...] = jnp.zeros_like(m_ref)

    def step(chunks):
        n = len(chunks)
        units = [(i, h) for i in range(n) for h in heads]
        r0 = [pl.multiple_of(c * CHUNK, CHUNK) for c in chunks]
        cumc = [cumc_ref[c] for c in chunks]
        rcol = [rc_ref[c] for c in chunks]
        rrow = [rr_ref[c] for c in chunks]
        q = {(i, h): q_ref[pl.ds(r0[i], CHUNK), h * LANES:(h + 1) * LANES] for i, h in units}
        k = {(i, h): k_ref[pl.ds(r0[i], CHUNK), h * LANES:(h + 1) * LANES] for i, h in units}
        v_aug = {(i, h): jnp.concatenate(
            [v_ref[pl.ds(r0[i], CHUNK), h * dv:(h + 1) * dv], ones_col], axis=1) for i, h in units}
        qk = {u: _dot_nt(q[u], k[u]) for u in units}
        cc = {(i, h): jnp.broadcast_to(cumc[i][:, h:h + 1], (CHUNK, LANES)) for i, h in units}
        rc = {(i, h): jnp.broadcast_to(rcol[i][:, h:h + 1], (CHUNK, LANES)) for i, h in units}
        rr = {(i, h): rrow[i][h:h + 1, :] for i, h in units}
        rmax = {u: jnp.max(jnp.where(causal, rr[u], -jnp.inf), axis=-1, keepdims=True) for u in units}
        m_in, keep, kw = {}, {}, {}
        m = [m_ref[h][0:1, :] for h in heads]
        for i, h in units:
            m_in[i, h] = m[h]
            c_last = cc[i, h][CHUNK - 1:CHUNK, :]
            log_keep = c_last + m[h]
            m_new = jnp.maximum(log_keep, c_last + jnp.max(rc[i, h], axis=0, keepdims=True))
            keep[i, h] = jnp.exp(log_keep - m_new)
            w_end = jnp.exp(c_last + rc[i, h] - m_new)
            kw[i, h] = (k[i, h].astype(F32) * w_end).astype(BF16)
            m[h] = m_new
        for h in heads:
            m_ref[h] = jnp.broadcast_to(m[h], m_ref.shape[1:])
        upd = {u: _dot_tn(kw[u], v_aug[u]) for u in units}
        g, w_intra = {}, {}
        for u in units:
            g[u] = jnp.maximum(m_in[u], rmax[u])
            w_intra[u] = (jnp.exp(jnp.where(causal, rr[u] - g[u][:, :CHUNK], -jnp.inf))
                          * qk[u]).astype(BF16)
        wv = {u: _dot(w_intra[u], v_aug[u]) for u in units}
        state = [c_ref[h] for h in heads]
        for i in range(n):
            qc = [_dot(q[i, h], state[h].astype(BF16)) for h in heads]
            for h in heads:
                s_inter = jnp.exp(m_in[i, h] - g[i, h])
                tot = jnp.concatenate([s_inter, s_inter], axis=1) * qc[h] + wv[i, h]
                den = tot[:, dv:]
                inv = 1.0 / jnp.maximum(jnp.abs(den), jnp.exp(-(cc[i, h] + g[i, h])))
                hs = tot[:, :dv] * inv
                og = og_ref[pl.ds(r0[i], CHUNK), h * dv:(h + 1) * dv].astype(F32)
                y = _norm_rows(hs, wn_ref[:, h * dv:(h + 1) * dv]) * jax.nn.sigmoid(og)
                y_ref[pl.ds(r0[i], CHUNK), h * dv:(h + 1) * dv] = y.astype(BF16)
                state[h] = jnp.concatenate([keep[i, h], keep[i, h]], axis=1) * state[h] + upd[i, h]
        for h in heads:
            c_ref[h] = state[h]

    def body(it, carry):
        step([it * cps + i for i in range(cps)])
        return carry

    lax.fori_loop(0, nc // cps, body, 0)
    if nc % cps:
        step([jnp.int32(nc - nc % cps + i) for i in range(nc % cps)])


def _mlstm_chunk(proj, cumc, rc, rr, w_norm, hg):
    b, tp, _ = proj.shape
    nc = tp // CHUNK
    nh = MLSTM_HEADS
    ng = nh // hg
    wb = LANES * hg
    kern = functools.partial(_mlstm_chunk_kernel, nc=nc, hg=hg, cps=MLSTM_CHUNKS_PER_STEP)
    col_spec = pl.BlockSpec((None, None, nc, CHUNK, hg), lambda i, j: (i, j, 0, 0, 0))
    row_spec = pl.BlockSpec((None, None, nc, hg, CHUNK), lambda i, j: (i, j, 0, 0, 0))
    return pl.pallas_call(
        kern,
        grid=(b, ng),
        in_specs=[
            pl.BlockSpec((None, tp, wb), lambda i, j: (i, 0, j)),
            pl.BlockSpec((None, tp, wb), lambda i, j: (i, 0, ng + j)),
            pl.BlockSpec((None, tp, wb), lambda i, j: (i, 0, 2 * ng + j)),
            pl.BlockSpec((None, tp, wb), lambda i, j: (i, 0, 3 * ng + j)),
            col_spec, col_spec, row_spec,
            pl.BlockSpec((1, wb), lambda i, j: (0, j)),
        ],
        out_specs=pl.BlockSpec((None, tp, wb), lambda i, j: (i, 0, j)),
        out_shape=jax.ShapeDtypeStruct((b, tp, nh * MLSTM_DV), BF16),
        scratch_shapes=[
            pltpu.VMEM((hg, LANES, 2 * MLSTM_DV), F32),
            pltpu.VMEM((hg, SUBLANES, LANES), F32),
        ],
        compiler_params=_params("parallel", "arbitrary"),
        name="mlstm_chunk",
    )(proj, proj, proj, proj, cumc, rc, rr, w_norm.reshape(1, nh * MLSTM_DV))


def _pad_heads(w, nh, dh):
    d = w.shape[0]
    w = w.reshape(d, nh, dh)
    return jnp.pad(w, ((0, 0), (0, 0), (0, LANES - dh))).reshape(d, nh * LANES)


def _mlstm_layer(h, nw, w_in, gate_bias, w_norm, w_out):
    b, tp, d = h.shape
    m = b * tp
    nh = MLSTM_HEADS
    qk = nh * MLSTM_DQK
    vd = nh * MLSTM_DV
    hg = MLSTM_HEADS_PER_STEP
    w_main = jnp.concatenate([
        _pad_heads(w_in[:, :qk], nh, MLSTM_DQK),
        _pad_heads(w_in[:, qk:2 * qk] * (MLSTM_DQK ** -0.5), nh, MLSTM_DQK),
        w_in[:, 2 * qk:2 * qk + 2 * vd],
    ], axis=1).astype(BF16)
    w_gate = jnp.pad(w_in[:, 2 * qk + 2 * vd:], ((0, 0), (0, LANES - 2 * nh))).astype(BF16)
    proj, gates = _norm_matmul(h.reshape(m, d), nw.reshape(1, d), w_main, w_gate)
    proj = proj.reshape(b, tp, -1)
    gates = gates.reshape(b, tp, LANES)
    gates_t = jnp.swapaxes(gates[:, :, :2 * nh], 1, 2)
    cumc, rc, rr = _mlstm_gates(gates, gates_t, gate_bias, hg)
    y = _mlstm_chunk(proj, cumc, rc, rr, w_norm, hg)
    return y.reshape(m, vd), w_out.astype(BF16)


def _mla_proj_kernel(h_ref, nw_ref, win_ref, qn_ref, wuq_ref, wuqs_ref, kvn_ref, wuk_ref, wuv_ref,
                     qtab_ref, ktab_ref, q_out, k_out, v_out):
    xn = _norm_rows(h_ref[...], nw_ref[...]).astype(BF16)
    c = _dot(xn, win_ref[...])
    cq = _norm_rows(c[:, :MLA_Q_RANK], qn_ref[...]).astype(BF16)
    lat = MLA_Q_RANK + MLA_KV_RANK
    ckv = _norm_rows(c[:, MLA_Q_RANK:lat], kvn_ref[...]).astype(BF16)
    kr = c[:, lat:lat + LANES]
    kr_swap = c[:, lat + LANES:]
    q = _dot(cq, wuq_ref[...])
    q_swap = _dot(cq, wuqs_ref[...])
    kn = _dot(ckv, wuk_ref[...])
    v_out[...] = _dot(ckv, wuv_ref[...]).astype(BF16)

    def head_norm_rope(x, x_swap, tab_ref):
        ms = jnp.sum(x * x, axis=-1, keepdims=True) * (1.0 / MLA_QK)
        return (x * tab_ref[0] + x_swap * tab_ref[1]) * lax.rsqrt(ms + RMS_EPS)

    for h in range(MLA_HEADS):
        sl = slice(h * LANES, (h + 1) * LANES)
        q_out[:, sl] = head_norm_rope(q[:, sl], q_swap[:, sl], qtab_ref).astype(BF16)
        k_out[:, sl] = head_norm_rope(kn[:, sl] + kr, kr_swap, ktab_ref).astype(BF16)


def _mla_proj(h, nw, w_in, q_norm, w_uq, w_uq_swap, kv_norm, w_uk, w_uv, qtab, ktab):
    b, tp, d = h.shape
    tt = _pick(tp, (704, 352, 192, 64))
    nh = MLA_HEADS
    full = lambda a: pl.BlockSpec(a.shape, lambda i, j: (0,) * a.ndim)
    tab = pl.BlockSpec((2, tt, LANES), lambda i, j: (0, j, 0))
    row = lambda n: pl.BlockSpec((None, tt, n), lambda i, j: (i, j, 0))
    return pl.pallas_call(
        _mla_proj_kernel,
        grid=(b, tp // tt),
        in_specs=[row(d), full(nw), full(w_in), full(q_norm), full(w_uq), full(w_uq_swap),
                  full(kv_norm), full(w_uk), full(w_uv), tab, tab],
        out_specs=[row(nh * LANES), row(nh * LANES), row(nh * MLA_V)],
        out_shape=[
            jax.ShapeDtypeStruct((b, tp, nh * LANES), BF16),
            jax.ShapeDtypeStruct((b, tp, nh * LANES), BF16),
            jax.ShapeDtypeStruct((b, tp, nh * MLA_V), BF16),
        ],
        compiler_params=_params("parallel", "arbitrary"),
        name="mla_proj",
    )(h, nw, w_in, q_norm, w_uq, w_uq_swap, kv_norm, w_uk, w_uv, qtab, ktab)


def _mla_attn_kernel(q_ref, k_ref, v_ref, o_ref, *, tq, nq):
    ri = lax.broadcasted_iota(jnp.int32, (tq, tq), 0)
    ci = lax.broadcasted_iota(jnp.int32, (tq, tq), 1)
    tri = jnp.where(ci <= ri, 0.0, NEG_BIG)
    tri0 = jnp.where(ci >= ROW0, tri, NEG_BIG)
    pad_row = jnp.where(lax.broadcasted_iota(jnp.int32, (1, max(nq - 1, 1) * tq), 1) >= ROW0, 0.0, NEG_BIG)
    lane = lax.broadcasted_iota(jnp.int32, (tq, LANES), 1)
    units = [(qi, r) for qi in range(nq) for r in range(2)]

    def scores(qi, r):
        lo = qi * tq
        hs = slice(r * LANES, (r + 1) * LANES)
        q = q_ref[lo:lo + tq, hs]
        s_diag = _dot_nt(q, k_ref[lo:lo + tq, hs]) + (tri if qi else tri0)
        s_main = _dot_nt(q, k_ref[0:lo, hs]) + pad_row[:, :lo] if qi else None
        return s_main, s_diag

    def attend(qi, s_main, s_diag):
        lo = qi * tq
        m = jnp.max(s_diag, axis=-1, keepdims=True)
        if qi:
            m = jnp.maximum(m, jnp.max(s_main, axis=-1, keepdims=True))
        p = jnp.exp(s_diag - m)
        l = jnp.sum(p, axis=-1, keepdims=True)
        o = _dot(p.astype(BF16), v_ref[lo:lo + tq, :])
        if qi:
            p = jnp.exp(s_main - m)
            l = l + jnp.sum(p, axis=-1, keepdims=True)
            o = o + _dot(p.astype(BF16), v_ref[0:lo, :])
        return o * (1.0 / l)

    nxt = scores(*units[0])
    outs = []
    for n, (qi, r) in enumerate(units):
        cur = nxt
        if n + 1 < len(units):
            nxt = scores(*units[n + 1])
        outs.append(attend(qi, *cur))
        if r == 1:
            o = jnp.where(lane < MLA_V, outs[0], outs[1])
            outs = []
            if qi == 0:
                o = jnp.where(lax.broadcasted_iota(jnp.int32, (tq, LANES), 0) >= ROW0, o, 0.0)
            o_ref[qi * tq:(qi + 1) * tq, :] = o.astype(BF16)


def _mla_attn(q, k, v):
    b, tp, _ = q.shape
    tq = _pick(tp, (352, 192, 64))
    npair = MLA_HEADS // 2
    kern = functools.partial(_mla_attn_kernel, tq=tq, nq=tp // tq)
    return pl.pallas_call(
        kern,
        grid=(b, npair),
        in_specs=[
            pl.BlockSpec((None, tp, 2 * LANES), lambda i, j: (i, 0, j)),
            pl.BlockSpec((None, tp, 2 * LANES), lambda i, j: (i, 0, j)),
            pl.BlockSpec((None, tp, LANES), lambda i, j: (i, 0, j)),
        ],
        out_specs=pl.BlockSpec((None, tp, LANES), lambda i, j: (i, 0, j)),
        out_shape=jax.ShapeDtypeStruct((b, tp, MLA_HEADS * MLA_V), BF16),
        compiler_params=_params("parallel", "arbitrary"),
        name="mla_attn",
    )(q, k, v)


def _swap_rope_halves(a):
    half = MLA_ROPE // 2
    lo, hi = MLA_NOPE, MLA_NOPE + half
    return jnp.concatenate([a[..., :lo], a[..., hi:hi + half], a[..., lo:hi], a[..., hi + half:]], axis=-1)


def _rope_tables(tp, gain, scale):
    half = MLA_ROPE // 2
    pos = jnp.arange(tp, dtype=F32) - float(ROW0)
    inv_freq = ROPE_THETA ** (-jnp.arange(0, MLA_ROPE, 2, dtype=F32) / MLA_ROPE)
    ang = pos[:, None] * inv_freq[None, :]
    cos, sin = jnp.cos(ang), jnp.sin(ang)
    ones = jnp.ones((tp, MLA_NOPE), F32)
    z_nope = jnp.zeros((tp, MLA_NOPE), F32)
    z_tail = jnp.zeros((tp, LANES - MLA_QK), F32)
    cos_t = jnp.concatenate([ones, cos, cos, z_tail], axis=1)
    sin_t = jnp.concatenate([z_nope, -sin, sin, z_tail], axis=1)
    g = jnp.pad(gain, (0, LANES - MLA_QK)).reshape(1, LANES) * scale
    return jnp.stack([cos_t * g, sin_t * _swap_rope_halves(g)])


def _mla_layer(h, nw, w_in, q_norm, w_uq, kv_norm, w_ukv, q_head_norm, k_head_norm, w_out):
    b, tp, d = h.shape
    m = b * tp
    nh = MLA_HEADS
    lat = MLA_Q_RANK + MLA_KV_RANK
    w_kr = jnp.concatenate([jnp.zeros((d, MLA_NOPE), F32), w_in[:, lat:],
                            jnp.zeros((d, LANES - MLA_QK), F32)], axis=1)
    w_in_p = jnp.concatenate([w_in[:, :lat], w_kr, _swap_rope_halves(w_kr)], axis=1).astype(BF16)
    w_uq_p = _pad_heads(w_uq, nh, MLA_QK)
    w_uq_swap = _swap_rope_halves(w_uq_p.reshape(MLA_Q_RANK, nh, LANES)).reshape(MLA_Q_RANK, nh * LANES)
    w_uq_p, w_uq_swap = w_uq_p.astype(BF16), w_uq_swap.astype(BF16)
    w_ukv3 = w_ukv.reshape(MLA_KV_RANK, nh, MLA_NOPE + MLA_V)
    w_uk_p = _pad_heads(w_ukv3[:, :, :MLA_NOPE].reshape(MLA_KV_RANK, nh * MLA_NOPE), nh, MLA_NOPE).astype(BF16)
    w_uv = w_ukv3[:, :, MLA_NOPE:].reshape(MLA_KV_RANK, nh * MLA_V).astype(BF16)
    q, k, v = _mla_proj(h, nw.reshape(1, d), w_in_p, q_norm.reshape(1, -1), w_uq_p, w_uq_swap,
                        kv_norm.reshape(1, -1), w_uk_p, w_uv,
                        _rope_tables(tp, q_head_norm, MLA_QK ** -0.5),
                        _rope_tables(tp, k_head_norm, 1.0))
    o = _mla_attn(q, k, v)
    return o.reshape(m, nh * MLA_V), w_out.astype(BF16)


def kernel(x, meta_tokens, attn_norm, ffn_norm, ff_up, ff_down, gdn_in, gdn_conv, gdn_a_log, gdn_dt_bias, gdn_norm, gdn_out, mlstm_in, mlstm_gate_bias, mlstm_norm, mlstm_out, mla_in, mla_q_norm, mla_uq, mla_kv_norm, mla_ukv, mla_q_head_norm, mla_k_head_norm, mla_out):
    b, t, d = x.shape
    depth = attn_norm.shape[0]
    meta = jnp.broadcast_to(meta_tokens[None].astype(x.dtype), (b, N_META, d))
    h = jnp.concatenate([jnp.zeros((b, LEAD_PAD, d), x.dtype), meta, x], axis=1)
    tp = h.shape[1]
    for layer in range(depth):
        kind, j = layer % N_MIXERS, layer // N_MIXERS
        if kind == 0:
            y, w_out = _gdn_layer(h, attn_norm[layer], gdn_in[j], gdn_conv[j], gdn_a_log[j],
                                  gdn_dt_bias[j], gdn_norm[j], gdn_out[j])
        elif kind == 1:
            y, w_out = _mlstm_layer(h, attn_norm[layer], mlstm_in[j], mlstm_gate_bias[j],
                                    mlstm_norm[j], mlstm_out[j])
        else:
            y, w_out = _mla_layer(h, attn_norm[layer], mla_in[j], mla_q_norm[j], mla_uq[j],
                                  mla_kv_norm[j], mla_ukv[j], mla_q_head_norm[j],
                                  mla_k_head_norm[j], mla_out[j])
        h = _out_mlp(h.reshape(b * tp, d), y, w_out, ffn_norm[layer].reshape(1, d),
                     ff_up[layer].astype(BF16), ff_down[layer].astype(BF16)).reshape(b, tp, d)
    return h[:, LEAD_PAD + N_META:]
```
